```python
import jax, jax.numpy as jnp
from jax import lax
import numpy as np

D_MODEL = 4096
BATCH = 2
SEQ = 8192
DEPTH = 1
DEC_BATCH = 2
DEC_SEQ = 4096
PAST_LEN = 128

N_MEM = 256
GLA_HEADS = 12
GLA_DK = 64
GLA_DV = 128
GLA_LR = 16
GLA_TAU = 16.0
GLA_CHUNK = 64
GLA_HK = GLA_HEADS * GLA_DK
GLA_W = GLA_HEADS * GLA_DV
MLA_HEADS = 12
MLA_Q_LORA = 1536
MLA_KV_LORA = 512
MLA_NOPE = 128
MLA_ROPE = 64
MLA_V = 128
MLA_BLOCK = 128
ROPE_THETA = 10000.0
MLA_W = MLA_HEADS * MLA_V
MEM_HEADS = 4
MEM_DH = 256
MEM_W = MEM_HEADS * MEM_DH
MIX = GLA_W + MLA_W + MEM_W
EPS = 1e-6
IN_SPLITS = (GLA_HK, GLA_HK, GLA_W, 2 * GLA_LR, GLA_W,
             MLA_Q_LORA, MLA_KV_LORA, MLA_ROPE, MLA_W,
             MEM_W, MEM_W)
N_IN = 2 * GLA_HK + 2 * GLA_W + 2 * GLA_LR + MLA_Q_LORA + MLA_KV_LORA + MLA_ROPE + MLA_W + 2 * MEM_W

kernel_name = 'hybrid_gla_mla_memory_encoder'


def rmsnorm(x, g):
    x32 = x.astype(jnp.float32)
    y = x32 * lax.rsqrt(jnp.mean(x32 * x32, axis=-1, keepdims=True) + EPS)
    return (y * g.astype(jnp.float32)).astype(x.dtype)


def gla_scan(q, k, v, logg, strict):
    bsz, t, h, dk = q.shape
    dv = v.shape[-1]
    n = t // GLA_CHUNK
    q = q.reshape(bsz, n, GLA_CHUNK, h, dk)
    k = k.reshape(bsz, n, GLA_CHUNK, h, dk)
    logg = logg.reshape(bsz, n, GLA_CHUNK, h, dk)
    v = v.reshape(bsz, n, GLA_CHUNK, h, dv)
    b = jnp.cumsum(logg, axis=2)
    q_in = q * jnp.exp(b)
    a = jnp.einsum('bnihd,bnjhd->bnhij', q_in, k * jnp.exp(-b))
    mask = jnp.tri(GLA_CHUNK, GLA_CHUNK, -1 if strict else 0, dtype=bool)
    a = jnp.where(mask, a, 0.0)
    o_intra = jnp.einsum('bnhij,bnjhv->bnihv', a, v)
    b_last = b[:, :, -1]
    kv = jnp.einsum('bnjhd,bnjhv->bnhdv', k * jnp.exp(b_last[:, :, None] - b), v)

    def step(s, inp):
        dec, kv_c = inp
        return dec[..., None] * s + kv_c, s

    s0 = jnp.zeros((bsz, h, dk, dv), jnp.float32)
    _, s_prev = lax.scan(step, s0, (jnp.moveaxis(jnp.exp(b_last), 1, 0), jnp.moveaxis(kv, 1, 0)))
    o_inter = jnp.einsum('bnihd,nbhdv->bnihv', q_in, s_prev)
    return (o_intra + o_inter).reshape(bsz, t, h, dv)


def gla_branch(q, k, v, lr, gate, w_g2, b_g, g_head):
    bsz, t, _ = q.shape
    f32 = jnp.float32
    qh = q.astype(f32).reshape(bsz, t, GLA_HEADS, GLA_DK) * (GLA_DK ** -0.5)
    kh = k.astype(f32).reshape(bsz, t, GLA_HEADS, GLA_DK)
    vh = v.astype(f32).reshape(bsz, t, GLA_HEADS, GLA_DV)
    z = jnp.einsum('btrl,rlk->btrk', lr.reshape(bsz, t, 2, GLA_LR), w_g2) + b_g
    logg = (jax.nn.log_sigmoid(z.astype(f32)) / GLA_TAU).reshape(bsz, t, 2, GLA_HEADS, GLA_DK)
    o_f = gla_scan(qh, kh, vh, logg[:, :, 0], False)
    flip = lambda a: jnp.flip(a, axis=1)
    o_b = flip(gla_scan(flip(qh), flip(kh), flip(vh), flip(logg[:, :, 1]), True))
    o = rmsnorm(o_f + o_b, g_head).reshape(bsz, t, GLA_W)
    return o.astype(q.dtype) * jax.nn.silu(gate)


def rope_tables(t):
    pos = jnp.arange(t, dtype=jnp.float32)
    inv = 1.0 / (ROPE_THETA ** (jnp.arange(0, MLA_ROPE, 2, dtype=jnp.float32) / MLA_ROPE))
    ang = pos[:, None] * inv[None, :]
    return jnp.cos(ang), jnp.sin(ang)


def apply_rope(x, cos, sin):
    half = MLA_ROPE // 2
    x32 = x.astype(jnp.float32)
    x1, x2 = x32[..., :half], x32[..., half:]
    return jnp.concatenate([x1 * cos - x2 * sin, x1 * sin + x2 * cos], axis=-1).astype(x.dtype)


def blocked_attention(q, k, v, scale):
    bsz, t, h, dq = q.shape
    dv = v.shape[-1]
    nb = t // MLA_BLOCK
    qb = q.reshape(bsz, nb, MLA_BLOCK, h, dq).transpose(1, 0, 2, 3, 4)

    def one(qi):
        s = jnp.einsum('bqhd,bkhd->bhqk', qi, k).astype(jnp.float32) * scale
        p = jax.nn.softmax(s, axis=-1).astype(v.dtype)
        return jnp.einsum('bhqk,bkhv->bqhv', p, v)

    o = lax.map(one, qb)
    return o.transpose(1, 0, 2, 3, 4).reshape(bsz, t, h, dv)


def mla_branch(c_q, c_kv, k_r, gate, g_q, w_uq, g_kv, w_ukv):
    bsz, t, _ = c_q.shape
    q = (rmsnorm(c_q, g_q) @ w_uq).reshape(bsz, t, MLA_HEADS, MLA_NOPE + MLA_ROPE)
    kv = (rmsnorm(c_kv, g_kv) @ w_ukv).reshape(bsz, t, MLA_HEADS, MLA_NOPE + MLA_V)
    cos, sin = rope_tables(t)
    q_rope = apply_rope(q[..., MLA_NOPE:], cos[:, None, :], sin[:, None, :])
    k_rope = apply_rope(k_r, cos, sin)
    q_full = jnp.concatenate([q[..., :MLA_NOPE], q_rope], axis=-1)
    k_full = jnp.concatenate(
        [kv[..., :MLA_NOPE], jnp.broadcast_to(k_rope[:, :, None, :], (bsz, t, MLA_HEADS, MLA_ROPE))], axis=-1)
    o = blocked_attention(q_full, k_full, kv[..., MLA_NOPE:], (MLA_NOPE + MLA_ROPE) ** -0.5)
    return o.reshape(bsz, t, MLA_W) * jax.nn.silu(gate)


def mem_branch(q, gate, mem, mem_g, mem_w_kv):
    bsz, t, _ = q.shape
    m = mem.shape[1]
    kv = (rmsnorm(mem, mem_g) @ mem_w_kv).reshape(bsz, m, 2, MEM_HEADS, MEM_DH)
    qh = q.reshape(bsz, t, MEM_HEADS, MEM_DH)
    s = jnp.einsum('bthd,bmhd->bhtm', qh, kv[:, :, 0]).astype(jnp.float32) * (MEM_DH ** -0.5)
    p = jax.nn.softmax(s, axis=-1).astype(q.dtype)
    o = jnp.einsum('bhtm,bmhd->bthd', p, kv[:, :, 1]).reshape(bsz, t, MEM_W)
    return o * jax.nn.silu(gate)


def encoder_layer(x, mem, g_in, w_in, gla_w_g2, gla_b_g, gla_g_head, mla_g_q, mla_w_uq,
                  mla_g_kv, mla_w_ukv, mem_g, mem_w_kv, w_out):
    h = rmsnorm(x, g_in)
    p = h @ w_in
    idx = [int(i) for i in np.cumsum(IN_SPLITS)[:-1]]
    (g_q, g_k, g_v, g_lr, g_gate, m_cq, m_ckv, m_kr, m_gate, c_q, c_gate) = jnp.split(p, idx, axis=-1)
    o_a = gla_branch(g_q, g_k, g_v, g_lr, g_gate, gla_w_g2, gla_b_g, gla_g_head)
    o_b = mla_branch(m_cq, m_ckv, m_kr, m_gate, mla_g_q, mla_w_uq, mla_g_kv, mla_w_ukv)
    o_c = mem_branch(c_q, c_gate, mem, mem_g, mem_w_kv)
    mixed = jnp.concatenate([o_a, o_b, o_c], axis=-1)
    return x + mixed @ w_out


def trunk(x, mem, g_in, w_in, gla_w_g2, gla_b_g, gla_g_head, mla_g_q, mla_w_uq,
          mla_g_kv, mla_w_ukv, mem_g, mem_w_kv, w_out, g_final):
    for l in range(DEPTH):
        x = encoder_layer(x, mem, g_in[l], w_in[l], gla_w_g2[l], gla_b_g[l], gla_g_head[l],
                          mla_g_q[l], mla_w_uq[l], mla_g_kv[l], mla_w_ukv[l],
                          mem_g[l], mem_w_kv[l], w_out[l])
    return rmsnorm(x, g_final)


def setup_inputs(seed: int = 0) -> dict:
    key = jax.random.key(seed)
    ks = jax.random.split(key, 20)
    f32 = jnp.float32
    nrm = lambda k, shape, scale: scale * jax.random.normal(k, shape, f32)
    gain = lambda k, shape: 1.0 + 0.02 * jax.random.normal(k, shape, f32)
    return {
        'x_prompt': nrm(ks[0], (BATCH, SEQ, D_MODEL), 1.0),
        'x_sample': nrm(ks[1], (DEC_BATCH, DEC_SEQ, D_MODEL), 1.0),
        'mem_prompt': nrm(ks[2], (BATCH, N_MEM, D_MODEL), 1.0),
        'mem_sample': nrm(ks[3], (DEC_BATCH, N_MEM, D_MODEL), 1.0),
        'g_in': gain(ks[4], (DEPTH, D_MODEL)),
        'w_in': nrm(ks[5], (DEPTH, D_MODEL, N_IN), D_MODEL ** -0.5),
        'gla_w_g2': nrm(ks[6], (DEPTH, 2, GLA_LR, GLA_HK), GLA_LR ** -0.5),
        'gla_b_g': nrm(ks[7], (DEPTH, 2, GLA_HK), 0.1),
        'gla_g_head': gain(ks[8], (DEPTH, GLA_DV)),
        'mla_g_q': gain(ks[9], (DEPTH, MLA_Q_LORA)),
        'mla_w_uq': nrm(ks[10], (DEPTH, MLA_Q_LORA, MLA_HEADS * (MLA_NOPE + MLA_ROPE)), MLA_Q_LORA ** -0.5),
        'mla_g_kv': gain(ks[11], (DEPTH, MLA_KV_LORA)),
        'mla_w_ukv': nrm(ks[12], (DEPTH, MLA_KV_LORA, MLA_HEADS * (MLA_NOPE + MLA_V)), MLA_KV_LORA ** -0.5),
        'mem_g': gain(ks[13], (DEPTH, D_MODEL)),
        'mem_w_kv': nrm(ks[14], (DEPTH, D_MODEL, 2 * MEM_W), D_MODEL ** -0.5),
        'w_out': nrm(ks[15], (DEPTH, MIX, D_MODEL), MIX ** -0.5),
        'g_final': gain(ks[16], (D_MODEL,)),
    }


def reference(x_prompt, x_sample, mem_prompt, mem_sample, g_in, w_in, gla_w_g2, gla_b_g,
              gla_g_head, mla_g_q, mla_w_uq, mla_g_kv, mla_w_ukv, mem_g, mem_w_kv, w_out, g_final):
    y_prompt = trunk(x_prompt, mem_prompt, g_in, w_in, gla_w_g2, gla_b_g, gla_g_head, mla_g_q,
                     mla_w_uq, mla_g_kv, mla_w_ukv, mem_g, mem_w_kv, w_out, g_final)
    y_sample = trunk(x_sample, mem_sample, g_in, w_in, gla_w_g2, gla_b_g, gla_g_head, mla_g_q,
                     mla_w_uq, mla_g_kv, mla_w_ukv, mem_g, mem_w_kv, w_out, g_final)
    return (y_prompt, y_sample)
```

```python
import functools

import jax
import jax.numpy as jnp
import numpy as np
from jax import lax
from jax.experimental import pallas as pl
from jax.experimental.pallas import tpu as pltpu

F32 = jnp.float32
BF16 = jnp.bfloat16

D_MODEL = 4096
N_MEM = 256
GLA_HEADS = 12
GLA_DK = 64
GLA_DV = 128
GLA_LR = 16
GLA_TAU = 16.0
GLA_CHUNK = 64
GLA_HK = GLA_HEADS * GLA_DK
GLA_W = GLA_HEADS * GLA_DV
MLA_HEADS = 12
MLA_Q_LORA = 1536
MLA_KV_LORA = 512
MLA_NOPE = 128
MLA_ROPE = 64
MLA_V = 128
ROPE_THETA = 10000.0
MLA_W = MLA_HEADS * MLA_V
MLA_QK_PAD = 256
MEM_HEADS = 4
MEM_DH = 256
MEM_W = MEM_HEADS * MEM_DH
EPS = 1e-6
LANES = 128

COL_G_V = 0
COL_G_GATE = 1536
COL_M_CQ = 3072
COL_M_GATE = 4608
COL_G_Q = 6144
COL_G_K = 6912
COL_M_CKV = 7680
COL_C_Q = 8192
COL_C_GATE = 9216
COL_G_LR = 10240
COL_M_KR = 10368
P_USED = 10496
P_TN = 512
P_COLS = 10752

VMEM_LIMIT = 56 * 1024 * 1024


def _cparams(n_axes):
    return pltpu.CompilerParams(
        dimension_semantics=("arbitrary",) * n_axes, vmem_limit_bytes=VMEM_LIMIT)


def _dot(a, b):
    return jnp.dot(a, b, preferred_element_type=F32)


def _dot_nt(a, b):
    return lax.dot_general(a, b, (((1,), (1,)), ((), ())), preferred_element_type=F32)


def _dot_tn(a, b):
    return lax.dot_general(a, b, (((0,), (0,)), ((), ())), preferred_element_type=F32)


def _silu(x):
    return x * jax.nn.sigmoid(x)


NORM_ROWS = 64


def _norm_matmul_kernel(x_ref, g_ref, w_ref, o_ref, h_ref):
    @pl.when(pl.program_id(1) == 0)
    def _():
        def rows(r, carry):
            r0 = pl.multiple_of(r * NORM_ROWS, NORM_ROWS)
            x = x_ref[pl.ds(r0, NORM_ROWS), :]
            ms = jnp.mean(x * x, axis=-1, keepdims=True)
            h_ref[pl.ds(r0, NORM_ROWS), :] = (x * lax.rsqrt(ms + EPS) * g_ref[...]).astype(BF16)
            return carry
        lax.fori_loop(0, x_ref.shape[0] // NORM_ROWS, rows, 0)

    o_ref[...] = _dot(h_ref[...], w_ref[...]).astype(o_ref.dtype)


def _norm_matmul(x, g, w, tm, tn):
    m, k = x.shape
    n = w.shape[1]
    return pl.pallas_call(
        _norm_matmul_kernel,
        grid=(m // tm, n // tn),
        in_specs=[
            pl.BlockSpec((tm, k), lambda i, j: (i, 0)),
            pl.BlockSpec((1, k), lambda i, j: (0, 0)),
            pl.BlockSpec((k, tn), lambda i, j: (0, j)),
        ],
        out_specs=pl.BlockSpec((tm, tn), lambda i, j: (i, j)),
        out_shape=jax.ShapeDtypeStruct((m, n), BF16),
        scratch_shapes=[pltpu.VMEM((tm, k), BF16)],
        compiler_params=_cparams(2),
        name="norm_matmul",
    )(x, g, w)


def _gla_kernel(rev, nchunk, *refs):
    if rev:
        (q_ref, k_ref, v_ref, lr_ref, wg_ref, bg_ref, gate_ref, of_ref, gh_ref,
         o_ref, s_ref) = refs
    else:
        q_ref, k_ref, v_ref, lr_ref, wg_ref, bg_ref, o_ref, s_ref = refs
    C = GLA_CHUNK

    @pl.when(pl.program_id(1) == 0)
    def _():
        s_ref[...] = jnp.zeros_like(s_ref)

    row = lax.broadcasted_iota(jnp.int32, (C, C), 0)
    col = lax.broadcasted_iota(jnp.int32, (C, C), 1)
    if rev:
        tri = col >= row
        amask = col > row
    else:
        tri = col <= row
        amask = col <= row
    tri_bf = jnp.where(tri, 1.0, 0.0).astype(BF16)
    q_lo = lax.broadcasted_iota(jnp.int32, (C, LANES), 1) < GLA_DK
    s_lo = lax.broadcasted_iota(jnp.int32, (GLA_DV, LANES), 1) < GLA_DK

    def chunk(ci, carry):
        c = (nchunk - 1 - ci) if rev else ci
        r0 = pl.multiple_of(c * C, C)
        rows = pl.ds(r0, C)
        z = _dot(lr_ref[rows, :], wg_ref[...]) + bg_ref[...]
        lg = (jnp.minimum(z, 0.0) - jnp.log1p(jnp.exp(-jnp.abs(z)))) * (1.0 / GLA_TAU)
        hi = lg.astype(BF16)
        lo = (lg - hi.astype(F32)).astype(BF16)
        b = _dot(tri_bf, hi) + _dot(tri_bf, lo)
        bl = b[0:1, :] if rev else b[C - 1:C, :]
        q = q_ref[rows, :].astype(F32)
        k = k_ref[rows, :].astype(F32)
        q_in = (q * jnp.exp(b) * (GLA_DK ** -0.5)).astype(BF16)
        k_out = (k * jnp.exp(-b)).astype(BF16)
        k_last = (k * jnp.exp(bl - b)).astype(BF16)
        dec = jnp.exp(bl)
        for p in range(GLA_HEADS // 2):
            sl = slice(p * LANES, (p + 1) * LANES)
            ql, ko, kl = q_in[:, sl], k_out[:, sl], k_last[:, sl]
            st = s_ref[p]
            st_bf = st.astype(BF16)
            kvs = []
            for half in range(2):
                h = 2 * p + half
                hs = slice(h * GLA_DV, (h + 1) * GLA_DV)
                keep = q_lo if half == 0 else jnp.logical_not(q_lo)
                qm = jnp.where(keep, ql, jnp.zeros_like(ql))
                a = jnp.where(amask, _dot_nt(qm, ko), 0.0).astype(BF16)
                vh = v_ref[rows, hs]
                o = _dot(a, vh) + _dot_nt(qm, st_bf)
                kvs.append(_dot_tn(vh, kl))
                if rev:
                    tot = o + of_ref[rows, hs]
                    ms = jnp.mean(tot * tot, axis=-1, keepdims=True)
                    y = tot * lax.rsqrt(ms + EPS) * gh_ref[...]
                    o_ref[rows, hs] = (y * _silu(gate_ref[rows, hs].astype(F32))).astype(o_ref.dtype)
                else:
                    o_ref[rows, hs] = o
            kv = jnp.where(s_lo, kvs[0], kvs[1])
            s_ref[p] = dec[:, sl] * st + kv
        return carry

    lax.fori_loop(0, nchunk, chunk, 0)


def _gla_call(p_act, wg, bg, rev, bsz, t, tb, extra=None):
    nblk = t // tb
    nchunk = tb // GLA_CHUNK

    def rowblk(b, i):
        return b * nblk + ((nblk - 1 - i) if rev else i)

    def act(width, colblk):
        return pl.BlockSpec((tb, width), lambda b, i: (rowblk(b, i), colblk))

    def full(shape):
        return pl.BlockSpec(shape, lambda b, i: (0,) * len(shape))

    in_specs = [
        act(GLA_HK, COL_G_Q // GLA_HK),
        act(GLA_HK, COL_G_K // GLA_HK),
        act(GLA_W, COL_G_V // GLA_W),
        act(LANES, COL_G_LR // LANES),
        full(wg.shape),
        full(bg.shape),
    ]
    args = [p_act, p_act, p_act, p_act, wg, bg]
    if rev:
        o_f, g_head = extra
        in_specs += [act(GLA_W, COL_G_GATE // GLA_W), act(GLA_W, 0), full(g_head.shape)]
        args += [p_act, o_f, g_head]
        out_dtype = BF16
    else:
        out_dtype = F32
    return pl.pallas_call(
        functools.partial(_gla_kernel, rev, nchunk),
        grid=(bsz, nblk),
        in_specs=in_specs,
        out_specs=act(GLA_W, 0),
        out_shape=jax.ShapeDtypeStruct((bsz * t, GLA_W), out_dtype),
        scratch_shapes=[pltpu.VMEM((GLA_HEADS // 2, GLA_DV, LANES), F32)],
        compiler_params=_cparams(2),
        name="gla_bwd" if rev else "gla_fwd",
    )(*args)


def _rope(r, cos_t, sin_t):
    quarter = MLA_ROPE // 2
    swapped = pltpu.roll(r, quarter, 1) + pltpu.roll(r, LANES - quarter, 1)
    return r * cos_t + swapped * sin_t


def _mla_prep_kernel(cq_ref, ckv_ref, kr_ref, gq_ref, gkv_ref, wq_ref, wk_ref, wv_ref,
                     cos_ref, sin_ref, q_ref, k_ref, v_ref):
    scale = (MLA_NOPE + MLA_ROPE) ** -0.5
    cos_t = cos_ref[...]
    sin_t = sin_ref[...]

    cq = cq_ref[...].astype(F32)
    ms = jnp.mean(cq * cq, axis=-1, keepdims=True)
    qn = (cq * lax.rsqrt(ms + EPS) * gq_ref[...]).astype(BF16)
    ckv = ckv_ref[...].astype(F32)
    ms = jnp.mean(ckv * ckv, axis=-1, keepdims=True)
    kvn = (ckv * lax.rsqrt(ms + EPS) * gkv_ref[...]).astype(BF16)

    k_rope = _rope(kr_ref[...].astype(F32), cos_t, sin_t).astype(BF16)
    k_nope = _dot(kvn, wk_ref[...]).astype(BF16)
    v_ref[...] = _dot(kvn, wv_ref[...]).astype(BF16)
    for h in range(MLA_HEADS):
        c0 = h * MLA_QK_PAD
        qh = _dot(qn, wq_ref[:, c0:c0 + MLA_QK_PAD])
        q_ref[:, c0:c0 + MLA_NOPE] = (qh[:, :MLA_NOPE] * scale).astype(BF16)
        q_ref[:, c0 + MLA_NOPE:c0 + MLA_QK_PAD] = (
            _rope(qh[:, MLA_NOPE:], cos_t, sin_t) * scale).astype(BF16)
        k_ref[:, c0:c0 + MLA_NOPE] = k_nope[:, h * MLA_NOPE:(h + 1) * MLA_NOPE]
        k_ref[:, c0 + MLA_NOPE:c0 + MLA_QK_PAD] = k_rope


def _mla_prep(p_act, g_q, g_kv, wq, wk, wv, cos_t, sin_t, t, tm):
    m = p_act.shape[0]
    nt = t // tm

    def act(width, colblk):
        return pl.BlockSpec((tm, width), lambda i: (i, colblk))

    def full(shape):
        return pl.BlockSpec(shape, lambda i: (0,) * len(shape))

    table = pl.BlockSpec((tm, LANES), lambda i: (i % nt, 0))
    qk_w = MLA_HEADS * MLA_QK_PAD
    return pl.pallas_call(
        _mla_prep_kernel,
        grid=(m // tm,),
        in_specs=[
            act(MLA_Q_LORA, COL_M_CQ // MLA_Q_LORA),
            act(MLA_KV_LORA, COL_M_CKV // MLA_KV_LORA),
            act(LANES, COL_M_KR // LANES),
            full(g_q.shape), full(g_kv.shape), full(wq.shape), full(wk.shape), full(wv.shape),
            table, table,
        ],
        out_specs=[act(qk_w, 0), act(qk_w, 0), act(MLA_W, 0)],
        out_shape=[
            jax.ShapeDtypeStruct((m, qk_w), BF16),
            jax.ShapeDtypeStruct((m, qk_w), BF16),
            jax.ShapeDtypeStruct((m, MLA_W), BF16),
        ],
        compiler_params=_cparams(1),
        name="mla_prep",
    )(p_act, p_act, p_act, g_q, g_kv, wq, wk, wv, cos_t, sin_t)


def _mla_attn_kernel(nkv, bk, q_ref, k_ref, v_ref, gate_ref, o_ref):
    q = q_ref[...]
    tq = q.shape[0]

    def body(j, carry):
        m, l, acc = carry
        r0 = pl.multiple_of(j * bk, bk)
        s = _dot_nt(q, k_ref[pl.ds(r0, bk), :])
        m_new = jnp.maximum(m, jnp.max(s, axis=-1, keepdims=True))
        alpha = jnp.exp(m - m_new)
        p = jnp.exp(s - m_new)
        l = alpha * l + jnp.sum(p, axis=-1, keepdims=True)
        acc = alpha * acc + _dot(p.astype(BF16), v_ref[pl.ds(r0, bk), :])
        return m_new, l, acc

    init = (jnp.full((tq, 1), -jnp.inf, F32), jnp.zeros((tq, 1), F32),
            jnp.zeros((tq, MLA_V), F32))
    _, l, acc = lax.fori_loop(0, nkv, body, init)
    o_ref[...] = (acc / l * _silu(gate_ref[...].astype(F32))).astype(o_ref.dtype)


def _mla_attn(q, k, v, p_act, bsz, t, tq, bk):
    nq = t // tq
    gate_col0 = COL_M_GATE // MLA_V
    return pl.pallas_call(
        functools.partial(_mla_attn_kernel, t // bk, bk),
        grid=(bsz, MLA_HEADS, nq),
        in_specs=[
            pl.BlockSpec((tq, MLA_QK_PAD), lambda b, h, i: (b * nq + i, h)),
            pl.BlockSpec((t, MLA_QK_PAD), lambda b, h, i: (b, h)),
            pl.BlockSpec((t, MLA_V), lambda b, h, i: (b, h)),
            pl.BlockSpec((tq, MLA_V), lambda b, h, i: (b * nq + i, gate_col0 + h)),
        ],
        out_specs=pl.BlockSpec((tq, MLA_V), lambda b, h, i: (b * nq + i, h)),
        out_shape=jax.ShapeDtypeStruct((bsz * t, MLA_W), BF16),
        compiler_params=_cparams(3),
        name="mla_attn",
    )(q, k, v, p_act)


def _mem_attn_kernel(q_ref, gate_ref, kv_ref, o_ref):
    for h in range(MEM_HEADS):
        hs = slice(h * MEM_DH, (h + 1) * MEM_DH)
        s = _dot_nt(q_ref[:, hs], kv_ref[:, hs]) * (MEM_DH ** -0.5)
        p = jnp.exp(s - jnp.max(s, axis=-1, keepdims=True))
        l = jnp.sum(p, axis=-1, keepdims=True)
        o = _dot(p.astype(BF16), kv_ref[:, MEM_W + h * MEM_DH:MEM_W + (h + 1) * MEM_DH]) / l
        o_ref[:, hs] = (o * _silu(gate_ref[:, hs].astype(F32))).astype(o_ref.dtype)


def _mem_attn(p_act, kv_mem, bsz, t, tq):
    nq = t // tq
    return pl.pallas_call(
        _mem_attn_kernel,
        grid=(bsz, nq),
        in_specs=[
            pl.BlockSpec((tq, MEM_W), lambda b, i: (b * nq + i, COL_C_Q // MEM_W)),
            pl.BlockSpec((tq, MEM_W), lambda b, i: (b * nq + i, COL_C_GATE // MEM_W)),
            pl.BlockSpec((N_MEM, 2 * MEM_W), lambda b, i: (b, 0)),
        ],
        out_specs=pl.BlockSpec((tq, MEM_W), lambda b, i: (b * nq + i, 0)),
        out_shape=jax.ShapeDtypeStruct((bsz * t, MEM_W), BF16),
        compiler_params=_cparams(2),
        name="mem_attn",
    )(p_act, p_act, kv_mem)


def _out_kernel(nj, tn, oa_ref, ob_ref, oc_ref, wa_ref, wb_ref, wc_ref, x_ref, g_ref,
                y_ref, ybuf_ref, ssq_ref):
    j = pl.program_id(1)
    acc = (_dot(oa_ref[...], wa_ref[...]) + _dot(ob_ref[...], wb_ref[...])
           + _dot(oc_ref[...], wc_ref[...]) + x_ref[...])
    ybuf_ref[j] = acc
    ssq = jnp.sum(acc * acc, axis=-1, keepdims=True)

    @pl.when(j == 0)
    def _():
        ssq_ref[...] = ssq

    @pl.when(j > 0)
    def _():
        ssq_ref[...] += ssq

    @pl.when(j == nj - 1)
    def _():
        r = lax.rsqrt(ssq_ref[...] * (1.0 / D_MODEL) + EPS)
        for jj in range(nj):
            cs = slice(jj * tn, (jj + 1) * tn)
            y_ref[:, cs] = ybuf_ref[jj] * r * g_ref[:, cs]


def _out_proj(o_a, o_b, o_c, w_out, x, g_final, tm, tn):
    m = x.shape[0]
    nj = D_MODEL // tn
    return pl.pallas_call(
        functools.partial(_out_kernel, nj, tn),
        grid=(m // tm, nj),
        in_specs=[
            pl.BlockSpec((tm, GLA_W), lambda i, j: (i, 0)),
            pl.BlockSpec((tm, MLA_W), lambda i, j: (i, 0)),
            pl.BlockSpec((tm, MEM_W), lambda i, j: (i, 0)),
            pl.BlockSpec((GLA_W, tn), lambda i, j: (0, j)),
            pl.BlockSpec((MLA_W, tn), lambda i, j: (GLA_W // MLA_W, j)),
            pl.BlockSpec((MEM_W, tn), lambda i, j: ((GLA_W + MLA_W) // MEM_W, j)),
            pl.BlockSpec((tm, tn), lambda i, j: (i, j)),
            pl.BlockSpec((1, D_MODEL), lambda i, j: (0, 0)),
        ],
        out_specs=pl.BlockSpec((tm, D_MODEL), lambda i, j: (i, 0)),
        out_shape=jax.ShapeDtypeStruct((m, D_MODEL), F32),
        scratch_shapes=[pltpu.VMEM((nj, tm, tn), F32), pltpu.VMEM((tm, 1), F32)],
        compiler_params=_cparams(2),
        name="out_proj",
    )(o_a, o_b, o_c, w_out, w_out, w_out, x, g_final)


def _prep_w_in(w_in):
    splits = (GLA_HK, GLA_HK, GLA_W, 2 * GLA_LR, GLA_W, MLA_Q_LORA, MLA_KV_LORA, MLA_ROPE,
              MLA_W, MEM_W, MEM_W)
    idx = [int(i) for i in np.cumsum(splits)[:-1]]
    (g_q, g_k, g_v, g_lr, g_gate, m_cq, m_ckv, m_kr, m_gate, c_q, c_gate) = jnp.split(
        w_in, idx, axis=-1)
    k = w_in.shape[0]
    zeros = lambda n: jnp.zeros((k, n), w_in.dtype)
    cols = [g_v, g_gate, m_cq, m_gate, g_q, g_k, m_ckv, c_q, c_gate,
            g_lr, zeros(LANES - 2 * GLA_LR), m_kr, zeros(LANES - MLA_ROPE), zeros(P_COLS - P_USED)]
    return jnp.concatenate(cols, axis=-1).astype(BF16)


def _prep_gla_gate(w_g2, b_g):
    wgs = []
    for d in range(2):
        w = jnp.zeros((LANES, GLA_HK), F32).at[d * GLA_LR:(d + 1) * GLA_LR].set(w_g2[d])
        wgs.append(w.astype(BF16))
    return wgs, [b_g[0][None, :], b_g[1][None, :]]


def _prep_mla(w_uq, w_ukv):
    wq = w_uq.reshape(MLA_Q_LORA, MLA_HEADS, MLA_NOPE + MLA_ROPE)
    wq = jnp.pad(wq, ((0, 0), (0, 0), (0, MLA_QK_PAD - MLA_NOPE - MLA_ROPE)))
    wq = wq.reshape(MLA_Q_LORA, MLA_HEADS * MLA_QK_PAD).astype(BF16)
    wkv = w_ukv.reshape(MLA_KV_LORA, MLA_HEADS, MLA_NOPE + MLA_V)
    wk = wkv[:, :, :MLA_NOPE].reshape(MLA_KV_LORA, MLA_HEADS * MLA_NOPE).astype(BF16)
    wv = wkv[:, :, MLA_NOPE:].reshape(MLA_KV_LORA, MLA_W).astype(BF16)
    return wq, wk, wv


def _rope_tables(t):
    pos = jnp.arange(t, dtype=F32)
    inv = 1.0 / (ROPE_THETA ** (jnp.arange(0, MLA_ROPE, 2, dtype=F32) / MLA_ROPE))
    ang = pos[:, None] * inv[None, :]
    cos, sin = jnp.cos(ang), jnp.sin(ang)
    z = jnp.zeros((t, LANES - MLA_ROPE), F32)
    return (jnp.concatenate([cos, cos, z], axis=-1), jnp.concatenate([-sin, sin, z], axis=-1))


def _pick(t, pref):
    return pref if t % pref == 0 else t


def _trunk(x, mem, w):
    bsz, t, _ = x.shape
    m = bsz * t
    xf = x.reshape(m, D_MODEL)
    p_act = _norm_matmul(xf, w["g_in"], w["w_in"], _pick(m, 512), P_TN)

    tb = _pick(t, 512)
    o_f = _gla_call(p_act, w["wg"][0], w["bg"][0], False, bsz, t, tb)
    o_a = _gla_call(p_act, w["wg"][1], w["bg"][1], True, bsz, t, tb, (o_f, w["g_head"]))

    cos_t, sin_t = _rope_tables(t)
    q, k, v = _mla_prep(p_act, w["g_q"], w["g_kv"], w["wq"], w["wk"], w["wv"], cos_t, sin_t,
                        t, _pick(t, 512))
    o_b = _mla_attn(q, k, v, p_act, bsz, t, _pick(t, 512), _pick(t, 512))

    kv_mem = _norm_matmul(mem.reshape(bsz * N_MEM, D_MODEL), w["mem_g"], w["mem_w_kv"],
                          bsz * N_MEM, 512)
    o_c = _mem_attn(p_act, kv_mem, bsz, t, _pick(t, 512))

    y = _out_proj(o_a, o_b, o_c, w["w_out"], xf, w["g_final"], _pick(m, 512), 512)
    return y.reshape(bsz, t, D_MODEL)


def kernel(x_prompt, x_sample, mem_prompt, mem_sample, g_in, w_in, gla_w_g2, gla_b_g, gla_g_head,
           mla_g_q, mla_w_uq, mla_g_kv, mla_w_ukv, mem_g, mem_w_kv, w_out, g_final):
    wg, bg = _prep_gla_gate(gla_w_g2[0], gla_b_g[0])
    wq, wk, wv = _prep_mla(mla_w_uq[0], mla_w_ukv[0])
    w = {
        "g_in": g_in[0][None, :],
        "w_in": _prep_w_in(w_in[0]),
        "wg": wg, "bg": bg,
        "g_head": gla_g_head[0][None, :],
        "g_q": mla_g_q[0][None, :], "g_kv": mla_g_kv[0][None, :],
        "wq": wq, "wk": wk, "wv": wv,
        "mem_g": mem_g[0][None, :],
        "mem_w_kv": mem_w_kv[0].astype(BF16),
        "w_out": w_out[0].astype(BF16),
        "g_final": g_final[None, :],
    }
    return (_trunk(x_prompt, mem_prompt, w), _trunk(x_sample, mem_sample, w))
```

```python
import functools

import jax
import jax.numpy as jnp
import numpy as np
from jax import lax
from jax.experimental import pallas as pl
from jax.experimental.pallas import tpu as pltpu

F32 = jnp.float32
BF16 = jnp.bfloat16

D_MODEL = 4096
N_MEM = 256
GLA_HEADS = 12
GLA_DK = 64
GLA_DV = 128
GLA_LR = 16
GLA_TAU = 16.0
GLA_CHUNK = 64
GLA_HK = GLA_HEADS * GLA_DK
GLA_W = GLA_HEADS * GLA_DV
MLA_HEADS = 12
MLA_Q_LORA = 1536
MLA_KV_LORA = 512
MLA_NOPE = 128
MLA_ROPE = 64
MLA_V = 128
ROPE_THETA = 10000.0
MLA_W = MLA_HEADS * MLA_V
MLA_QK_PAD = 256
MLA_V_PAD = 256
MEM_HEADS = 4
MEM_DH = 256
MEM_W = MEM_HEADS * MEM_DH
EPS = 1e-6
LANES = 128
LOG2E = 1.4426950408889634

COL_G_V = 0
COL_G_GATE = 1536
COL_M_CQ = 3072
COL_M_GATE = 4608
COL_G_Q = 6144
COL_G_K = 6912
COL_M_CKV = 7680
COL_C_Q = 8192
COL_C_GATE = 9216
COL_G_LR = 10240
COL_M_KR = 10368
P_USED = 10496
P_TN = 1536
P_COLS = 10752

VMEM_LIMIT = 56 * 1024 * 1024


def _cparams(n_axes):
    return pltpu.CompilerParams(
        dimension_semantics=("arbitrary",) * n_axes, vmem_limit_bytes=VMEM_LIMIT)


def _dot(a, b):
    return jnp.dot(a, b, preferred_element_type=F32)


def _dot_nt(a, b):
    return lax.dot_general(a, b, (((1,), (1,)), ((), ())), preferred_element_type=F32)


def _dot_tn(a, b):
    return lax.dot_general(a, b, (((0,), (0,)), ((), ())), preferred_element_type=F32)


def _silu(x):
    return x * jax.nn.sigmoid(x)


NORM_ROWS = 64


def _norm_matmul_kernel(x_ref, g_ref, w_ref, o_ref, h_ref):
    @pl.when(pl.program_id(1) == 0)
    def _():
        def rows(r, carry):
            r0 = pl.multiple_of(r * NORM_ROWS, NORM_ROWS)
            x = x_ref[pl.ds(r0, NORM_ROWS), :]
            ms = jnp.mean(x * x, axis=-1, keepdims=True)
            h_ref[pl.ds(r0, NORM_ROWS), :] = (x * lax.rsqrt(ms + EPS) * g_ref[...]).astype(BF16)
            return carry
        lax.fori_loop(0, x_ref.shape[0] // NORM_ROWS, rows, 0)

    o_ref[...] = _dot(h_ref[...], w_ref[...]).astype(o_ref.dtype)


def _norm_matmul(x, g, w, tm, tn):
    m, k = x.shape
    n = w.shape[1]
    return pl.pallas_call(
        _norm_matmul_kernel,
        grid=(m // tm, n // tn),
        in_specs=[
            pl.BlockSpec((tm, k), lambda i, j: (i, 0)),
            pl.BlockSpec((1, k), lambda i, j: (0, 0)),
            pl.BlockSpec((k, tn), lambda i, j: (0, j)),
        ],
        out_specs=pl.BlockSpec((tm, tn), lambda i, j: (i, j)),
        out_shape=jax.ShapeDtypeStruct((m, n), BF16),
        scratch_shapes=[pltpu.VMEM((tm, k), BF16)],
        compiler_params=_cparams(2),
        name="norm_matmul",
    )(x, g, w)


def _gla_kernel(rev, nchunk, *refs):
    if rev:
        (q_ref, k_ref, v_ref, lr_ref, wg_ref, bg_ref, gate_ref, of_ref, gh_ref,
         o_ref, s_ref) = refs
    else:
        q_ref, k_ref, v_ref, lr_ref, wg_ref, bg_ref, o_ref, s_ref = refs
    C = GLA_CHUNK

    @pl.when(pl.program_id(1) == 0)
    def _():
        s_ref[...] = jnp.zeros_like(s_ref)

    row = lax.broadcasted_iota(jnp.int32, (C, C), 0)
    col = lax.broadcasted_iota(jnp.int32, (C, C), 1)
    if rev:
        tri = col >= row
        amask = col > row
    else:
        tri = col <= row
        amask = col <= row
    tri_bf = jnp.where(tri, 1.0, 0.0).astype(BF16)
    q_lo = lax.broadcasted_iota(jnp.int32, (C, LANES), 1) < GLA_DK
    s_lo = lax.broadcasted_iota(jnp.int32, (GLA_DV, LANES), 1) < GLA_DK

    def chunk(ci, carry):
        c = (nchunk - 1 - ci) if rev else ci
        r0 = pl.multiple_of(c * C, C)
        rows = pl.ds(r0, C)
        z = _dot(lr_ref[rows, :], wg_ref[...]) + bg_ref[...]
        lg = (jnp.minimum(z, 0.0) - jnp.log1p(jnp.exp(-jnp.abs(z)))) * (1.0 / GLA_TAU)
        hi = lg.astype(BF16)
        lo = (lg - hi.astype(F32)).astype(BF16)
        b = _dot(tri_bf, hi) + _dot(tri_bf, lo)
        bl = b[0:1, :] if rev else b[C - 1:C, :]
        q = q_ref[rows, :].astype(F32)
        k = k_ref[rows, :].astype(F32)
        q_in = (q * jnp.exp(b) * (GLA_DK ** -0.5)).astype(BF16)
        k_out = (k * jnp.exp(-b)).astype(BF16)
        k_last = (k * jnp.exp(bl - b)).astype(BF16)
        dec = jnp.exp(bl)
        for p in range(GLA_HEADS // 2):
            sl = slice(p * LANES, (p + 1) * LANES)
            ql, ko, kl = q_in[:, sl], k_out[:, sl], k_last[:, sl]
            st = s_ref[p]
            st_bf = st.astype(BF16)
            kvs = []
            for half in range(2):
                h = 2 * p + half
                hs = slice(h * GLA_DV, (h + 1) * GLA_DV)
                keep = q_lo if half == 0 else jnp.logical_not(q_lo)
                qm = jnp.where(keep, ql, jnp.zeros_like(ql))
                a = jnp.where(amask, _dot_nt(qm, ko), 0.0).astype(BF16)
                vh = v_ref[rows, hs]
                o = _dot(a, vh) + _dot_nt(qm, st_bf)
                kvs.append(_dot_tn(vh, kl))
                if rev:
                    tot = o + of_ref[rows, hs]
                    ms = jnp.mean(tot * tot, axis=-1, keepdims=True)
                    y = tot * lax.rsqrt(ms + EPS) * gh_ref[...]
                    o_ref[rows, hs] = (y * _silu(gate_ref[rows, hs].astype(F32))).astype(o_ref.dtype)
                else:
                    o_ref[rows, hs] = o
            kv = jnp.where(s_lo, kvs[0], kvs[1])
            s_ref[p] = dec[:, sl] * st + kv
        return carry

    lax.fori_loop(0, nchunk, chunk, 0)


def _gla_call(p_act, wg, bg, rev, bsz, t, tb, extra=None):
    nblk = t // tb
    nchunk = tb // GLA_CHUNK

    def rowblk(b, i):
        return b * nblk + ((nblk - 1 - i) if rev else i)

    def act(width, colblk):
        return pl.BlockSpec((tb, width), lambda b, i: (rowblk(b, i), colblk))

    def full(shape):
        return pl.BlockSpec(shape, lambda b, i: (0,) * len(shape))

    in_specs = [
        act(GLA_HK, COL_G_Q // GLA_HK),
        act(GLA_HK, COL_G_K // GLA_HK),
        act(GLA_W, COL_G_V // GLA_W),
        act(LANES, COL_G_LR // LANES),
        full(wg.shape),
        full(bg.shape),
    ]
    args = [p_act, p_act, p_act, p_act, wg, bg]
    if rev:
        o_f, g_head = extra
        in_specs += [act(GLA_W, COL_G_GATE // GLA_W), act(GLA_W, 0), full(g_head.shape)]
        args += [p_act, o_f, g_head]
        out_dtype = BF16
    else:
        out_dtype = F32
    return pl.pallas_call(
        functools.partial(_gla_kernel, rev, nchunk),
        grid=(bsz, nblk),
        in_specs=in_specs,
        out_specs=act(GLA_W, 0),
        out_shape=jax.ShapeDtypeStruct((bsz * t, GLA_W), out_dtype),
        scratch_shapes=[pltpu.VMEM((GLA_HEADS // 2, GLA_DV, LANES), F32)],
        compiler_params=_cparams(2),
        name="gla_bwd" if rev else "gla_fwd",
    )(*args)


def _rope(r, cos_t, sin_t):
    quarter = MLA_ROPE // 2
    swapped = pltpu.roll(r, quarter, 1) + pltpu.roll(r, LANES - quarter, 1)
    return r * cos_t + swapped * sin_t


def _mla_prep_kernel(cq_ref, ckv_ref, kr_ref, gq_ref, gkv_ref, wq_ref, wk_ref, wv_ref,
                     cos_ref, sin_ref, q_ref, k_ref, v_ref):
    scale = (MLA_NOPE + MLA_ROPE) ** -0.5 * LOG2E
    cos_t = cos_ref[...]
    sin_t = sin_ref[...]

    cq = cq_ref[...].astype(F32)
    ms = jnp.mean(cq * cq, axis=-1, keepdims=True)
    qn = (cq * lax.rsqrt(ms + EPS) * gq_ref[...]).astype(BF16)
    ckv = ckv_ref[...].astype(F32)
    ms = jnp.mean(ckv * ckv, axis=-1, keepdims=True)
    kvn = (ckv * lax.rsqrt(ms + EPS) * gkv_ref[...]).astype(BF16)

    k_rope = _rope(kr_ref[...].astype(F32), cos_t, sin_t).astype(BF16)
    k_nope = _dot(kvn, wk_ref[...]).astype(BF16)
    v = _dot(kvn, wv_ref[...]).astype(BF16)
    ones_col = jnp.where(
        lax.broadcasted_iota(jnp.int32, (v.shape[0], MLA_V_PAD - MLA_V), 1) == 0,
        1.0, 0.0).astype(BF16)
    for h in range(MLA_HEADS):
        c0 = h * MLA_QK_PAD
        qh = _dot(qn, wq_ref[:, c0:c0 + MLA_QK_PAD])
        q_ref[:, c0:c0 + MLA_NOPE] = (qh[:, :MLA_NOPE] * scale).astype(BF16)
        q_ref[:, c0 + MLA_NOPE:c0 + MLA_QK_PAD] = (
            _rope(qh[:, MLA_NOPE:], cos_t, sin_t) * scale).astype(BF16)
        k_ref[:, c0:c0 + MLA_NOPE] = k_nope[:, h * MLA_NOPE:(h + 1) * MLA_NOPE]
        k_ref[:, c0 + MLA_NOPE:c0 + MLA_QK_PAD] = k_rope
        v0 = h * MLA_V_PAD
        v_ref[:, v0:v0 + MLA_V] = v[:, h * MLA_V:(h + 1) * MLA_V]
        v_ref[:, v0 + MLA_V:v0 + MLA_V_PAD] = ones_col


def _mla_prep(p_act, g_q, g_kv, wq, wk, wv, cos_t, sin_t, t, tm):
    m = p_act.shape[0]
    nt = t // tm

    def act(width, colblk):
        return pl.BlockSpec((tm, width), lambda i: (i, colblk))

    def full(shape):
        return pl.BlockSpec(shape, lambda i: (0,) * len(shape))

    table = pl.BlockSpec((tm, LANES), lambda i: (i % nt, 0))
    qk_w = MLA_HEADS * MLA_QK_PAD
    return pl.pallas_call(
        _mla_prep_kernel,
        grid=(m // tm,),
        in_specs=[
            act(MLA_Q_LORA, COL_M_CQ // MLA_Q_LORA),
            act(MLA_KV_LORA, COL_M_CKV // MLA_KV_LORA),
            act(LANES, COL_M_KR // LANES),
            full(g_q.shape), full(g_kv.shape), full(wq.shape), full(wk.shape), full(wv.shape),
            table, table,
        ],
        out_specs=[act(qk_w, 0), act(qk_w, 0), act(MLA_HEADS * MLA_V_PAD, 0)],
        out_shape=[
            jax.ShapeDtypeStruct((m, qk_w), BF16),
            jax.ShapeDtypeStruct((m, qk_w), BF16),
            jax.ShapeDtypeStruct((m, MLA_HEADS * MLA_V_PAD), BF16),
        ],
        compiler_params=_cparams(1),
        name="mla_prep",
    )(p_act, p_act, p_act, g_q, g_kv, wq, wk, wv, cos_t, sin_t)


def _mla_attn_kernel(nkv, bk, q_ref, k_ref, v_ref, gate_ref, o_ref, s_ref):
    q = q_ref[...]
    tq = q.shape[0]

    def scores(j):
        s_ref[j % 2] = _dot_nt(q, k_ref[j * bk:(j + 1) * bk, :])

    m = jnp.full((tq, 1), -jnp.inf, F32)
    acc = jnp.zeros((tq, MLA_V_PAD), F32)
    scores(0)
    for j in range(nkv):
        if j + 1 < nkv:
            scores(j + 1)
        s = s_ref[j % 2]
        m_new = jnp.maximum(m, jnp.max(s, axis=-1, keepdims=True))
        alpha = jnp.exp2(m - m_new)
        p = jnp.exp2(s - m_new).astype(BF16)
        acc = alpha * acc + _dot(p, v_ref[j * bk:(j + 1) * bk, :])
        m = m_new
    o = acc[:, :MLA_V] / acc[:, MLA_V:MLA_V + 1]
    o_ref[...] = (o * _silu(gate_ref[...].astype(F32))).astype(o_ref.dtype)


def _mla_attn(q, k, v, p_act, bsz, t, tq, bk):
    nq = t // tq
    nkv = t // bk
    gate_col0 = COL_M_GATE // MLA_V
    return pl.pallas_call(
        functools.partial(_mla_attn_kernel, nkv, bk),
        grid=(bsz, MLA_HEADS, nq),
        in_specs=[
            pl.BlockSpec((tq, MLA_QK_PAD), lambda b, h, i: (b * nq + i, h)),
            pl.BlockSpec((t, MLA_QK_PAD), lambda b, h, i: (b, h)),
            pl.BlockSpec((t, MLA_V_PAD), lambda b, h, i: (b, h)),
            pl.BlockSpec((tq, MLA_V), lambda b, h, i: (b * nq + i, gate_col0 + h)),
        ],
        out_specs=pl.BlockSpec((tq, MLA_V), lambda b, h, i: (b * nq + i, h)),
        out_shape=jax.ShapeDtypeStruct((bsz * t, MLA_W), BF16),
        scratch_shapes=[pltpu.VMEM((2, tq, bk), F32)],
        compiler_params=_cparams(3),
        name="mla_attn",
    )(q, k, v, p_act)


def _mem_attn_kernel(q_ref, gate_ref, kv_ref, o_ref):
    for h in range(MEM_HEADS):
        hs = slice(h * MEM_DH, (h + 1) * MEM_DH)
        s = _dot_nt(q_ref[:, hs], kv_ref[:, hs]) * (MEM_DH ** -0.5)
        p = jnp.exp(s - jnp.max(s, axis=-1, keepdims=True))
        l = jnp.sum(p, axis=-1, keepdims=True)
        o = _dot(p.astype(BF16), kv_ref[:, MEM_W + h * MEM_DH:MEM_W + (h + 1) * MEM_DH]) / l
        o_ref[:, hs] = (o * _silu(gate_ref[:, hs].astype(F32))).astype(o_ref.dtype)


def _mem_attn(p_act, kv_mem, bsz, t, tq):
    nq = t // tq
    return pl.pallas_call(
        _mem_attn_kernel,
        grid=(bsz, nq),
        in_specs=[
            pl.BlockSpec((tq, MEM_W), lambda b, i: (b * nq + i, COL_C_Q // MEM_W)),
            pl.BlockSpec((tq, MEM_W), lambda b, i: (b * nq + i, COL_C_GATE // MEM_W)),
            pl.BlockSpec((N_MEM, 2 * MEM_W), lambda b, i: (b, 0)),
        ],
        out_specs=pl.BlockSpec((tq, MEM_W), lambda b, i: (b * nq + i, 0)),
        out_shape=jax.ShapeDtypeStruct((bsz * t, MEM_W), BF16),
        compiler_params=_cparams(2),
        name="mem_attn",
    )(p_act, p_act, kv_mem)


def _out_kernel(nj, tn, oa_ref, ob_ref, oc_ref, wa_ref, wb_ref, wc_ref, x_ref, g_ref,
                y_ref, ssq_ref):
    j = pl.program_id(1)
    acc = (_dot(oa_ref[...], wa_ref[...]) + _dot(ob_ref[...], wb_ref[...])
           + _dot(oc_ref[...], wc_ref[...]) + x_ref[...])
    ssq = jnp.sum(acc * acc, axis=-1, keepdims=True)
    y_ref[:, pl.ds(pl.multiple_of(j * tn, tn), tn)] = acc

    @pl.when(j == 0)
    def _():
        ssq_ref[...] = ssq

    @pl.when(j > 0)
    def _():
        ssq_ref[...] += ssq

    @pl.when(j == nj - 1)
    def _():
        r = lax.rsqrt(ssq_ref[...] * (1.0 / D_MODEL) + EPS)
        rows = y_ref.shape[0]
        for r0 in range(0, rows, NORM_ROWS):
            rs = slice(r0, r0 + NORM_ROWS)
            y_ref[rs, :] = y_ref[rs, :] * r[rs] * g_ref[...]


def _out_proj(o_a, o_b, o_c, w_out, x, g_final, tm, tn):
    m = x.shape[0]
    nj = D_MODEL // tn
    return pl.pallas_call(
        functools.partial(_out_kernel, nj, tn),
        grid=(m // tm, nj),
        in_specs=[
            pl.BlockSpec((tm, GLA_W), lambda i, j: (i, 0)),
            pl.BlockSpec((tm, MLA_W), lambda i, j: (i, 0)),
            pl.BlockSpec((tm, MEM_W), lambda i, j: (i, 0)),
            pl.BlockSpec((GLA_W, tn), lambda i, j: (0, j)),
            pl.BlockSpec((MLA_W, tn), lambda i, j: (GLA_W // MLA_W, j)),
            pl.BlockSpec((MEM_W, tn), lambda i, j: ((GLA_W + MLA_W) // MEM_W, j)),
            pl.BlockSpec((tm, tn), lambda i, j: (i, j)),
            pl.BlockSpec((1, D_MODEL), lambda i, j: (0, 0)),
        ],
        out_specs=pl.BlockSpec((tm, D_MODEL), lambda i, j: (i, 0)),
        out_shape=jax.ShapeDtypeStruct((m, D_MODEL), F32),
        scratch_shapes=[pltpu.VMEM((tm, 1), F32)],
        compiler_params=_cparams(2),
        name="out_proj",
    )(o_a, o_b, o_c, w_out, w_out, w_out, x, g_final)


def _prep_w_in(w_in):
    splits = (GLA_HK, GLA_HK, GLA_W, 2 * GLA_LR, GLA_W, MLA_Q_LORA, MLA_KV_LORA, MLA_ROPE,
              MLA_W, MEM_W, MEM_W)
    idx = [int(i) for i in np.cumsum(splits)[:-1]]
    (g_q, g_k, g_v, g_lr, g_gate, m_cq, m_ckv, m_kr, m_gate, c_q, c_gate) = jnp.split(
        w_in, idx, axis=-1)
    k = w_in.shape[0]
    zeros = lambda n: jnp.zeros((k, n), w_in.dtype)
    cols = [g_v, g_gate, m_cq, m_gate, g_q, g_k, m_ckv, c_q, c_gate,
            g_lr, zeros(LANES - 2 * GLA_LR), m_kr, zeros(LANES - MLA_ROPE), zeros(P_COLS - P_USED)]
    return jnp.concatenate(cols, axis=-1).astype(BF16)


def _prep_gla_gate(w_g2, b_g):
    wgs = []
    for d in range(2):
        w = jnp.zeros((LANES, GLA_HK), F32).at[d * GLA_LR:(d + 1) * GLA_LR].set(w_g2[d])
        wgs.append(w.astype(BF16))
    return wgs, [b_g[0][None, :], b_g[1][None, :]]


def _prep_mla(w_uq, w_ukv):
    wq = w_uq.reshape(MLA_Q_LORA, MLA_HEADS, MLA_NOPE + MLA_ROPE)
    wq = jnp.pad(wq, ((0, 0), (0, 0), (0, MLA_QK_PAD - MLA_NOPE - MLA_ROPE)))
    wq = wq.reshape(MLA_Q_LORA, MLA_HEADS * MLA_QK_PAD).astype(BF16)
    wkv = w_ukv.reshape(MLA_KV_LORA, MLA_HEADS, MLA_NOPE + MLA_V)
    wk = wkv[:, :, :MLA_NOPE].reshape(MLA_KV_LORA, MLA_HEADS * MLA_NOPE).astype(BF16)
    wv = wkv[:, :, MLA_NOPE:].reshape(MLA_KV_LORA, MLA_W).astype(BF16)
    return wq, wk, wv


def _rope_tables(t):
    pos = jnp.arange(t, dtype=F32)
    inv = 1.0 / (ROPE_THETA ** (jnp.arange(0, MLA_ROPE, 2, dtype=F32) / MLA_ROPE))
    ang = pos[:, None] * inv[None, :]
    cos, sin = jnp.cos(ang), jnp.sin(ang)
    z = jnp.zeros((t, LANES - MLA_ROPE), F32)
    return (jnp.concatenate([cos, cos, z], axis=-1), jnp.concatenate([-sin, sin, z], axis=-1))


def _pick(t, pref):
    return pref if t % pref == 0 else t


def _trunk(x, mem, w):
    bsz, t, _ = x.shape
    m = bsz * t
    xf = x.reshape(m, D_MODEL)
    p_act = _norm_matmul(xf, w["g_in"], w["w_in"], _pick(m, 512), P_TN)

    tb = _pick(t, 512)
    o_f = _gla_call(p_act, w["wg"][0], w["bg"][0], False, bsz, t, tb)
    o_a = _gla_call(p_act, w["wg"][1], w["bg"][1], True, bsz, t, tb, (o_f, w["g_head"]))

    cos_t, sin_t = _rope_tables(t)
    q, k, v = _mla_prep(p_act, w["g_q"], w["g_kv"], w["wq"], w["wk"], w["wv"], cos_t, sin_t,
                        t, _pick(t, 512))
    o_b = _mla_attn(q, k, v, p_act, bsz, t, _pick(t, 512), _pick(t, 512))

    kv_mem = _norm_matmul(mem.reshape(bsz * N_MEM, D_MODEL), w["mem_g"], w["mem_w_kv"],
                          bsz * N_MEM, 512)
    o_c = _mem_attn(p_act, kv_mem, bsz, t, _pick(t, 512))

    y = _out_proj(o_a, o_b, o_c, w["w_out"], xf, w["g_final"], _pick(m, 512), 1024)
    return y.reshape(bsz, t, D_MODEL)


def kernel(x_prompt, x_sample, mem_prompt, mem_sample, g_in, w_in, gla_w_g2, gla_b_g, gla_g_head,
           mla_g_q, mla_w_uq, mla_g_kv, mla_w_ukv, mem_g, mem_w_kv, w_out, g_final):
    wg, bg = _prep_gla_gate(gla_w_g2[0], gla_b_g[0])
    wq, wk, wv = _prep_mla(mla_w_uq[0], mla_w_ukv[0])
    w = {
        "g_in": g_in[0][None, :],
        "w_in": _prep_w_in(w_in[0]),
        "wg": wg, "bg": bg,
        "g_head": gla_g_head[0][None, :],
        "g_q": mla_g_q[0][None, :], "g_kv": mla_g_kv[0][None, :],
        "wq": wq, "wk": wk, "wv": wv,
        "mem_g": mem_g[0][None, :],
        "mem_w_kv": mem_w_kv[0].astype(BF16),
        "w_out": w_out[0].astype(BF16),
        "g_final": g_final[None, :],
    }
    return (_trunk(x_prompt, mem_prompt, w), _trunk(x_sample, mem_sample, w))
```

```python
import functools

import jax
import jax.numpy as jnp
import numpy as np
from jax import lax
from jax.experimental import pallas as pl
from jax.experimental.pallas import tpu as pltpu

F32 = jnp.float32
BF16 = jnp.bfloat16

D_MODEL = 4096
N_MEM = 256
GLA_HEADS = 12
GLA_DK = 64
GLA_DV = 128
GLA_LR = 16
GLA_TAU = 16.0
GLA_CHUNK = 64
GLA_HK = GLA_HEADS * GLA_DK
GLA_W = GLA_HEADS * GLA_DV
GLA_UNROLL = 2
MLA_HEADS = 12
MLA_Q_LORA = 1536
MLA_KV_LORA = 512
MLA_NOPE = 128
MLA_ROPE = 64
MLA_V = 128
ROPE_THETA = 10000.0
MLA_W = MLA_HEADS * MLA_V
MLA_QK_PAD = 256
MLA_V_PAD = 256
MEM_HEADS = 4
MEM_DH = 256
MEM_W = MEM_HEADS * MEM_DH
EPS = 1e-6
LANES = 128
LOG2E = 1.4426950408889634

COL_G_V = 0
COL_G_GATE = 1536
COL_M_CQ = 3072
COL_M_GATE = 4608
COL_G_Q = 6144
COL_G_K = 6912
COL_M_CKV = 7680
COL_C_Q = 8192
COL_C_GATE = 9216
COL_G_LR = 10240
COL_M_KR = 10368
P_USED = 10496
P_TN = 1536
P_COLS = 10752

VMEM_LIMIT = 56 * 1024 * 1024


def _cparams(n_axes):
    return pltpu.CompilerParams(
        dimension_semantics=("arbitrary",) * n_axes, vmem_limit_bytes=VMEM_LIMIT)


def _dot(a, b):
    return jnp.dot(a, b, preferred_element_type=F32)


def _dot_nt(a, b):
    return lax.dot_general(a, b, (((1,), (1,)), ((), ())), preferred_element_type=F32)


def _dot_tn(a, b):
    return lax.dot_general(a, b, (((0,), (0,)), ((), ())), preferred_element_type=F32)


def _silu(x):
    return x * jax.nn.sigmoid(x)


NORM_ROWS = 64


def _norm_matmul_kernel(x_ref, g_ref, w_ref, o_ref, h_ref):
    @pl.when(pl.program_id(1) == 0)
    def _():
        def rows(r, carry):
            r0 = pl.multiple_of(r * NORM_ROWS, NORM_ROWS)
            x = x_ref[pl.ds(r0, NORM_ROWS), :]
            ms = jnp.mean(x * x, axis=-1, keepdims=True)
            h_ref[pl.ds(r0, NORM_ROWS), :] = (x * lax.rsqrt(ms + EPS) * g_ref[...]).astype(BF16)
            return carry
        lax.fori_loop(0, x_ref.shape[0] // NORM_ROWS, rows, 0)

    o_ref[...] = _dot(h_ref[...], w_ref[...]).astype(o_ref.dtype)


def _norm_matmul(x, g, w, tm, tn):
    m, k = x.shape
    n = w.shape[1]
    return pl.pallas_call(
        _norm_matmul_kernel,
        grid=(m // tm, n // tn),
        in_specs=[
            pl.BlockSpec((tm, k), lambda i, j: (i, 0)),
            pl.BlockSpec((1, k), lambda i, j: (0, 0)),
            pl.BlockSpec((k, tn), lambda i, j: (0, j)),
        ],
        out_specs=pl.BlockSpec((tm, tn), lambda i, j: (i, j)),
        out_shape=jax.ShapeDtypeStruct((m, n), BF16),
        scratch_shapes=[pltpu.VMEM((tm, k), BF16)],
        compiler_params=_cparams(2),
        name="norm_matmul",
    )(x, g, w)


def _gla_kernel(rev, nchunk, *refs):
    if rev:
        (q_ref, k_ref, v_ref, lr_ref, wg_ref, bg_ref, gate_ref, of_ref, gh_ref,
         o_ref, s_ref) = refs
    else:
        q_ref, k_ref, v_ref, lr_ref, wg_ref, bg_ref, o_ref, s_ref = refs
    C = GLA_CHUNK

    @pl.when(pl.program_id(1) == 0)
    def _():
        s_ref[...] = jnp.zeros_like(s_ref)

    row = lax.broadcasted_iota(jnp.int32, (C, C), 0)
    col = lax.broadcasted_iota(jnp.int32, (C, C), 1)
    tri = (col >= row) if rev else (col <= row)
    tri_bf = jnp.where(tri, 1.0, 0.0).astype(BF16)
    row2 = lax.broadcasted_iota(jnp.int32, (2 * C, 2 * C), 0)
    col2 = lax.broadcasted_iota(jnp.int32, (2 * C, 2 * C), 1)
    same_head = (row2 < C) == (col2 < C)
    amask = same_head & ((col2 > row2) if rev else (col2 <= row2))
    q_lo = lax.broadcasted_iota(jnp.int32, (C, LANES), 1) < GLA_DK

    npair = GLA_HEADS // 2

    def gate_logits(c):
        rows = pl.ds(c * C, C)
        z = _dot(lr_ref[rows, :], wg_ref[...]) + bg_ref[...]
        lg = (jnp.minimum(z, 0.0) - jnp.log1p(jnp.exp(-jnp.abs(z)))) * (1.0 / GLA_TAU)
        hi = lg.astype(BF16)
        return hi, (lg - hi.astype(F32)).astype(BF16)

    def decayed_qk(c, hi, lo):
        rows = pl.ds(c * C, C)
        b = _dot(tri_bf, hi) + _dot(tri_bf, lo)
        bl = b[0:1, :] if rev else b[C - 1:C, :]
        q = q_ref[rows, :].astype(F32)
        k = k_ref[rows, :].astype(F32)
        q_in = (q * jnp.exp(b) * (GLA_DK ** -0.5)).astype(BF16)
        k_out = (k * jnp.exp(-b)).astype(BF16)
        k_last = (k * jnp.exp(bl - b)).astype(BF16)
        return q_in, k_out, k_last, jnp.exp(bl)

    def intra(c, prep):
        rows = pl.ds(c * C, C)
        q_in, k_out, k_last, _ = prep
        out = []
        for p in range(npair):
            sl = slice(p * LANES, (p + 1) * LANES)
            ql, ko, kl = q_in[:, sl], k_out[:, sl], k_last[:, sl]
            zero = jnp.zeros_like(ql)
            qs = jnp.concatenate([jnp.where(q_lo, ql, zero), jnp.where(q_lo, zero, ql)], axis=0)
            ks = jnp.concatenate([jnp.where(q_lo, kl, zero), jnp.where(q_lo, zero, kl)], axis=0)
            vs = jnp.concatenate([v_ref[rows, (2 * p + i) * GLA_DV:(2 * p + i + 1) * GLA_DV]
                                  for i in range(2)], axis=0)
            a_raw = _dot_nt(qs, jnp.concatenate([ko, ko], axis=0))
            out.append((qs, vs, a_raw, _dot_tn(ks, vs)))
        return out

    def outputs(c, prep, pairs):
        rows = pl.ds(c * C, C)
        dec = prep[3]
        for p, (qs, vs, a_raw, kv) in enumerate(pairs):
            a = jnp.where(amask, a_raw, 0.0).astype(BF16)
            st = s_ref[p]
            o2 = _dot(jnp.concatenate([a, qs], axis=1),
                      jnp.concatenate([vs, st.astype(BF16)], axis=0))
            dec_col = jnp.broadcast_to(dec[:, p * LANES:(p + 1) * LANES], (LANES, LANES)).T
            s_ref[p] = dec_col * st + kv
            for i in range(2):
                hs = slice((2 * p + i) * GLA_DV, (2 * p + i + 1) * GLA_DV)
                o = o2[i * C:(i + 1) * C]
                if rev:
                    tot = o + of_ref[rows, hs]
                    ms = jnp.mean(tot * tot, axis=-1, keepdims=True)
                    y = tot * lax.rsqrt(ms + EPS) * gh_ref[...]
                    o_ref[rows, hs] = (y * _silu(gate_ref[rows, hs].astype(F32))).astype(o_ref.dtype)
                else:
                    o_ref[rows, hs] = o

    order = [(nchunk - 1 - i) if rev else i for i in range(nchunk)]
    prep = decayed_qk(order[0], *gate_logits(order[0]))
    for i, c in enumerate(order):
        nxt = order[i + 1] if i + 1 < nchunk else None
        if nxt is not None:
            hi_lo = gate_logits(nxt)
        pairs = intra(c, prep)
        if nxt is not None:
            prep_next = decayed_qk(nxt, *hi_lo)
        outputs(c, prep, pairs)
        if nxt is not None:
            prep = prep_next


def _gla_call(p_act, wg, bg, rev, bsz, t, tb, extra=None):
    nblk = t // tb
    nchunk = tb // GLA_CHUNK

    def rowblk(b, i):
        return b * nblk + ((nblk - 1 - i) if rev else i)

    def act(width, colblk):
        return pl.BlockSpec((tb, width), lambda b, i: (rowblk(b, i), colblk))

    def full(shape):
        return pl.BlockSpec(shape, lambda b, i: (0,) * len(shape))

    in_specs = [
        act(GLA_HK, COL_G_Q // GLA_HK),
        act(GLA_HK, COL_G_K // GLA_HK),
        act(GLA_W, COL_G_V // GLA_W),
        act(LANES, COL_G_LR // LANES),
        full(wg.shape),
        full(bg.shape),
    ]
    args = [p_act, p_act, p_act, p_act, wg, bg]
    if rev:
        o_f, g_head = extra
        in_specs += [act(GLA_W, COL_G_GATE // GLA_W), act(GLA_W, 0), full(g_head.shape)]
        args += [p_act, o_f, g_head]
        out_dtype = BF16
    else:
        out_dtype = F32
    return pl.pallas_call(
        functools.partial(_gla_kernel, rev, nchunk),
        grid=(bsz, nblk),
        in_specs=in_specs,
        out_specs=act(GLA_W, 0),
        out_shape=jax.ShapeDtypeStruct((bsz * t, GLA_W), out_dtype),
        scratch_shapes=[pltpu.VMEM((GLA_HEADS // 2, GLA_DV, LANES), F32)],
        compiler_params=_cparams(2),
        name="gla_bwd" if rev else "gla_fwd",
    )(*args)


def _rope(r, cos_t, sin_t):
    quarter = MLA_ROPE // 2
    swapped = pltpu.roll(r, quarter, 1) + pltpu.roll(r, LANES - quarter, 1)
    return r * cos_t + swapped * sin_t


def _mla_prep_kernel(cq_ref, ckv_ref, kr_ref, gq_ref, gkv_ref, wq_ref, wk_ref, wv_ref,
                     cos_ref, sin_ref, q_ref, k_ref, v_ref):
    scale = (MLA_NOPE + MLA_ROPE) ** -0.5 * LOG2E
    cos_t = cos_ref[...]
    sin_t = sin_ref[...]

    cq = cq_ref[...].astype(F32)
    ms = jnp.mean(cq * cq, axis=-1, keepdims=True)
    qn = (cq * lax.rsqrt(ms + EPS) * gq_ref[...]).astype(BF16)
    ckv = ckv_ref[...].astype(F32)
    ms = jnp.mean(ckv * ckv, axis=-1, keepdims=True)
    kvn = (ckv * lax.rsqrt(ms + EPS) * gkv_ref[...]).astype(BF16)

    k_rope = _rope(kr_ref[...].astype(F32), cos_t, sin_t).astype(BF16)
    k_nope = _dot(kvn, wk_ref[...]).astype(BF16)
    v = _dot(kvn, wv_ref[...]).astype(BF16)
    ones_col = jnp.where(
        lax.broadcasted_iota(jnp.int32, (v.shape[0], MLA_V_PAD - MLA_V), 1) == 0,
        1.0, 0.0).astype(BF16)
    for h in range(MLA_HEADS):
        c0 = h * MLA_QK_PAD
        qh = _dot(qn, wq_ref[:, c0:c0 + MLA_QK_PAD])
        q_ref[:, c0:c0 + MLA_NOPE] = (qh[:, :MLA_NOPE] * scale).astype(BF16)
        q_ref[:, c0 + MLA_NOPE:c0 + MLA_QK_PAD] = (
            _rope(qh[:, MLA_NOPE:], cos_t, sin_t) * scale).astype(BF16)
        k_ref[:, c0:c0 + MLA_NOPE] = k_nope[:, h * MLA_NOPE:(h + 1) * MLA_NOPE]
        k_ref[:, c0 + MLA_NOPE:c0 + MLA_QK_PAD] = k_rope
        v0 = h * MLA_V_PAD
        v_ref[:, v0:v0 + MLA_V] = v[:, h * MLA_V:(h + 1) * MLA_V]
        v_ref[:, v0 + MLA_V:v0 + MLA_V_PAD] = ones_col


def _mla_prep(p_act, g_q, g_kv, wq, wk, wv, cos_t, sin_t, t, tm):
    m = p_act.shape[0]
    nt = t // tm

    def act(width, colblk):
        return pl.BlockSpec((tm, width), lambda i: (i, colblk))

    def full(shape):
        return pl.BlockSpec(shape, lambda i: (0,) * len(shape))

    table = pl.BlockSpec((tm, LANES), lambda i: (i % nt, 0))
    qk_w = MLA_HEADS * MLA_QK_PAD
    return pl.pallas_call(
        _mla_prep_kernel,
        grid=(m // tm,),
        in_specs=[
            act(MLA_Q_LORA, COL_M_CQ // MLA_Q_LORA),
            act(MLA_KV_LORA, COL_M_CKV // MLA_KV_LORA),
            act(LANES, COL_M_KR // LANES),
            full(g_q.shape), full(g_kv.shape), full(wq.shape), full(wk.shape), full(wv.shape),
            table, table,
        ],
        out_specs=[act(qk_w, 0), act(qk_w, 0), act(MLA_HEADS * MLA_V_PAD, 0)],
        out_shape=[
            jax.ShapeDtypeStruct((m, qk_w), BF16),
            jax.ShapeDtypeStruct((m, qk_w), BF16),
            jax.ShapeDtypeStruct((m, MLA_HEADS * MLA_V_PAD), BF16),
        ],
        compiler_params=_cparams(1),
        name="mla_prep",
    )(p_act, p_act, p_act, g_q, g_kv, wq, wk, wv, cos_t, sin_t)


def _mla_attn_kernel(nkv, bk, q_ref, k_ref, v_ref, gate_ref, o_ref, s_ref):
    q = q_ref[...]
    tq = q.shape[0]

    def scores(j):
        s_ref[j % 2] = _dot_nt(q, k_ref[j * bk:(j + 1) * bk, :])

    m = jnp.full((tq, 1), -jnp.inf, F32)
    acc = jnp.zeros((tq, MLA_V_PAD), F32)
    scores(0)
    for j in range(nkv):
        if j + 1 < nkv:
            scores(j + 1)
        s = s_ref[j % 2]
        m_new = jnp.maximum(m, jnp.max(s, axis=-1, keepdims=True))
        alpha = jnp.exp2(m - m_new)
        p = jnp.exp2(s - m_new).astype(BF16)
        acc = alpha * acc + _dot(p, v_ref[j * bk:(j + 1) * bk, :])
        m = m_new
    o = acc[:, :MLA_V] / acc[:, MLA_V:MLA_V + 1]
    o_ref[...] = (o * _silu(gate_ref[...].astype(F32))).astype(o_ref.dtype)


def _mla_attn(q, k, v, p_act, bsz, t, tq, bk):
    nq = t // tq
    nkv = t // bk
    gate_col0 = COL_M_GATE // MLA_V
    return pl.pallas_call(
        functools.partial(_mla_attn_kernel, nkv, bk),
        grid=(bsz, MLA_HEADS, nq),
        in_specs=[
            pl.BlockSpec((tq, MLA_QK_PAD), lambda b, h, i: (b * nq + i, h)),
            pl.BlockSpec((t, MLA_QK_PAD), lambda b, h, i: (b, h)),
            pl.BlockSpec((t, MLA_V_PAD), lambda b, h, i: (b, h)),
            pl.BlockSpec((tq, MLA_V), lambda b, h, i: (b * nq + i, gate_col0 + h)),
        ],
        out_specs=pl.BlockSpec((tq, MLA_V), lambda b, h, i: (b * nq + i, h)),
        out_shape=jax.ShapeDtypeStruct((bsz * t, MLA_W), BF16),
        scratch_shapes=[pltpu.VMEM((2, tq, bk), F32)],
        compiler_params=_cparams(3),
        name="mla_attn",
    )(q, k, v, p_act)


def _mem_attn_kernel(q_ref, gate_ref, kv_ref, o_ref):
    for h in range(MEM_HEADS):
        hs = slice(h * MEM_DH, (h + 1) * MEM_DH)
        s = _dot_nt(q_ref[:, hs], kv_ref[:, hs]) * (MEM_DH ** -0.5)
        p = jnp.exp(s - jnp.max(s, axis=-1, keepdims=True))
        l = jnp.sum(p, axis=-1, keepdims=True)
        o = _dot(p.astype(BF16), kv_ref[:, MEM_W + h * MEM_DH:MEM_W + (h + 1) * MEM_DH]) / l
        o_ref[:, hs] = (o * _silu(gate_ref[:, hs].astype(F32))).astype(o_ref.dtype)


def _mem_attn(p_act, kv_mem, bsz, t, tq):
    nq = t // tq
    return pl.pallas_call(
        _mem_attn_kernel,
        grid=(bsz, nq),
        in_specs=[
            pl.BlockSpec((tq, MEM_W), lambda b, i: (b * nq + i, COL_C_Q // MEM_W)),
            pl.BlockSpec((tq, MEM_W), lambda b, i: (b * nq + i, COL_C_GATE // MEM_W)),
            pl.BlockSpec((N_MEM, 2 * MEM_W), lambda b, i: (b, 0)),
        ],
        out_specs=pl.BlockSpec((tq, MEM_W), lambda b, i: (b * nq + i, 0)),
        out_shape=jax.ShapeDtypeStruct((bsz * t, MEM_W), BF16),
        compiler_params=_cparams(2),
        name="mem_attn",
    )(p_act, p_act, kv_mem)


def _out_kernel(nj, tn, oa_ref, ob_ref, oc_ref, wa_ref, wb_ref, wc_ref, x_ref, g_ref,
                y_ref, ssq_ref):
    j = pl.program_id(1)
    acc = (_dot(oa_ref[...], wa_ref[...]) + _dot(ob_ref[...], wb_ref[...])
           + _dot(oc_ref[...], wc_ref[...]) + x_ref[...])
    ssq = jnp.sum(acc * acc, axis=-1, keepdims=True)
    y_ref[:, pl.ds(pl.multiple_of(j * tn, tn), tn)] = acc

    @pl.when(j == 0)
    def _():
        ssq_ref[...] = ssq

    @pl.when(j > 0)
    def _():
        ssq_ref[...] += ssq

    @pl.when(j == nj - 1)
    def _():
        r = lax.rsqrt(ssq_ref[...] * (1.0 / D_MODEL) + EPS)
        rows = y_ref.shape[0]
        for r0 in range(0, rows, NORM_ROWS):
            rs = slice(r0, r0 + NORM_ROWS)
            y_ref[rs, :] = y_ref[rs, :] * r[rs] * g_ref[...]


def _out_proj(o_a, o_b, o_c, w_out, x, g_final, tm, tn):
    m = x.shape[0]
    nj = D_MODEL // tn
    return pl.pallas_call(
        functools.partial(_out_kernel, nj, tn),
        grid=(m // tm, nj),
        in_specs=[
            pl.BlockSpec((tm, GLA_W), lambda i, j: (i, 0)),
            pl.BlockSpec((tm, MLA_W), lambda i, j: (i, 0)),
            pl.BlockSpec((tm, MEM_W), lambda i, j: (i, 0)),
            pl.BlockSpec((GLA_W, tn), lambda i, j: (0, j)),
            pl.BlockSpec((MLA_W, tn), lambda i, j: (GLA_W // MLA_W, j)),
            pl.BlockSpec((MEM_W, tn), lambda i, j: ((GLA_W + MLA_W) // MEM_W, j)),
            pl.BlockSpec((tm, tn), lambda i, j: (i, j)),
            pl.BlockSpec((1, D_MODEL), lambda i, j: (0, 0)),
        ],
        out_specs=pl.BlockSpec((tm, D_MODEL), lambda i, j: (i, 0)),
        out_shape=jax.ShapeDtypeStruct((m, D_MODEL), F32),
        scratch_shapes=[pltpu.VMEM((tm, 1), F32)],
        compiler_params=_cparams(2),
        name="out_proj",
    )(o_a, o_b, o_c, w_out, w_out, w_out, x, g_final)


def _prep_w_in(w_in):
    splits = (GLA_HK, GLA_HK, GLA_W, 2 * GLA_LR, GLA_W, MLA_Q_LORA, MLA_KV_LORA, MLA_ROPE,
              MLA_W, MEM_W, MEM_W)
    idx = [int(i) for i in np.cumsum(splits)[:-1]]
    (g_q, g_k, g_v, g_lr, g_gate, m_cq, m_ckv, m_kr, m_gate, c_q, c_gate) = jnp.split(
        w_in, idx, axis=-1)
    k = w_in.shape[0]
    zeros = lambda n: jnp.zeros((k, n), w_in.dtype)
    cols = [g_v, g_gate, m_cq, m_gate, g_q, g_k, m_ckv, c_q, c_gate,
            g_lr, zeros(LANES - 2 * GLA_LR), m_kr, zeros(LANES - MLA_ROPE), zeros(P_COLS - P_USED)]
    return jnp.concatenate(cols, axis=-1).astype(BF16)


def _prep_gla_gate(w_g2, b_g):
    wgs = []
    for d in range(2):
        w = jnp.zeros((LANES, GLA_HK), F32).at[d * GLA_LR:(d + 1) * GLA_LR].set(w_g2[d])
        wgs.append(w.astype(BF16))
    return wgs, [b_g[0][None, :], b_g[1][None, :]]


def _prep_mla(w_uq, w_ukv):
    wq = w_uq.reshape(MLA_Q_LORA, MLA_HEADS, MLA_NOPE + MLA_ROPE)
    wq = jnp.pad(wq, ((0, 0), (0, 0), (0, MLA_QK_PAD - MLA_NOPE - MLA_ROPE)))
    wq = wq.reshape(MLA_Q_LORA, MLA_HEADS * MLA_QK_PAD).astype(BF16)
    wkv = w_ukv.reshape(MLA_KV_LORA, MLA_HEADS, MLA_NOPE + MLA_V)
    wk = wkv[:, :, :MLA_NOPE].reshape(MLA_KV_LORA, MLA_HEADS * MLA_NOPE).astype(BF16)
    wv = wkv[:, :, MLA_NOPE:].reshape(MLA_KV_LORA, MLA_W).astype(BF16)
    return wq, wk, wv


def _rope_tables(t):
    pos = jnp.arange(t, dtype=F32)
    inv = 1.0 / (ROPE_THETA ** (jnp.arange(0, MLA_ROPE, 2, dtype=F32) / MLA_ROPE))
    ang = pos[:, None] * inv[None, :]
    cos, sin = jnp.cos(ang), jnp.sin(ang)
    z = jnp.zeros((t, LANES - MLA_ROPE), F32)
    return (jnp.concatenate([cos, cos, z], axis=-1), jnp.concatenate([-sin, sin, z], axis=-1))


def _pick(t, pref):
    return pref if t % pref == 0 else t


def _trunk(x, mem, w):
    bsz, t, _ = x.shape
    m = bsz * t
    xf = x.reshape(m, D_MODEL)
    p_act = _norm_matmul(xf, w["g_in"], w["w_in"], _pick(m, 512), P_TN)

    tb = _pick(t, 512)
    o_f = _gla_call(p_act, w["wg"][0], w["bg"][0], False, bsz, t, tb)
    o_a = _gla_call(p_act, w["wg"][1], w["bg"][1], True, bsz, t, tb, (o_f, w["g_head"]))

    cos_t, sin_t = _rope_tables(t)
    q, k, v = _mla_prep(p_act, w["g_q"], w["g_kv"], w["wq"], w["wk"], w["wv"], cos_t, sin_t,
                        t, _pick(t, 512))
    o_b = _mla_attn(q, k, v, p_act, bsz, t, _pick(t, 512), _pick(t, 512))

    kv_mem = _norm_matmul(mem.reshape(bsz * N_MEM, D_MODEL), w["mem_g"], w["mem_w_kv"],
                          bsz * N_MEM, 512)
    o_c = _mem_attn(p_act, kv_mem, bsz, t, _pick(t, 512))

    y = _out_proj(o_a, o_b, o_c, w["w_out"], xf, w["g_final"], _pick(m, 512), 1024)
    return y.reshape(bsz, t, D_MODEL)


def kernel(x_prompt, x_sample, mem_prompt, mem_sample, g_in, w_in, gla_w_g2, gla_b_g, gla_g_head,
           mla_g_q, mla_w_uq, mla_g_kv, mla_w_ukv, mem_g, mem_w_kv, w_out, g_final):
    wg, bg = _prep_gla_gate(gla_w_g2[0], gla_b_g[0])
    wq, wk, wv = _prep_mla(mla_w_uq[0], mla_w_ukv[0])
    w = {
        "g_in": g_in[0][None, :],
        "w_in": _prep_w_in(w_in[0]),
        "wg": wg, "bg": bg,
        "g_head": gla_g_head[0][None, :],
        "g_q": mla_g_q[0][None, :], "g_kv": mla_g_kv[0][None, :],
        "wq": wq, "wk": wk, "wv": wv,
        "mem_g": mem_g[0][None, :],
        "mem_w_kv": mem_w_kv[0].astype(BF16),
        "w_out": w_out[0].astype(BF16),
        "g_final": g_final[None, :],
    }
    return (_trunk(x_prompt, mem_prompt, w), _trunk(x_sample, mem_sample, w))
```

```python
import functools

import jax
import jax.numpy as jnp
import numpy as np
from jax import lax
from jax.experimental import pallas as pl
from jax.experimental.pallas import tpu as pltpu

F32 = jnp.float32
BF16 = jnp.bfloat16

D_MODEL = 4096
N_MEM = 256
GLA_HEADS = 12
GLA_DK = 64
GLA_DV = 128
GLA_LR = 16
GLA_TAU = 16.0
GLA_CHUNK = 64
GLA_HK = GLA_HEADS * GLA_DK
GLA_W = GLA_HEADS * GLA_DV
GLA_UNROLL = 2
MLA_HEADS = 12
MLA_Q_LORA = 1536
MLA_KV_LORA = 512
MLA_NOPE = 128
MLA_ROPE = 64
MLA_V = 128
ROPE_THETA = 10000.0
MLA_W = MLA_HEADS * MLA_V
MLA_QK_PAD = 256
MLA_V_PAD = 256
MEM_HEADS = 4
MEM_DH = 256
MEM_W = MEM_HEADS * MEM_DH
EPS = 1e-6
LANES = 128
LOG2E = 1.4426950408889634

COL_G_V = 0
COL_G_GATE = 1536
COL_M_CQ = 3072
COL_M_GATE = 4608
COL_G_Q = 6144
COL_G_K = 6912
COL_M_CKV = 7680
COL_C_Q = 8192
COL_C_GATE = 9216
COL_G_LR = 10240
COL_M_KR = 10368
P_USED = 10496
P_TN = 1536
P_COLS = 10752

VMEM_LIMIT = 60 * 1024 * 1024


def _cparams(n_axes):
    return pltpu.CompilerParams(
        dimension_semantics=("arbitrary",) * n_axes, vmem_limit_bytes=VMEM_LIMIT)


def _dot(a, b):
    return jnp.dot(a, b, preferred_element_type=F32)


def _dot_nt(a, b):
    return lax.dot_general(a, b, (((1,), (1,)), ((), ())), preferred_element_type=F32)


def _dot_tn(a, b):
    return lax.dot_general(a, b, (((0,), (0,)), ((), ())), preferred_element_type=F32)


def _silu(x):
    return x * jax.nn.sigmoid(x)


NORM_ROWS = 64


ROW_TILES_PER_W = 2


def _norm_matmul_kernel(x_ref, g_ref, w_ref, o_ref, h_ref):
    r = pl.program_id(2)

    @pl.when(pl.program_id(1) == 0)
    def _():
        def rows(c, carry):
            r0 = pl.multiple_of(c * NORM_ROWS, NORM_ROWS)
            x = x_ref[pl.ds(r0, NORM_ROWS), :]
            ms = jnp.mean(x * x, axis=-1, keepdims=True)
            h_ref[r, pl.ds(r0, NORM_ROWS), :] = (
                x * lax.rsqrt(ms + EPS) * g_ref[...]).astype(BF16)
            return carry
        lax.fori_loop(0, x_ref.shape[0] // NORM_ROWS, rows, 0)

    o_ref[...] = _dot(h_ref[r], w_ref[...]).astype(o_ref.dtype)


def _norm_matmul(x, g, w, tm, tn):
    m, k = x.shape
    n = w.shape[1]
    nr = ROW_TILES_PER_W

    def x_map(i, j, r):
        return (jnp.where(j == 0, i * nr + r, i * nr + nr - 1), 0)

    return pl.pallas_call(
        _norm_matmul_kernel,
        grid=(m // (nr * tm), n // tn, nr),
        in_specs=[
            pl.BlockSpec((tm, k), x_map),
            pl.BlockSpec((1, k), lambda i, j, r: (0, 0)),
            pl.BlockSpec((k, tn), lambda i, j, r: (0, j)),
        ],
        out_specs=pl.BlockSpec((tm, tn), lambda i, j, r: (i * nr + r, j)),
        out_shape=jax.ShapeDtypeStruct((m, n), BF16),
        scratch_shapes=[pltpu.VMEM((nr, tm, k), BF16)],
        compiler_params=_cparams(3),
        name="norm_matmul",
    )(x, g, w)


def _gla_kernel(rev, nchunk, *refs):
    if rev:
        (q_ref, k_ref, v_ref, lr_ref, wg_ref, bg_ref, gate_ref, of_ref, gh_ref,
         o_ref, s_ref) = refs
    else:
        q_ref, k_ref, v_ref, lr_ref, wg_ref, bg_ref, o_ref, s_ref = refs
    C = GLA_CHUNK

    @pl.when(pl.program_id(1) == 0)
    def _():
        s_ref[...] = jnp.zeros_like(s_ref)

    row = lax.broadcasted_iota(jnp.int32, (C, C), 0)
    col = lax.broadcasted_iota(jnp.int32, (C, C), 1)
    tri = (col >= row) if rev else (col <= row)
    tri_bf = jnp.where(tri, 1.0, 0.0).astype(BF16)
    row2 = lax.broadcasted_iota(jnp.int32, (2 * C, 2 * C), 0)
    col2 = lax.broadcasted_iota(jnp.int32, (2 * C, 2 * C), 1)
    same_head = (row2 < C) == (col2 < C)
    amask = same_head & ((col2 > row2) if rev else (col2 <= row2))
    q_lo = lax.broadcasted_iota(jnp.int32, (C, LANES), 1) < GLA_DK

    npair = GLA_HEADS // 2

    def gate_logits(c):
        rows = pl.ds(c * C, C)
        z = _dot(lr_ref[rows, :], wg_ref[...]) + bg_ref[...]
        lg = (jnp.minimum(z, 0.0) - jnp.log1p(jnp.exp(-jnp.abs(z)))) * (1.0 / GLA_TAU)
        hi = lg.astype(BF16)
        return hi, (lg - hi.astype(F32)).astype(BF16)

    def decayed_qk(c, hi, lo):
        rows = pl.ds(c * C, C)
        b = _dot(tri_bf, hi) + _dot(tri_bf, lo)
        bl = b[0:1, :] if rev else b[C - 1:C, :]
        q = q_ref[rows, :].astype(F32)
        k = k_ref[rows, :].astype(F32)
        q_in = (q * jnp.exp(b) * (GLA_DK ** -0.5)).astype(BF16)
        k_out = (k * jnp.exp(-b)).astype(BF16)
        k_last = (k * jnp.exp(bl - b)).astype(BF16)
        return q_in, k_out, k_last, jnp.exp(bl)

    def intra(c, prep):
        rows = pl.ds(c * C, C)
        q_in, k_out, k_last, _ = prep
        out = []
        for p in range(npair):
            sl = slice(p * LANES, (p + 1) * LANES)
            ql, ko, kl = q_in[:, sl], k_out[:, sl], k_last[:, sl]
            zero = jnp.zeros_like(ql)
            qs = jnp.concatenate([jnp.where(q_lo, ql, zero), jnp.where(q_lo, zero, ql)], axis=0)
            ks = jnp.concatenate([jnp.where(q_lo, kl, zero), jnp.where(q_lo, zero, kl)], axis=0)
            vs = jnp.concatenate([v_ref[rows, (2 * p + i) * GLA_DV:(2 * p + i + 1) * GLA_DV]
                                  for i in range(2)], axis=0)
            a_raw = _dot_nt(qs, jnp.concatenate([ko, ko], axis=0))
            out.append((qs, vs, a_raw, _dot_tn(ks, vs)))
        return out

    def outputs(c, prep, pairs):
        rows = pl.ds(c * C, C)
        dec = prep[3]
        for p, (qs, vs, a_raw, kv) in enumerate(pairs):
            a = jnp.where(amask, a_raw, 0.0).astype(BF16)
            st = s_ref[p]
            o2 = _dot(jnp.concatenate([a, qs], axis=1),
                      jnp.concatenate([vs, st.astype(BF16)], axis=0))
            dec_col = jnp.broadcast_to(dec[:, p * LANES:(p + 1) * LANES], (LANES, LANES)).T
            s_ref[p] = dec_col * st + kv
            for i in range(2):
                hs = slice((2 * p + i) * GLA_DV, (2 * p + i + 1) * GLA_DV)
                o = o2[i * C:(i + 1) * C]
                if rev:
                    tot = o + of_ref[rows, hs]
                    ms = jnp.mean(tot * tot, axis=-1, keepdims=True)
                    y = tot * lax.rsqrt(ms + EPS) * gh_ref[...]
                    o_ref[rows, hs] = (y * _silu(gate_ref[rows, hs].astype(F32))).astype(o_ref.dtype)
                else:
                    o_ref[rows, hs] = o

    order = [(nchunk - 1 - i) if rev else i for i in range(nchunk)]
    prep = decayed_qk(order[0], *gate_logits(order[0]))
    for i, c in enumerate(order):
        nxt = order[i + 1] if i + 1 < nchunk else None
        if nxt is not None:
            hi_lo = gate_logits(nxt)
        pairs = intra(c, prep)
        if nxt is not None:
            prep_next = decayed_qk(nxt, *hi_lo)
        outputs(c, prep, pairs)
        if nxt is not None:
            prep = prep_next


def _gla_call(p_act, wg, bg, rev, bsz, t, tb, extra=None):
    nblk = t // tb
    nchunk = tb // GLA_CHUNK

    def rowblk(b, i):
        return b * nblk + ((nblk - 1 - i) if rev else i)

    def act(width, colblk):
        return pl.BlockSpec((tb, width), lambda b, i: (rowblk(b, i), colblk))

    def full(shape):
        return pl.BlockSpec(shape, lambda b, i: (0,) * len(shape))

    in_specs = [
        act(GLA_HK, COL_G_Q // GLA_HK),
        act(GLA_HK, COL_G_K // GLA_HK),
        act(GLA_W, COL_G_V // GLA_W),
        act(LANES, COL_G_LR // LANES),
        full(wg.shape),
        full(bg.shape),
    ]
    args = [p_act, p_act, p_act, p_act, wg, bg]
    if rev:
        o_f, g_head = extra
        in_specs += [act(GLA_W, COL_G_GATE // GLA_W), act(GLA_W, 0), full(g_head.shape)]
        args += [p_act, o_f, g_head]
        out_dtype = BF16
    else:
        out_dtype = F32
    return pl.pallas_call(
        functools.partial(_gla_kernel, rev, nchunk),
        grid=(bsz, nblk),
        in_specs=in_specs,
        out_specs=act(GLA_W, 0),
        out_shape=jax.ShapeDtypeStruct((bsz * t, GLA_W), out_dtype),
        scratch_shapes=[pltpu.VMEM((GLA_HEADS // 2, GLA_DV, LANES), F32)],
        compiler_params=_cparams(2),
        name="gla_bwd" if rev else "gla_fwd",
    )(*args)


def _rope(r, cos_t, sin_t):
    quarter = MLA_ROPE // 2
    swapped = pltpu.roll(r, quarter, 1) + pltpu.roll(r, LANES - quarter, 1)
    return r * cos_t + swapped * sin_t


def _mla_prep_kernel(cq_ref, ckv_ref, kr_ref, gq_ref, gkv_ref, wq_ref, wk_ref, wv_ref,
                     cos_ref, sin_ref, q_ref, k_ref, v_ref):
    scale = (MLA_NOPE + MLA_ROPE) ** -0.5 * LOG2E
    cos_t = cos_ref[...]
    sin_t = sin_ref[...]

    cq = cq_ref[...].astype(F32)
    ms = jnp.mean(cq * cq, axis=-1, keepdims=True)
    qn = (cq * lax.rsqrt(ms + EPS) * gq_ref[...]).astype(BF16)
    ckv = ckv_ref[...].astype(F32)
    ms = jnp.mean(ckv * ckv, axis=-1, keepdims=True)
    kvn = (ckv * lax.rsqrt(ms + EPS) * gkv_ref[...]).astype(BF16)

    k_rope = _rope(kr_ref[...].astype(F32), cos_t, sin_t).astype(BF16)
    k_nope = _dot(kvn, wk_ref[...]).astype(BF16)
    v = _dot(kvn, wv_ref[...]).astype(BF16)
    ones_col = jnp.where(
        lax.broadcasted_iota(jnp.int32, (v.shape[0], MLA_V_PAD - MLA_V), 1) == 0,
        1.0, 0.0).astype(BF16)
    for h in range(MLA_HEADS):
        c0 = h * MLA_QK_PAD
        qh = _dot(qn, wq_ref[:, c0:c0 + MLA_QK_PAD])
        q_ref[:, c0:c0 + MLA_NOPE] = (qh[:, :MLA_NOPE] * scale).astype(BF16)
        q_ref[:, c0 + MLA_NOPE:c0 + MLA_QK_PAD] = (
            _rope(qh[:, MLA_NOPE:], cos_t, sin_t) * scale).astype(BF16)
        k_ref[:, c0:c0 + MLA_NOPE] = k_nope[:, h * MLA_NOPE:(h + 1) * MLA_NOPE]
        k_ref[:, c0 + MLA_NOPE:c0 + MLA_QK_PAD] = k_rope
        v0 = h * MLA_V_PAD
        v_ref[:, v0:v0 + MLA_V] = v[:, h * MLA_V:(h + 1) * MLA_V]
        v_ref[:, v0 + MLA_V:v0 + MLA_V_PAD] = ones_col


def _mla_prep(p_act, g_q, g_kv, wq, wk, wv, cos_t, sin_t, t, tm):
    m = p_act.shape[0]
    nt = t // tm

    def act(width, colblk):
        return pl.BlockSpec((tm, width), lambda i: (i, colblk))

    def full(shape):
        return pl.BlockSpec(shape, lambda i: (0,) * len(shape))

    table = pl.BlockSpec((tm, LANES), lambda i: (i % nt, 0))
    qk_w = MLA_HEADS * MLA_QK_PAD
    return pl.pallas_call(
        _mla_prep_kernel,
        grid=(m // tm,),
        in_specs=[
            act(MLA_Q_LORA, COL_M_CQ // MLA_Q_LORA),
            act(MLA_KV_LORA, COL_M_CKV // MLA_KV_LORA),
            act(LANES, COL_M_KR // LANES),
            full(g_q.shape), full(g_kv.shape), full(wq.shape), full(wk.shape), full(wv.shape),
            table, table,
        ],
        out_specs=[act(qk_w, 0), act(qk_w, 0), act(MLA_HEADS * MLA_V_PAD, 0)],
        out_shape=[
            jax.ShapeDtypeStruct((m, qk_w), BF16),
            jax.ShapeDtypeStruct((m, qk_w), BF16),
            jax.ShapeDtypeStruct((m, MLA_HEADS * MLA_V_PAD), BF16),
        ],
        compiler_params=_cparams(1),
        name="mla_prep",
    )(p_act, p_act, p_act, g_q, g_kv, wq, wk, wv, cos_t, sin_t)


def _mla_attn_kernel(nkv, bk, q_ref, k_ref, v_ref, gate_ref, o_ref, s_ref):
    q = q_ref[...]
    tq = q.shape[0]

    def scores(j):
        s_ref[j % 2] = _dot_nt(q, k_ref[j * bk:(j + 1) * bk, :])

    m = jnp.full((tq, 1), -jnp.inf, F32)
    acc = jnp.zeros((tq, MLA_V_PAD), F32)
    scores(0)
    for j in range(nkv):
        if j + 1 < nkv:
            scores(j + 1)
        s = s_ref[j % 2]
        m_new = jnp.maximum(m, jnp.max(s, axis=-1, keepdims=True))
        alpha = jnp.exp2(m - m_new)
        p = jnp.exp2(s - m_new).astype(BF16)
        acc = alpha * acc + _dot(p, v_ref[j * bk:(j + 1) * bk, :])
        m = m_new
    o = acc[:, :MLA_V] / acc[:, MLA_V:MLA_V + 1]
    o_ref[...] = (o * _silu(gate_ref[...].astype(F32))).astype(o_ref.dtype)


def _mla_attn(q, k, v, p_act, bsz, t, tq, bk):
    nq = t // tq
    nkv = t // bk
    gate_col0 = COL_M_GATE // MLA_V
    return pl.pallas_call(
        functools.partial(_mla_attn_kernel, nkv, bk),
        grid=(bsz, MLA_HEADS, nq),
        in_specs=[
            pl.BlockSpec((tq, MLA_QK_PAD), lambda b, h, i: (b * nq + i, h)),
            pl.BlockSpec((t, MLA_QK_PAD), lambda b, h, i: (b, h)),
            pl.BlockSpec((t, MLA_V_PAD), lambda b, h, i: (b, h)),
            pl.BlockSpec((tq, MLA_V), lambda b, h, i: (b * nq + i, gate_col0 + h)),
        ],
        out_specs=pl.BlockSpec((tq, MLA_V), lambda b, h, i: (b * nq + i, h)),
        out_shape=jax.ShapeDtypeStruct((bsz * t, MLA_W), BF16),
        scratch_shapes=[pltpu.VMEM((2, tq, bk), F32)],
        compiler_params=_cparams(3),
        name="mla_attn",
    )(q, k, v, p_act)


def _mem_attn_kernel(q_ref, gate_ref, kv_ref, o_ref):
    for h in range(MEM_HEADS):
        hs = slice(h * MEM_DH, (h + 1) * MEM_DH)
        s = _dot_nt(q_ref[:, hs], kv_ref[:, hs]) * (MEM_DH ** -0.5)
        p = jnp.exp(s - jnp.max(s, axis=-1, keepdims=True))
        l = jnp.sum(p, axis=-1, keepdims=True)
        o = _dot(p.astype(BF16), kv_ref[:, MEM_W + h * MEM_DH:MEM_W + (h + 1) * MEM_DH]) / l
        o_ref[:, hs] = (o * _silu(gate_ref[:, hs].astype(F32))).astype(o_ref.dtype)


def _mem_attn(p_act, kv_mem, bsz, t, tq):
    nq = t // tq
    return pl.pallas_call(
        _mem_attn_kernel,
        grid=(bsz, nq),
        in_specs=[
            pl.BlockSpec((tq, MEM_W), lambda b, i: (b * nq + i, COL_C_Q // MEM_W)),
            pl.BlockSpec((tq, MEM_W), lambda b, i: (b * nq + i, COL_C_GATE // MEM_W)),
            pl.BlockSpec((N_MEM, 2 * MEM_W), lambda b, i: (b, 0)),
        ],
        out_specs=pl.BlockSpec((tq, MEM_W), lambda b, i: (b * nq + i, 0)),
        out_shape=jax.ShapeDtypeStruct((bsz * t, MEM_W), BF16),
        compiler_params=_cparams(2),
        name="mem_attn",
    )(p_act, p_act, kv_mem)


OUT_TN = 1024


def _out_kernel(oa_ref, ob_ref, oc_ref, w_ref, x_ref, g_ref, y_ref):
    ssq = jnp.zeros((y_ref.shape[0], 1), F32)
    for c0 in range(0, D_MODEL, OUT_TN):
        cs = slice(c0, c0 + OUT_TN)
        acc = (_dot(oa_ref[...], w_ref[0:GLA_W, cs])
               + _dot(ob_ref[...], w_ref[GLA_W:GLA_W + MLA_W, cs])
               + _dot(oc_ref[...], w_ref[GLA_W + MLA_W:, cs]) + x_ref[:, cs])
        ssq = ssq + jnp.sum(acc * acc, axis=-1, keepdims=True)
        y_ref[:, cs] = acc
    r = lax.rsqrt(ssq * (1.0 / D_MODEL) + EPS)
    for c0 in range(0, D_MODEL, OUT_TN):
        cs = slice(c0, c0 + OUT_TN)
        y_ref[:, cs] = y_ref[:, cs] * r * g_ref[:, cs]


def _out_proj(o_a, o_b, o_c, w_out, x, g_final, tm):
    m = x.shape[0]
    return pl.pallas_call(
        _out_kernel,
        grid=(m // tm,),
        in_specs=[
            pl.BlockSpec((tm, GLA_W), lambda i: (i, 0)),
            pl.BlockSpec((tm, MLA_W), lambda i: (i, 0)),
            pl.BlockSpec((tm, MEM_W), lambda i: (i, 0)),
            pl.BlockSpec((D_MODEL, D_MODEL), lambda i: (0, 0), pipeline_mode=pl.Buffered(1)),
            pl.BlockSpec((tm, D_MODEL), lambda i: (i, 0)),
            pl.BlockSpec((1, D_MODEL), lambda i: (0, 0)),
        ],
        out_specs=pl.BlockSpec((tm, D_MODEL), lambda i: (i, 0)),
        out_shape=jax.ShapeDtypeStruct((m, D_MODEL), F32),
        compiler_params=_cparams(1),
        name="out_proj",
    )(o_a, o_b, o_c, w_out, x, g_final)


IN_WIDTHS = (GLA_HK, GLA_HK, GLA_W, 2 * GLA_LR, GLA_W, MLA_Q_LORA, MLA_KV_LORA, MLA_ROPE,
             MLA_W, MEM_W, MEM_W)
IN_DST = (COL_G_Q, COL_G_K, COL_G_V, COL_G_LR, COL_G_GATE, COL_M_CQ, COL_M_CKV, COL_M_KR,
          COL_M_GATE, COL_C_Q, COL_C_GATE)
N_IN = sum(IN_WIDTHS)
W_IN_ROWS = 256


def _w_in_relayout_kernel(w_ref, o_ref):
    o_ref[:, COL_G_LR:] = jnp.zeros((o_ref.shape[0], P_COLS - COL_G_LR), o_ref.dtype)
    src = 0
    for width, dst in zip(IN_WIDTHS, IN_DST):
        o_ref[:, dst:dst + width] = w_ref[:, src:src + width].astype(o_ref.dtype)
        src += width


def _prep_w_in(w_in):
    k = w_in.shape[0]
    return pl.pallas_call(
        _w_in_relayout_kernel,
        grid=(k // W_IN_ROWS,),
        in_specs=[pl.BlockSpec((W_IN_ROWS, N_IN), lambda i: (i, 0))],
        out_specs=pl.BlockSpec((W_IN_ROWS, P_COLS), lambda i: (i, 0)),
        out_shape=jax.ShapeDtypeStruct((k, P_COLS), BF16),
        compiler_params=_cparams(1),
        name="w_in_relayout",
    )(w_in)


def _prep_gla_gate(w_g2, b_g):
    wgs = []
    for d in range(2):
        w = jnp.zeros((LANES, GLA_HK), F32).at[d * GLA_LR:(d + 1) * GLA_LR].set(w_g2[d])
        wgs.append(w.astype(BF16))
    return wgs, [b_g[0][None, :], b_g[1][None, :]]


def _prep_mla(w_uq, w_ukv):
    wq = w_uq.reshape(MLA_Q_LORA, MLA_HEADS, MLA_NOPE + MLA_ROPE)
    wq = jnp.pad(wq, ((0, 0), (0, 0), (0, MLA_QK_PAD - MLA_NOPE - MLA_ROPE)))
    wq = wq.reshape(MLA_Q_LORA, MLA_HEADS * MLA_QK_PAD).astype(BF16)
    wkv = w_ukv.reshape(MLA_KV_LORA, MLA_HEADS, MLA_NOPE + MLA_V)
    wk = wkv[:, :, :MLA_NOPE].reshape(MLA_KV_LORA, MLA_HEADS * MLA_NOPE).astype(BF16)
    wv = wkv[:, :, MLA_NOPE:].reshape(MLA_KV_LORA, MLA_W).astype(BF16)
    return wq, wk, wv


def _rope_tables(t):
    pos = jnp.arange(t, dtype=F32)
    inv = 1.0 / (ROPE_THETA ** (jnp.arange(0, MLA_ROPE, 2, dtype=F32) / MLA_ROPE))
    ang = pos[:, None] * inv[None, :]
    cos, sin = jnp.cos(ang), jnp.sin(ang)
    z = jnp.zeros((t, LANES - MLA_ROPE), F32)
    return (jnp.concatenate([cos, cos, z], axis=-1), jnp.concatenate([-sin, sin, z], axis=-1))


def _pick(t, pref):
    return pref if t % pref == 0 else t


def _trunk(x, mem, w):
    bsz, t, _ = x.shape
    m = bsz * t
    xf = x.reshape(m, D_MODEL)
    p_act = _norm_matmul(xf, w["g_in"], w["w_in"], min(512, m // ROW_TILES_PER_W), P_TN)

    tb = _pick(t, 512)
    o_f = _gla_call(p_act, w["wg"][0], w["bg"][0], False, bsz, t, tb)
    o_a = _gla_call(p_act, w["wg"][1], w["bg"][1], True, bsz, t, tb, (o_f, w["g_head"]))

    cos_t, sin_t = _rope_tables(t)
    q, k, v = _mla_prep(p_act, w["g_q"], w["g_kv"], w["wq"], w["wk"], w["wv"], cos_t, sin_t,
                        t, _pick(t, 512))
    o_b = _mla_attn(q, k, v, p_act, bsz, t, _pick(t, 512), _pick(t, 512))

    kv_mem = _norm_matmul(mem.reshape(bsz * N_MEM, D_MODEL), w["mem_g"], w["mem_w_kv"],
                          bsz * N_MEM // ROW_TILES_PER_W, 512)
    o_c = _mem_attn(p_act, kv_mem, bsz, t, _pick(t, 512))

    y = _out_proj(o_a, o_b, o_c, w["w_out"], xf, w["g_final"], _pick(m, 256))
    return y.reshape(bsz, t, D_MODEL)


def kernel(x_prompt, x_sample, mem_prompt, mem_sample, g_in, w_in, gla_w_g2, gla_b_g, gla_g_head,
           mla_g_q, mla_w_uq, mla_g_kv, mla_w_ukv, mem_g, mem_w_kv, w_out, g_final):
    wg, bg = _prep_gla_gate(gla_w_g2[0], gla_b_g[0])
    wq, wk, wv = _prep_mla(mla_w_uq[0], mla_w_ukv[0])
    w = {
        "g_in": g_in[0][None, :],
        "w_in": _prep_w_in(w_in[0]),
        "wg": wg, "bg": bg,
        "g_head": gla_g_head[0][None, :],
        "g_q": mla_g_q[0][None, :], "g_kv": mla_g_kv[0][None, :],
        "wq": wq, "wk": wk, "wv": wv,
        "mem_g": mem_g[0][None, :],
        "mem_w_kv": mem_w_kv[0].astype(BF16),
        "w_out": w_out[0].astype(BF16),
        "g_final": g_final[None, :],
    }
    return (_trunk(x_prompt, mem_prompt, w), _trunk(x_sample, mem_sample, w))
```

```python
import functools

import jax
import jax.numpy as jnp
from jax import lax
from jax.experimental import pallas as pl
from jax.experimental.pallas import tpu as pltpu

F32 = jnp.float32
BF16 = jnp.bfloat16

D_MODEL = 4096
N_MEM = 256
GLA_HEADS = 12
GLA_DK = 64
GLA_DV = 128
GLA_LR = 16
GLA_TAU = 16.0
GLA_CHUNK = 64
GLA_HK = GLA_HEADS * GLA_DK
GLA_W = GLA_HEADS * GLA_DV
MLA_HEADS = 12
MLA_Q_LORA = 1536
MLA_KV_LORA = 512
MLA_NOPE = 128
MLA_ROPE = 64
MLA_V = 128
ROPE_THETA = 10000.0
MLA_W = MLA_HEADS * MLA_V
MLA_QK_PAD = 256
MLA_V_PAD = 256
MEM_HEADS = 4
MEM_DH = 256
MEM_W = MEM_HEADS * MEM_DH
EPS = 1e-6
LANES = 128
LOG2E = 1.4426950408889634

COL_G_V = 0
COL_G_GATE = 1536
COL_M_CQ = 3072
COL_M_GATE = 4608
COL_G_Q = 6144
COL_G_K = 6912
COL_M_CKV = 7680
COL_C_Q = 8192
COL_C_GATE = 9216
COL_G_LR = 10240
COL_M_KR = 10368
P_USED = 10496
P_TN = 1536
P_COLS = 10752

VMEM_LIMIT = 60 * 1024 * 1024


def _cparams(n_axes):
    return pltpu.CompilerParams(
        dimension_semantics=("arbitrary",) * n_axes, vmem_limit_bytes=VMEM_LIMIT)


def _dot(a, b):
    return jnp.dot(a, b, preferred_element_type=F32)


def _dot_nt(a, b):
    return lax.dot_general(a, b, (((1,), (1,)), ((), ())), preferred_element_type=F32)


def _dot_tn(a, b):
    return lax.dot_general(a, b, (((0,), (0,)), ((), ())), preferred_element_type=F32)


def _silu(x):
    return x * jax.nn.sigmoid(x)


NORM_ROWS = 64
ROW_TILES_PER_W = 2


def _norm_matmul_kernel(w_transposed, x_ref, g_ref, w_ref, o_ref, h_ref):
    r = pl.program_id(2)

    @pl.when(pl.program_id(1) == 0)
    def _():
        def rows(c, carry):
            r0 = pl.multiple_of(c * NORM_ROWS, NORM_ROWS)
            x = x_ref[pl.ds(r0, NORM_ROWS), :]
            ms = jnp.mean(x * x, axis=-1, keepdims=True)
            h_ref[r, pl.ds(r0, NORM_ROWS), :] = (
                x * lax.rsqrt(ms + EPS) * g_ref[...]).astype(BF16)
            return carry
        lax.fori_loop(0, x_ref.shape[0] // NORM_ROWS, rows, 0)

    dot = _dot_nt if w_transposed else _dot
    o_ref[...] = dot(h_ref[r], w_ref[...]).astype(o_ref.dtype)


def _norm_matmul(x, g, w, tm, tn, w_transposed=False):
    m, k = x.shape
    n = w.shape[0] if w_transposed else w.shape[1]
    nr = ROW_TILES_PER_W
    if w_transposed:
        w_spec = pl.BlockSpec((tn, k), lambda i, j, r: (j, 0))
    else:
        w_spec = pl.BlockSpec((k, tn), lambda i, j, r: (0, j))

    def x_map(i, j, r):
        return (jnp.where(j == 0, i * nr + r, i * nr + nr - 1), 0)

    return pl.pallas_call(
        functools.partial(_norm_matmul_kernel, w_transposed),
        grid=(m // (nr * tm), n // tn, nr),
        in_specs=[
            pl.BlockSpec((tm, k), x_map),
            pl.BlockSpec((1, k), lambda i, j, r: (0, 0)),
            w_spec,
        ],
        out_specs=pl.BlockSpec((tm, tn), lambda i, j, r: (i * nr + r, j)),
        out_shape=jax.ShapeDtypeStruct((m, n), BF16),
        scratch_shapes=[pltpu.VMEM((nr, tm, k), BF16)],
        compiler_params=_cparams(3),
        name="norm_matmul",
    )(x, g, w)


def _gla_kernel(rev, nchunk, *refs):
    if rev:
        (q_ref, k_ref, v_ref, lr_ref, wg_ref, bg_ref, gate_ref, of_ref, gh_ref,
         o_ref, s_ref) = refs
    else:
        q_ref, k_ref, v_ref, lr_ref, wg_ref, bg_ref, o_ref, s_ref = refs
    C = GLA_CHUNK

    @pl.when(pl.program_id(1) == 0)
    def _():
        s_ref[...] = jnp.zeros_like(s_ref)

    row = lax.broadcasted_iota(jnp.int32, (C, C), 0)
    col = lax.broadcasted_iota(jnp.int32, (C, C), 1)
    tri = (col >= row) if rev else (col <= row)
    tri_bf = jnp.where(tri, 1.0, 0.0).astype(BF16)
    row2 = lax.broadcasted_iota(jnp.int32, (2 * C, 2 * C), 0)
    col2 = lax.broadcasted_iota(jnp.int32, (2 * C, 2 * C), 1)
    same_head = (row2 < C) == (col2 < C)
    amask = same_head & ((col2 > row2) if rev else (col2 <= row2))
    q_lo = lax.broadcasted_iota(jnp.int32, (C, LANES), 1) < GLA_DK

    npair = GLA_HEADS // 2

    def gate_logits(c):
        rows = pl.ds(c * C, C)
        z = _dot(lr_ref[rows, :], wg_ref[...]) + bg_ref[...]
        lg = (jnp.minimum(z, 0.0) - jnp.log1p(jnp.exp(-jnp.abs(z)))) * (1.0 / GLA_TAU)
        hi = lg.astype(BF16)
        return hi, (lg - hi.astype(F32)).astype(BF16)

    def decayed_qk(c, hi, lo):
        rows = pl.ds(c * C, C)
        b = _dot(tri_bf, hi) + _dot(tri_bf, lo)
        bl = b[0:1, :] if rev else b[C - 1:C, :]
        q = q_ref[rows, :].astype(F32)
        k = k_ref[rows, :].astype(F32)
        q_in = (q * jnp.exp(b) * (GLA_DK ** -0.5)).astype(BF16)
        k_out = (k * jnp.exp(-b)).astype(BF16)
        k_last = (k * jnp.exp(bl - b)).astype(BF16)
        return q_in, k_out, k_last, jnp.exp(bl)

    def intra(c, prep):
        rows = pl.ds(c * C, C)
        q_in, k_out, k_last, _ = prep
        out = []
        for p in range(npair):
            sl = slice(p * LANES, (p + 1) * LANES)
            ql, ko, kl = q_in[:, sl], k_out[:, sl], k_last[:, sl]
            zero = jnp.zeros_like(ql)
            qs = jnp.concatenate([jnp.where(q_lo, ql, zero), jnp.where(q_lo, zero, ql)], axis=0)
            ks = jnp.concatenate([jnp.where(q_lo, kl, zero), jnp.where(q_lo, zero, kl)], axis=0)
            vs = jnp.concatenate([v_ref[rows, (2 * p + i) * GLA_DV:(2 * p + i + 1) * GLA_DV]
                                  for i in range(2)], axis=0)
            a_raw = _dot_nt(qs, jnp.concatenate([ko, ko], axis=0))
            out.append((qs, vs, a_raw, _dot_tn(ks, vs)))
        return out

    def outputs(c, prep, pairs):
        rows = pl.ds(c * C, C)
        dec = prep[3]
        for p, (qs, vs, a_raw, kv) in enumerate(pairs):
            a = jnp.where(amask, a_raw, 0.0).astype(BF16)
            st = s_ref[p]
            o2 = _dot(jnp.concatenate([a, qs], axis=1),
                      jnp.concatenate([vs, st.astype(BF16)], axis=0))
            dec_col = jnp.broadcast_to(dec[:, p * LANES:(p + 1) * LANES], (LANES, LANES)).T
            s_ref[p] = dec_col * st + kv
            for i in range(2):
                hs = slice((2 * p + i) * GLA_DV, (2 * p + i + 1) * GLA_DV)
                o = o2[i * C:(i + 1) * C]
                if rev:
                    tot = o + of_ref[rows, hs]
                    ms = jnp.mean(tot * tot, axis=-1, keepdims=True)
                    y = tot * lax.rsqrt(ms + EPS) * gh_ref[...]
                    o_ref[rows, hs] = (y * _silu(gate_ref[rows, hs].astype(F32))).astype(o_ref.dtype)
                else:
                    o_ref[rows, hs] = o

    order = [(nchunk - 1 - i) if rev else i for i in range(nchunk)]
    prep = decayed_qk(order[0], *gate_logits(order[0]))
    for i, c in enumerate(order):
        nxt = order[i + 1] if i + 1 < nchunk else None
        if nxt is not None:
            hi_lo = gate_logits(nxt)
        pairs = intra(c, prep)
        if nxt is not None:
            prep_next = decayed_qk(nxt, *hi_lo)
        outputs(c, prep, pairs)
        if nxt is not None:
            prep = prep_next


def _gla_call(p_act, wg, bg, rev, bsz, t, tb, extra=None):
    nblk = t // tb
    nchunk = tb // GLA_CHUNK

    def rowblk(b, i):
        return b * nblk + ((nblk - 1 - i) if rev else i)

    def act(width, colblk):
        return pl.BlockSpec((tb, width), lambda b, i: (rowblk(b, i), colblk))

    def full(shape):
        return pl.BlockSpec(shape, lambda b, i: (0,) * len(shape))

    in_specs = [
        act(GLA_HK, COL_G_Q // GLA_HK),
        act(GLA_HK, COL_G_K // GLA_HK),
        act(GLA_W, COL_G_V // GLA_W),
        act(LANES, COL_G_LR // LANES),
        full(wg.shape),
        full(bg.shape),
    ]
    args = [p_act, p_act, p_act, p_act, wg, bg]
    if rev:
        o_f, g_head = extra
        in_specs += [act(GLA_W, COL_G_GATE // GLA_W), act(GLA_W, 0), full(g_head.shape)]
        args += [p_act, o_f, g_head]
        out_dtype = BF16
    else:
        out_dtype = F32
    return pl.pallas_call(
        functools.partial(_gla_kernel, rev, nchunk),
        grid=(bsz, nblk),
        in_specs=in_specs,
        out_specs=act(GLA_W, 0),
        out_shape=jax.ShapeDtypeStruct((bsz * t, GLA_W), out_dtype),
        scratch_shapes=[pltpu.VMEM((GLA_HEADS // 2, GLA_DV, LANES), F32)],
        compiler_params=_cparams(2),
        name="gla_bwd" if rev else "gla_fwd",
    )(*args)


def _rope(r, cos_t, sin_t):
    quarter = MLA_ROPE // 2
    swapped = pltpu.roll(r, quarter, 1) + pltpu.roll(r, LANES - quarter, 1)
    return r * cos_t + swapped * sin_t


def _mla_prep_kernel(cq_ref, ckv_ref, kr_ref, gq_ref, gkv_ref, wq_ref, wk_ref, wv_ref,
                     cos_ref, sin_ref, q_ref, k_ref, v_ref):
    scale = (MLA_NOPE + MLA_ROPE) ** -0.5 * LOG2E
    cos_t = cos_ref[...]
    sin_t = sin_ref[...]

    cq = cq_ref[...].astype(F32)
    ms = jnp.mean(cq * cq, axis=-1, keepdims=True)
    qn = (cq * lax.rsqrt(ms + EPS) * gq_ref[...]).astype(BF16)
    ckv = ckv_ref[...].astype(F32)
    ms = jnp.mean(ckv * ckv, axis=-1, keepdims=True)
    kvn = (ckv * lax.rsqrt(ms + EPS) * gkv_ref[...]).astype(BF16)

    k_rope = _rope(kr_ref[...].astype(F32), cos_t, sin_t).astype(BF16)
    k_nope = _dot(kvn, wk_ref[...]).astype(BF16)
    v = _dot(kvn, wv_ref[...]).astype(BF16)
    ones_col = jnp.where(
        lax.broadcasted_iota(jnp.int32, (v.shape[0], MLA_V_PAD - MLA_V), 1) == 0,
        1.0, 0.0).astype(BF16)
    for h in range(MLA_HEADS):
        c0 = h * MLA_QK_PAD
        qh = _dot(qn, wq_ref[:, c0:c0 + MLA_QK_PAD])
        q_ref[:, c0:c0 + MLA_NOPE] = (qh[:, :MLA_NOPE] * scale).astype(BF16)
        q_ref[:, c0 + MLA_NOPE:c0 + MLA_QK_PAD] = (
            _rope(qh[:, MLA_NOPE:], cos_t, sin_t) * scale).astype(BF16)
        k_ref[:, c0:c0 + MLA_NOPE] = k_nope[:, h * MLA_NOPE:(h + 1) * MLA_NOPE]
        k_ref[:, c0 + MLA_NOPE:c0 + MLA_QK_PAD] = k_rope
        v0 = h * MLA_V_PAD
        v_ref[:, v0:v0 + MLA_V] = v[:, h * MLA_V:(h + 1) * MLA_V]
        v_ref[:, v0 + MLA_V:v0 + MLA_V_PAD] = ones_col


def _mla_prep(p_act, g_q, g_kv, wq, wk, wv, cos_t, sin_t, t, tm):
    m = p_act.shape[0]
    nt = t // tm

    def act(width, colblk):
        return pl.BlockSpec((tm, width), lambda i: (i, colblk))

    def full(shape):
        return pl.BlockSpec(shape, lambda i: (0,) * len(shape))

    table = pl.BlockSpec((tm, LANES), lambda i: (i % nt, 0))
    qk_w = MLA_HEADS * MLA_QK_PAD
    return pl.pallas_call(
        _mla_prep_kernel,
        grid=(m // tm,),
        in_specs=[
            act(MLA_Q_LORA, COL_M_CQ // MLA_Q_LORA),
            act(MLA_KV_LORA, COL_M_CKV // MLA_KV_LORA),
            act(LANES, COL_M_KR // LANES),
            full(g_q.shape), full(g_kv.shape), full(wq.shape), full(wk.shape), full(wv.shape),
            table, table,
        ],
        out_specs=[act(qk_w, 0), act(qk_w, 0), act(MLA_HEADS * MLA_V_PAD, 0)],
        out_shape=[
            jax.ShapeDtypeStruct((m, qk_w), BF16),
            jax.ShapeDtypeStruct((m, qk_w), BF16),
            jax.ShapeDtypeStruct((m, MLA_HEADS * MLA_V_PAD), BF16),
        ],
        compiler_params=_cparams(1),
        name="mla_prep",
    )(p_act, p_act, p_act, g_q, g_kv, wq, wk, wv, cos_t, sin_t)


def _mla_attn_kernel(nkv, bk, q_ref, k_ref, v_ref, gate_ref, o_ref, s_ref):
    q = q_ref[...]
    tq = q.shape[0]

    def scores(j):
        s_ref[j % 2] = _dot_nt(q, k_ref[j * bk:(j + 1) * bk, :])

    m = jnp.full((tq, 1), -jnp.inf, F32)
    acc = jnp.zeros((tq, MLA_V_PAD), F32)
    scores(0)
    for j in range(nkv):
        if j + 1 < nkv:
            scores(j + 1)
        s = s_ref[j % 2]
        m_new = jnp.maximum(m, jnp.max(s, axis=-1, keepdims=True))
        alpha = jnp.exp2(m - m_new)
        p = jnp.exp2(s - m_new).astype(BF16)
        acc = alpha * acc + _dot(p, v_ref[j * bk:(j + 1) * bk, :])
        m = m_new
    o = acc[:, :MLA_V] / acc[:, MLA_V:MLA_V + 1]
    o_ref[...] = (o * _silu(gate_ref[...].astype(F32))).astype(o_ref.dtype)


def _mla_attn(q, k, v, p_act, bsz, t, tq, bk):
    nq = t // tq
    nkv = t // bk
    gate_col0 = COL_M_GATE // MLA_V
    return pl.pallas_call(
        functools.partial(_mla_attn_kernel, nkv, bk),
        grid=(bsz, MLA_HEADS, nq),
        in_specs=[
            pl.BlockSpec((tq, MLA_QK_PAD), lambda b, h, i: (b * nq + i, h)),
            pl.BlockSpec((t, MLA_QK_PAD), lambda b, h, i: (b, h)),
            pl.BlockSpec((t, MLA_V_PAD), lambda b, h, i: (b, h)),
            pl.BlockSpec((tq, MLA_V), lambda b, h, i: (b * nq + i, gate_col0 + h)),
        ],
        out_specs=pl.BlockSpec((tq, MLA_V), lambda b, h, i: (b * nq + i, h)),
        out_shape=jax.ShapeDtypeStruct((bsz * t, MLA_W), BF16),
        scratch_shapes=[pltpu.VMEM((2, tq, bk), F32)],
        compiler_params=_cparams(3),
        name="mla_attn",
    )(q, k, v, p_act)


def _mem_attn_kernel(q_ref, gate_ref, kv_ref, o_ref):
    for h in range(MEM_HEADS):
        hs = slice(h * MEM_DH, (h + 1) * MEM_DH)
        s = _dot_nt(q_ref[:, hs], kv_ref[:, hs]) * (MEM_DH ** -0.5)
        p = jnp.exp(s - jnp.max(s, axis=-1, keepdims=True))
        l = jnp.sum(p, axis=-1, keepdims=True)
        o = _dot(p.astype(BF16), kv_ref[:, MEM_W + h * MEM_DH:MEM_W + (h + 1) * MEM_DH]) / l
        o_ref[:, hs] = (o * _silu(gate_ref[:, hs].astype(F32))).astype(o_ref.dtype)


def _mem_attn(p_act, kv_mem, bsz, t, tq):
    nq = t // tq
    return pl.pallas_call(
        _mem_attn_kernel,
        grid=(bsz, nq),
        in_specs=[
            pl.BlockSpec((tq, MEM_W), lambda b, i: (b * nq + i, COL_C_Q // MEM_W)),
            pl.BlockSpec((tq, MEM_W), lambda b, i: (b * nq + i, COL_C_GATE // MEM_W)),
            pl.BlockSpec((N_MEM, 2 * MEM_W), lambda b, i: (b, 0)),
        ],
        out_specs=pl.BlockSpec((tq, MEM_W), lambda b, i: (b * nq + i, 0)),
        out_shape=jax.ShapeDtypeStruct((bsz * t, MEM_W), BF16),
        compiler_params=_cparams(2),
        name="mem_attn",
    )(p_act, p_act, kv_mem)


OUT_TN = 1024


def _out_kernel(oa_ref, ob_ref, oc_ref, w_ref, x_ref, g_ref, y_ref):
    ssq = jnp.zeros((y_ref.shape[0], 1), F32)
    for c0 in range(0, D_MODEL, OUT_TN):
        cs = slice(c0, c0 + OUT_TN)
        acc = (_dot(oa_ref[...], w_ref[0:GLA_W, cs])
               + _dot(ob_ref[...], w_ref[GLA_W:GLA_W + MLA_W, cs])
               + _dot(oc_ref[...], w_ref[GLA_W + MLA_W:, cs]) + x_ref[:, cs])
        ssq = ssq + jnp.sum(acc * acc, axis=-1, keepdims=True)
        y_ref[:, cs] = acc
    r = lax.rsqrt(ssq * (1.0 / D_MODEL) + EPS)
    for c0 in range(0, D_MODEL, OUT_TN):
        cs = slice(c0, c0 + OUT_TN)
        y_ref[:, cs] = y_ref[:, cs] * r * g_ref[:, cs]


def _out_proj(o_a, o_b, o_c, w_out, x, g_final, tm):
    m = x.shape[0]
    return pl.pallas_call(
        _out_kernel,
        grid=(m // tm,),
        in_specs=[
            pl.BlockSpec((tm, GLA_W), lambda i: (i, 0)),
            pl.BlockSpec((tm, MLA_W), lambda i: (i, 0)),
            pl.BlockSpec((tm, MEM_W), lambda i: (i, 0)),
            pl.BlockSpec((D_MODEL, D_MODEL), lambda i: (0, 0), pipeline_mode=pl.Buffered(1)),
            pl.BlockSpec((tm, D_MODEL), lambda i: (i, 0)),
            pl.BlockSpec((1, D_MODEL), lambda i: (0, 0)),
        ],
        out_specs=pl.BlockSpec((tm, D_MODEL), lambda i: (i, 0)),
        out_shape=jax.ShapeDtypeStruct((m, D_MODEL), F32),
        compiler_params=_cparams(1),
        name="out_proj",
    )(o_a, o_b, o_c, w_out, x, g_final)


IN_WIDTHS = (GLA_HK, GLA_HK, GLA_W, 2 * GLA_LR, GLA_W, MLA_Q_LORA, MLA_KV_LORA, MLA_ROPE,
             MLA_W, MEM_W, MEM_W)
IN_DST = (COL_G_Q, COL_G_K, COL_G_V, COL_G_LR, COL_G_GATE, COL_M_CQ, COL_M_CKV, COL_M_KR,
          COL_M_GATE, COL_C_Q, COL_C_GATE)
N_IN = sum(IN_WIDTHS)
W_IN_COLS = 256


def _w_in_relayout_kernel(w_ref, o_ref):
    o_ref[COL_G_LR:, :] = jnp.zeros((P_COLS - COL_G_LR, o_ref.shape[1]), o_ref.dtype)
    src = 0
    for width, dst in zip(IN_WIDTHS, IN_DST):
        o_ref[dst:dst + width, :] = w_ref[src:src + width, :].astype(o_ref.dtype)
        src += width


def _prep_w_in(w_in):
    k = w_in.shape[0]
    return pl.pallas_call(
        _w_in_relayout_kernel,
        grid=(k // W_IN_COLS,),
        in_specs=[pl.BlockSpec((N_IN, W_IN_COLS), lambda i: (0, i))],
        out_specs=pl.BlockSpec((P_COLS, W_IN_COLS), lambda i: (0, i)),
        out_shape=jax.ShapeDtypeStruct((P_COLS, k), BF16),
        compiler_params=_cparams(1),
        name="w_in_relayout",
    )(w_in.T)


def _prep_gla_gate(w_g2, b_g):
    wgs = []
    for d in range(2):
        w = jnp.zeros((LANES, GLA_HK), F32).at[d * GLA_LR:(d + 1) * GLA_LR].set(w_g2[d])
        wgs.append(w.astype(BF16))
    return wgs, [b_g[0][None, :], b_g[1][None, :]]


def _prep_mla(w_uq, w_ukv):
    wq = w_uq.reshape(MLA_Q_LORA, MLA_HEADS, MLA_NOPE + MLA_ROPE)
    wq = jnp.pad(wq, ((0, 0), (0, 0), (0, MLA_QK_PAD - MLA_NOPE - MLA_ROPE)))
    wq = wq.reshape(MLA_Q_LORA, MLA_HEADS * MLA_QK_PAD).astype(BF16)
    wkv = w_ukv.reshape(MLA_KV_LORA, MLA_HEADS, MLA_NOPE + MLA_V)
    wk = wkv[:, :, :MLA_NOPE].reshape(MLA_KV_LORA, MLA_HEADS * MLA_NOPE).astype(BF16)
    wv = wkv[:, :, MLA_NOPE:].reshape(MLA_KV_LORA, MLA_W).astype(BF16)
    return wq, wk, wv


def _rope_tables(t):
    pos = jnp.arange(t, dtype=F32)
    inv = 1.0 / (ROPE_THETA ** (jnp.arange(0, MLA_ROPE, 2, dtype=F32) / MLA_ROPE))
    ang = pos[:, None] * inv[None, :]
    cos, sin = jnp.cos(ang), jnp.sin(ang)
    z = jnp.zeros((t, LANES - MLA_ROPE), F32)
    return (jnp.concatenate([cos, cos, z], axis=-1), jnp.concatenate([-sin, sin, z], axis=-1))


def _pick(t, pref):
    return pref if t % pref == 0 else t


def _trunk(x, mem, w):
    bsz, t, _ = x.shape
    m = bsz * t
    xf = x.reshape(m, D_MODEL)
    p_act = _norm_matmul(xf, w["g_in"], w["w_in_t"], min(512, m // ROW_TILES_PER_W), P_TN,
                         w_transposed=True)

    tb = _pick(t, 1024)
    o_f = _gla_call(p_act, w["wg"][0], w["bg"][0], False, bsz, t, tb)
    o_a = _gla_call(p_act, w["wg"][1], w["bg"][1], True, bsz, t, tb, (o_f, w["g_head"]))

    cos_t, sin_t = _rope_tables(t)
    q, k, v = _mla_prep(p_act, w["g_q"], w["g_kv"], w["wq"], w["wk"], w["wv"], cos_t, sin_t,
                        t, _pick(t, 512))
    o_b = _mla_attn(q, k, v, p_act, bsz, t, _pick(t, 512), _pick(t, 1024))

    kv_mem = _norm_matmul(mem.reshape(bsz * N_MEM, D_MODEL), w["mem_g"], w["mem_w_kv"],
                          bsz * N_MEM // ROW_TILES_PER_W, 512)
    o_c = _mem_attn(p_act, kv_mem, bsz, t, _pick(t, 512))

    y = _out_proj(o_a, o_b, o_c, w["w_out"], xf, w["g_final"], _pick(m, 256))
    return y.reshape(bsz, t, D_MODEL)


def kernel(x_prompt, x_sample, mem_prompt, mem_sample, g_in, w_in, gla_w_g2, gla_b_g, gla_g_head,
           mla_g_q, mla_w_uq, mla_g_kv, mla_w_ukv, mem_g, mem_w_kv, w_out, g_final):
    wg, bg = _prep_gla_gate(gla_w_g2[0], gla_b_g[0])
    wq, wk, wv = _prep_mla(mla_w_uq[0], mla_w_ukv[0])
    w = {
        "g_in": g_in[0][None, :],
        "w_in_t": _prep_w_in(w_in[0]),
        "wg": wg, "bg": bg,
        "g_head": gla_g_head[0][None, :],
        "g_q": mla_g_q[0][None, :], "g_kv": mla_g_kv[0][None, :],
        "wq": wq, "wk": wk, "wv": wv,
        "mem_g": mem_g[0][None, :],
        "mem_w_kv": mem_w_kv[0].astype(BF16),
        "w_out": w_out[0].astype(BF16),
        "g_final": g_final[None, :],
    }
    return (_trunk(x_prompt, mem_prompt, w), _trunk(x_sample, mem_sample, w))
```

```python
import functools

import jax
import jax.numpy as jnp
from jax import lax
from jax.experimental import pallas as pl
from jax.experimental.pallas import tpu as pltpu

F32 = jnp.float32
BF16 = jnp.bfloat16

D_MODEL = 4096
N_MEM = 256
GLA_HEADS = 12
GLA_DK = 64
GLA_DV = 128
GLA_LR = 16
GLA_TAU = 16.0
GLA_CHUNK = 64
GLA_HK = GLA_HEADS * GLA_DK
GLA_W = GLA_HEADS * GLA_DV
MLA_HEADS = 12
MLA_Q_LORA = 1536
MLA_KV_LORA = 512
MLA_NOPE = 128
MLA_ROPE = 64
MLA_V = 128
ROPE_THETA = 10000.0
MLA_W = MLA_HEADS * MLA_V
MLA_QK_PAD = 256
MLA_V_PAD = 256
MEM_HEADS = 4
MEM_DH = 256
MEM_W = MEM_HEADS * MEM_DH
EPS = 1e-6
LANES = 128
LOG2E = 1.4426950408889634

COL_G_V = 0
COL_G_GATE = 1536
COL_M_CQ = 3072
COL_M_GATE = 4608
COL_G_Q = 6144
COL_G_K = 6912
COL_M_CKV = 7680
COL_C_Q = 8192
COL_C_GATE = 9216
COL_G_LR = 10240
COL_M_KR = 10368
P_USED = 10496
P_TN = 1536
P_COLS = 10752

VMEM_LIMIT = 60 * 1024 * 1024


def _cparams(n_axes):
    return pltpu.CompilerParams(
        dimension_semantics=("arbitrary",) * n_axes, vmem_limit_bytes=VMEM_LIMIT)


def _dot(a, b):
    return jnp.dot(a, b, preferred_element_type=F32)


def _dot_nt(a, b):
    return lax.dot_general(a, b, (((1,), (1,)), ((), ())), preferred_element_type=F32)


def _dot_tn(a, b):
    return lax.dot_general(a, b, (((0,), (0,)), ((), ())), preferred_element_type=F32)


def _silu(x):
    return x * jax.nn.sigmoid(x)


NORM_ROWS = 64
ROW_TILES_PER_W = 2


def _norm_matmul_kernel(w_transposed, x_ref, g_ref, w_ref, o_ref, h_ref):
    r = pl.program_id(2)

    @pl.when(pl.program_id(1) == 0)
    def _():
        def rows(c, carry):
            r0 = pl.multiple_of(c * NORM_ROWS, NORM_ROWS)
            x = x_ref[pl.ds(r0, NORM_ROWS), :]
            ms = jnp.mean(x * x, axis=-1, keepdims=True)
            h_ref[r, pl.ds(r0, NORM_ROWS), :] = (
                x * lax.rsqrt(ms + EPS) * g_ref[...]).astype(BF16)
            return carry
        lax.fori_loop(0, x_ref.shape[0] // NORM_ROWS, rows, 0)

    dot = _dot_nt if w_transposed else _dot
    o_ref[...] = dot(h_ref[r], w_ref[...]).astype(o_ref.dtype)


def _norm_matmul(x, g, w, tm, tn, w_transposed=False):
    m, k = x.shape
    n = w.shape[0] if w_transposed else w.shape[1]
    nr = ROW_TILES_PER_W
    if w_transposed:
        w_spec = pl.BlockSpec((tn, k), lambda i, j, r: (j, 0))
    else:
        w_spec = pl.BlockSpec((k, tn), lambda i, j, r: (0, j))

    def x_map(i, j, r):
        return (jnp.where(j == 0, i * nr + r, i * nr + nr - 1), 0)

    return pl.pallas_call(
        functools.partial(_norm_matmul_kernel, w_transposed),
        grid=(m // (nr * tm), n // tn, nr),
        in_specs=[
            pl.BlockSpec((tm, k), x_map),
            pl.BlockSpec((1, k), lambda i, j, r: (0, 0)),
            w_spec,
        ],
        out_specs=pl.BlockSpec((tm, tn), lambda i, j, r: (i * nr + r, j)),
        out_shape=jax.ShapeDtypeStruct((m, n), BF16),
        scratch_shapes=[pltpu.VMEM((nr, tm, k), BF16)],
        compiler_params=_cparams(3),
        name="norm_matmul",
    )(x, g, w)


def _gla_kernel(rev, nchunk, *refs):
    if rev:
        (q_ref, k_ref, v_ref, lr_ref, wg_ref, bg_ref, gate_ref, of_ref, gh_ref,
         o_ref, s_ref) = refs
    else:
        q_ref, k_ref, v_ref, lr_ref, wg_ref, bg_ref, o_ref, s_ref = refs
    C = GLA_CHUNK

    @pl.when(pl.program_id(1) == 0)
    def _():
        s_ref[...] = jnp.zeros_like(s_ref)

    row = lax.broadcasted_iota(jnp.int32, (C, C), 0)
    col = lax.broadcasted_iota(jnp.int32, (C, C), 1)
    tri = (col >= row) if rev else (col <= row)
    tri_bf = jnp.where(tri, 1.0, 0.0).astype(BF16)
    row2 = lax.broadcasted_iota(jnp.int32, (2 * C, 2 * C), 0)
    col2 = lax.broadcasted_iota(jnp.int32, (2 * C, 2 * C), 1)
    same_head = (row2 < C) == (col2 < C)
    amask = same_head & ((col2 > row2) if rev else (col2 <= row2))
    q_lo = lax.broadcasted_iota(jnp.int32, (C, LANES), 1) < GLA_DK

    npair = GLA_HEADS // 2

    def gate_logits(c):
        rows = pl.ds(c * C, C)
        z = _dot(lr_ref[rows, :], wg_ref[...]) + bg_ref[...]
        lg = (jnp.minimum(z, 0.0) - jnp.log1p(jnp.exp(-jnp.abs(z)))) * (1.0 / GLA_TAU)
        hi = lg.astype(BF16)
        return hi, (lg - hi.astype(F32)).astype(BF16)

    def decayed_qk(c, hi, lo):
        rows = pl.ds(c * C, C)
        b = _dot(tri_bf, hi) + _dot(tri_bf, lo)
        bl = b[0:1, :] if rev else b[C - 1:C, :]
        q = q_ref[rows, :].astype(F32)
        k = k_ref[rows, :].astype(F32)
        q_in = (q * jnp.exp(b) * (GLA_DK ** -0.5)).astype(BF16)
        k_out = (k * jnp.exp(-b)).astype(BF16)
        k_last = (k * jnp.exp(bl - b)).astype(BF16)
        return q_in, k_out, k_last, jnp.exp(bl)

    def intra(c, prep):
        rows = pl.ds(c * C, C)
        q_in, k_out, k_last, _ = prep
        out = []
        for p in range(npair):
            sl = slice(p * LANES, (p + 1) * LANES)
            ql, ko, kl = q_in[:, sl], k_out[:, sl], k_last[:, sl]
            zero = jnp.zeros_like(ql)
            qs = jnp.concatenate([jnp.where(q_lo, ql, zero), jnp.where(q_lo, zero, ql)], axis=0)
            ks = jnp.concatenate([jnp.where(q_lo, kl, zero), jnp.where(q_lo, zero, kl)], axis=0)
            vs = jnp.concatenate([v_ref[rows, (2 * p + i) * GLA_DV:(2 * p + i + 1) * GLA_DV]
                                  for i in range(2)], axis=0)
            a_raw = _dot_nt(qs, jnp.concatenate([ko, ko], axis=0))
            out.append((qs, vs, a_raw, _dot_tn(ks, vs)))
        return out

    def outputs(c, prep, pairs):
        rows = pl.ds(c * C, C)
        dec = prep[3]
        for p, (qs, vs, a_raw, kv) in enumerate(pairs):
            a = jnp.where(amask, a_raw, 0.0).astype(BF16)
            st = s_ref[p]
            o2 = _dot(jnp.concatenate([a, qs], axis=1),
                      jnp.concatenate([vs, st.astype(BF16)], axis=0))
            dec_col = jnp.broadcast_to(dec[:, p * LANES:(p + 1) * LANES], (LANES, LANES)).T
            s_ref[p] = dec_col * st + kv
            for i in range(2):
                hs = slice((2 * p + i) * GLA_DV, (2 * p + i + 1) * GLA_DV)
                o = o2[i * C:(i + 1) * C]
                if rev:
                    tot = o + of_ref[rows, hs]
                    ms = jnp.mean(tot * tot, axis=-1, keepdims=True)
                    y = tot * lax.rsqrt(ms + EPS) * gh_ref[...]
                    o_ref[rows, hs] = (y * _silu(gate_ref[rows, hs].astype(F32))).astype(o_ref.dtype)
                else:
                    o_ref[rows, hs] = o

    order = [(nchunk - 1 - i) if rev else i for i in range(nchunk)]
    prep = decayed_qk(order[0], *gate_logits(order[0]))
    for i, c in enumerate(order):
        nxt = order[i + 1] if i + 1 < nchunk else None
        if nxt is not None:
            hi_lo = gate_logits(nxt)
        pairs = intra(c, prep)
        if nxt is not None:
            prep_next = decayed_qk(nxt, *hi_lo)
        outputs(c, prep, pairs)
        if nxt is not None:
            prep = prep_next


def _gla_call(p_act, wg, bg, rev, bsz, t, tb, extra=None):
    nblk = t // tb
    nchunk = tb // GLA_CHUNK

    def rowblk(b, i):
        return b * nblk + ((nblk - 1 - i) if rev else i)

    def act(width, colblk):
        return pl.BlockSpec((tb, width), lambda b, i: (rowblk(b, i), colblk))

    def full(shape):
        return pl.BlockSpec(shape, lambda b, i: (0,) * len(shape))

    in_specs = [
        act(GLA_HK, COL_G_Q // GLA_HK),
        act(GLA_HK, COL_G_K // GLA_HK),
        act(GLA_W, COL_G_V // GLA_W),
        act(LANES, COL_G_LR // LANES),
        full(wg.shape),
        full(bg.shape),
    ]
    args = [p_act, p_act, p_act, p_act, wg, bg]
    if rev:
        o_f, g_head = extra
        in_specs += [act(GLA_W, COL_G_GATE // GLA_W), act(GLA_W, 0), full(g_head.shape)]
        args += [p_act, o_f, g_head]
        out_dtype = BF16
    else:
        out_dtype = F32
    return pl.pallas_call(
        functools.partial(_gla_kernel, rev, nchunk),
        grid=(bsz, nblk),
        in_specs=in_specs,
        out_specs=act(GLA_W, 0),
        out_shape=jax.ShapeDtypeStruct((bsz * t, GLA_W), out_dtype),
        scratch_shapes=[pltpu.VMEM((GLA_HEADS // 2, GLA_DV, LANES), F32)],
        compiler_params=_cparams(2),
        name="gla_bwd" if rev else "gla_fwd",
    )(*args)


def _rope(r, cos_t, sin_t):
    quarter = MLA_ROPE // 2
    swapped = pltpu.roll(r, quarter, 1) + pltpu.roll(r, LANES - quarter, 1)
    return r * cos_t + swapped * sin_t


def _mla_prep_kernel(cq_ref, ckv_ref, kr_ref, gq_ref, gkv_ref, wq_ref, wk_ref, wv_ref,
                     cos_ref, sin_ref, q_ref, k_ref, v_ref):
    scale = (MLA_NOPE + MLA_ROPE) ** -0.5 * LOG2E
    cos_t = cos_ref[...]
    sin_t = sin_ref[...]

    cq = cq_ref[...].astype(F32)
    ms = jnp.mean(cq * cq, axis=-1, keepdims=True)
    qn = (cq * lax.rsqrt(ms + EPS) * gq_ref[...]).astype(BF16)
    ckv = ckv_ref[...].astype(F32)
    ms = jnp.mean(ckv * ckv, axis=-1, keepdims=True)
    kvn = (ckv * lax.rsqrt(ms + EPS) * gkv_ref[...]).astype(BF16)

    k_rope = _rope(kr_ref[...].astype(F32), cos_t, sin_t).astype(BF16)
    k_nope = _dot(kvn, wk_ref[...]).astype(BF16)
    v = _dot(kvn, wv_ref[...]).astype(BF16)
    ones_col = jnp.where(
        lax.broadcasted_iota(jnp.int32, (v.shape[0], MLA_V_PAD - MLA_V), 1) == 0,
        1.0, 0.0).astype(BF16)
    for h in range(MLA_HEADS):
        c0 = h * MLA_QK_PAD
        qh = _dot(qn, wq_ref[:, c0:c0 + MLA_QK_PAD])
        q_ref[:, c0:c0 + MLA_NOPE] = (qh[:, :MLA_NOPE] * scale).astype(BF16)
        q_ref[:, c0 + MLA_NOPE:c0 + MLA_QK_PAD] = (
            _rope(qh[:, MLA_NOPE:], cos_t, sin_t) * scale).astype(BF16)
        k_ref[:, c0:c0 + MLA_NOPE] = k_nope[:, h * MLA_NOPE:(h + 1) * MLA_NOPE]
        k_ref[:, c0 + MLA_NOPE:c0 + MLA_QK_PAD] = k_rope
        v0 = h * MLA_V_PAD
        v_ref[:, v0:v0 + MLA_V] = v[:, h * MLA_V:(h + 1) * MLA_V]
        v_ref[:, v0 + MLA_V:v0 + MLA_V_PAD] = ones_col


def _mla_prep(p_act, g_q, g_kv, wq, wk, wv, cos_t, sin_t, t, tm):
    m = p_act.shape[0]
    nt = t // tm

    def act(width, colblk):
        return pl.BlockSpec((tm, width), lambda i: (i, colblk))

    def full(shape):
        return pl.BlockSpec(shape, lambda i: (0,) * len(shape))

    table = pl.BlockSpec((tm, LANES), lambda i: (i % nt, 0))
    qk_w = MLA_HEADS * MLA_QK_PAD
    return pl.pallas_call(
        _mla_prep_kernel,
        grid=(m // tm,),
        in_specs=[
            act(MLA_Q_LORA, COL_M_CQ // MLA_Q_LORA),
            act(MLA_KV_LORA, COL_M_CKV // MLA_KV_LORA),
            act(LANES, COL_M_KR // LANES),
            full(g_q.shape), full(g_kv.shape), full(wq.shape), full(wk.shape), full(wv.shape),
            table, table,
        ],
        out_specs=[act(qk_w, 0), act(qk_w, 0), act(MLA_HEADS * MLA_V_PAD, 0)],
        out_shape=[
            jax.ShapeDtypeStruct((m, qk_w), BF16),
            jax.ShapeDtypeStruct((m, qk_w), BF16),
            jax.ShapeDtypeStruct((m, MLA_HEADS * MLA_V_PAD), BF16),
        ],
        compiler_params=_cparams(1),
        name="mla_prep",
    )(p_act, p_act, p_act, g_q, g_kv, wq, wk, wv, cos_t, sin_t)


ATTN_SPLIT = 2


def _mla_attn_kernel(nkv, bk, q_ref, k_ref, v_ref, gate_ref, o_ref, s_ref):
    tq = q_ref.shape[0] // ATTN_SPLIT
    q = [q_ref[h * tq:(h + 1) * tq, :] for h in range(ATTN_SPLIT)]

    def scores(h, j):
        s_ref[2 * h + j % 2] = _dot_nt(q[h], k_ref[j * bk:(j + 1) * bk, :])

    m = [jnp.full((tq, 1), -jnp.inf, F32) for _ in range(ATTN_SPLIT)]
    acc = [jnp.zeros((tq, MLA_V_PAD), F32) for _ in range(ATTN_SPLIT)]
    for h in range(ATTN_SPLIT):
        scores(h, 0)
    for j in range(nkv):
        for h in range(ATTN_SPLIT):
            if j + 1 < nkv:
                scores(h, j + 1)
            s = s_ref[2 * h + j % 2]
            m_new = jnp.maximum(m[h], jnp.max(s, axis=-1, keepdims=True))
            alpha = jnp.exp2(m[h] - m_new)
            p = jnp.exp2(s - m_new).astype(BF16)
            acc[h] = alpha * acc[h] + _dot(p, v_ref[j * bk:(j + 1) * bk, :])
            m[h] = m_new
    for h in range(ATTN_SPLIT):
        rows = slice(h * tq, (h + 1) * tq)
        o = acc[h][:, :MLA_V] / acc[h][:, MLA_V:MLA_V + 1]
        o_ref[rows, :] = (o * _silu(gate_ref[rows, :].astype(F32))).astype(o_ref.dtype)


def _mla_attn(q, k, v, p_act, bsz, t, tq, bk):
    nq = t // tq
    nkv = t // bk
    gate_col0 = COL_M_GATE // MLA_V
    return pl.pallas_call(
        functools.partial(_mla_attn_kernel, nkv, bk),
        grid=(bsz, MLA_HEADS, nq),
        in_specs=[
            pl.BlockSpec((tq, MLA_QK_PAD), lambda b, h, i: (b * nq + i, h)),
            pl.BlockSpec((t, MLA_QK_PAD), lambda b, h, i: (b, h)),
            pl.BlockSpec((t, MLA_V_PAD), lambda b, h, i: (b, h)),
            pl.BlockSpec((tq, MLA_V), lambda b, h, i: (b * nq + i, gate_col0 + h)),
        ],
        out_specs=pl.BlockSpec((tq, MLA_V), lambda b, h, i: (b * nq + i, h)),
        out_shape=jax.ShapeDtypeStruct((bsz * t, MLA_W), BF16),
        scratch_shapes=[pltpu.VMEM((2 * ATTN_SPLIT, tq // ATTN_SPLIT, bk), F32)],
        compiler_params=_cparams(3),
        name="mla_attn",
    )(q, k, v, p_act)


def _mem_attn_kernel(q_ref, gate_ref, kv_ref, o_ref):
    heads = [slice(h * MEM_DH, (h + 1) * MEM_DH) for h in range(MEM_HEADS)]
    scores = [_dot_nt(q_ref[:, hs], kv_ref[:, hs]) for hs in heads]
    for h, hs in enumerate(heads):
        s = scores[h] * (MEM_DH ** -0.5)
        p = jnp.exp(s - jnp.max(s, axis=-1, keepdims=True))
        l = jnp.sum(p, axis=-1, keepdims=True)
        o = _dot(p.astype(BF16), kv_ref[:, MEM_W + h * MEM_DH:MEM_W + (h + 1) * MEM_DH]) / l
        o_ref[:, hs] = (o * _silu(gate_ref[:, hs].astype(F32))).astype(o_ref.dtype)


def _mem_attn(p_act, kv_mem, bsz, t, tq):
    nq = t // tq
    return pl.pallas_call(
        _mem_attn_kernel,
        grid=(bsz, nq),
        in_specs=[
            pl.BlockSpec((tq, MEM_W), lambda b, i: (b * nq + i, COL_C_Q // MEM_W)),
            pl.BlockSpec((tq, MEM_W), lambda b, i: (b * nq + i, COL_C_GATE // MEM_W)),
            pl.BlockSpec((N_MEM, 2 * MEM_W), lambda b, i: (b, 0)),
        ],
        out_specs=pl.BlockSpec((tq, MEM_W), lambda b, i: (b * nq + i, 0)),
        out_shape=jax.ShapeDtypeStruct((bsz * t, MEM_W), BF16),
        compiler_params=_cparams(2),
        name="mem_attn",
    )(p_act, p_act, kv_mem)


OUT_TN = 1024


def _out_kernel(oa_ref, ob_ref, oc_ref, w_ref, x_ref, g_ref, y_ref):
    ssq = jnp.zeros((y_ref.shape[0], 1), F32)
    for c0 in range(0, D_MODEL, OUT_TN):
        cs = slice(c0, c0 + OUT_TN)
        acc = (_dot(oa_ref[...], w_ref[0:GLA_W, cs])
               + _dot(ob_ref[...], w_ref[GLA_W:GLA_W + MLA_W, cs])
               + _dot(oc_ref[...], w_ref[GLA_W + MLA_W:, cs]) + x_ref[:, cs])
        ssq = ssq + jnp.sum(acc * acc, axis=-1, keepdims=True)
        y_ref[:, cs] = acc
    r = lax.rsqrt(ssq * (1.0 / D_MODEL) + EPS)
    for c0 in range(0, D_MODEL, OUT_TN):
        cs = slice(c0, c0 + OUT_TN)
        y_ref[:, cs] = y_ref[:, cs] * r * g_ref[:, cs]


def _out_proj(o_a, o_b, o_c, w_out, x, g_final, tm):
    m = x.shape[0]
    return pl.pallas_call(
        _out_kernel,
        grid=(m // tm,),
        in_specs=[
            pl.BlockSpec((tm, GLA_W), lambda i: (i, 0)),
            pl.BlockSpec((tm, MLA_W), lambda i: (i, 0)),
            pl.BlockSpec((tm, MEM_W), lambda i: (i, 0)),
            pl.BlockSpec((D_MODEL, D_MODEL), lambda i: (0, 0), pipeline_mode=pl.Buffered(1)),
            pl.BlockSpec((tm, D_MODEL), lambda i: (i, 0)),
            pl.BlockSpec((1, D_MODEL), lambda i: (0, 0)),
        ],
        out_specs=pl.BlockSpec((tm, D_MODEL), lambda i: (i, 0)),
        out_shape=jax.ShapeDtypeStruct((m, D_MODEL), F32),
        compiler_params=_cparams(1),
        name="out_proj",
    )(o_a, o_b, o_c, w_out, x, g_final)


IN_WIDTHS = (GLA_HK, GLA_HK, GLA_W, 2 * GLA_LR, GLA_W, MLA_Q_LORA, MLA_KV_LORA, MLA_ROPE,
             MLA_W, MEM_W, MEM_W)
IN_DST = (COL_G_Q, COL_G_K, COL_G_V, COL_G_LR, COL_G_GATE, COL_M_CQ, COL_M_CKV, COL_M_KR,
          COL_M_GATE, COL_C_Q, COL_C_GATE)
N_IN = sum(IN_WIDTHS)
W_IN_COLS = 256


def _w_in_relayout_kernel(w_ref, o_ref):
    o_ref[COL_G_LR:, :] = jnp.zeros((P_COLS - COL_G_LR, o_ref.shape[1]), o_ref.dtype)
    src = 0
    for width, dst in zip(IN_WIDTHS, IN_DST):
        o_ref[dst:dst + width, :] = w_ref[src:src + width, :].astype(o_ref.dtype)
        src += width


def _prep_w_in(w_in):
    k = w_in.shape[0]
    return pl.pallas_call(
        _w_in_relayout_kernel,
        grid=(k // W_IN_COLS,),
        in_specs=[pl.BlockSpec((N_IN, W_IN_COLS), lambda i: (0, i))],
        out_specs=pl.BlockSpec((P_COLS, W_IN_COLS), lambda i: (0, i)),
        out_shape=jax.ShapeDtypeStruct((P_COLS, k), BF16),
        compiler_params=_cparams(1),
        name="w_in_relayout",
    )(w_in.T)


def _prep_gla_gate(w_g2, b_g):
    wgs = []
    for d in range(2):
        w = jnp.zeros((LANES, GLA_HK), F32).at[d * GLA_LR:(d + 1) * GLA_LR].set(w_g2[d])
        wgs.append(w.astype(BF16))
    return wgs, [b_g[0][None, :], b_g[1][None, :]]


def _prep_mla(w_uq, w_ukv):
    wq = w_uq.reshape(MLA_Q_LORA, MLA_HEADS, MLA_NOPE + MLA_ROPE)
    wq = jnp.pad(wq, ((0, 0), (0, 0), (0, MLA_QK_PAD - MLA_NOPE - MLA_ROPE)))
    wq = wq.reshape(MLA_Q_LORA, MLA_HEADS * MLA_QK_PAD).astype(BF16)
    wkv = w_ukv.reshape(MLA_KV_LORA, MLA_HEADS, MLA_NOPE + MLA_V)
    wk = wkv[:, :, :MLA_NOPE].reshape(MLA_KV_LORA, MLA_HEADS * MLA_NOPE).astype(BF16)
    wv = wkv[:, :, MLA_NOPE:].reshape(MLA_KV_LORA, MLA_W).astype(BF16)
    return wq, wk, wv


def _rope_tables(t):
    pos = jnp.arange(t, dtype=F32)
    inv = 1.0 / (ROPE_THETA ** (jnp.arange(0, MLA_ROPE, 2, dtype=F32) / MLA_ROPE))
    ang = pos[:, None] * inv[None, :]
    cos, sin = jnp.cos(ang), jnp.sin(ang)
    z = jnp.zeros((t, LANES - MLA_ROPE), F32)
    return (jnp.concatenate([cos, cos, z], axis=-1), jnp.concatenate([-sin, sin, z], axis=-1))


def _pick(t, pref):
    return pref if t % pref == 0 else t


def _trunk(x, mem, w):
    bsz, t, _ = x.shape
    m = bsz * t
    xf = x.reshape(m, D_MODEL)
    p_act = _norm_matmul(xf, w["g_in"], w["w_in_t"], min(512, m // ROW_TILES_PER_W), P_TN,
                         w_transposed=True)

    tb = _pick(t, 1024)
    o_f = _gla_call(p_act, w["wg"][0], w["bg"][0], False, bsz, t, tb)
    o_a = _gla_call(p_act, w["wg"][1], w["bg"][1], True, bsz, t, tb, (o_f, w["g_head"]))

    cos_t, sin_t = _rope_tables(t)
    q, k, v = _mla_prep(p_act, w["g_q"], w["g_kv"], w["wq"], w["wk"], w["wv"], cos_t, sin_t,
                        t, _pick(t, 512))
    o_b = _mla_attn(q, k, v, p_act, bsz, t, _pick(t, 1024), _pick(t, 1024))

    kv_mem = _norm_matmul(mem.reshape(bsz * N_MEM, D_MODEL), w["mem_g"], w["mem_w_kv"],
                          bsz * N_MEM // ROW_TILES_PER_W, 512)
    o_c = _mem_attn(p_act, kv_mem, bsz, t, _pick(t, 512))

    y = _out_proj(o_a, o_b, o_c, w["w_out"], xf, w["g_final"], _pick(m, 256))
    return y.reshape(bsz, t, D_MODEL)


def kernel(x_prompt, x_sample, mem_prompt, mem_sample, g_in, w_in, gla_w_g2, gla_b_g, gla_g_head,
           mla_g_q, mla_w_uq, mla_g_kv, mla_w_ukv, mem_g, mem_w_kv, w_out, g_final):
    wg, bg = _prep_gla_gate(gla_w_g2[0], gla_b_g[0])
    wq, wk, wv = _prep_mla(mla_w_uq[0], mla_w_ukv[0])
    w = {
        "g_in": g_in[0][None, :],
        "w_in_t": _prep_w_in(w_in[0]),
        "wg": wg, "bg": bg,
        "g_head": gla_g_head[0][None, :],
        "g_q": mla_g_q[0][None, :], "g_kv": mla_g_kv[0][None, :],
        "wq": wq, "wk": wk, "wv": wv,
        "mem_g": mem_g[0][None, :],
        "mem_w_kv": mem_w_kv[0].astype(BF16),
        "w_out": w_out[0].astype(BF16),
        "g_final": g_final[None, :],
    }
    return (_trunk(x_prompt, mem_prompt, w), _trunk(x_sample, mem_sample, w))
```

```python
import functools

import jax
import jax.numpy as jnp
from jax import lax
from jax.experimental import pallas as pl
from jax.experimental.pallas import tpu as pltpu

F32 = jnp.float32
BF16 = jnp.bfloat16

D_MODEL = 4096
N_MEM = 256
GLA_HEADS = 12
GLA_DK = 64
GLA_DV = 128
GLA_LR = 16
GLA_TAU = 16.0
GLA_CHUNK = 64
GLA_HK = GLA_HEADS * GLA_DK
GLA_W = GLA_HEADS * GLA_DV
MLA_HEADS = 12
MLA_Q_LORA = 1536
MLA_KV_LORA = 512
MLA_NOPE = 128
MLA_ROPE = 64
MLA_V = 128
ROPE_THETA = 10000.0
MLA_W = MLA_HEADS * MLA_V
MLA_QK_PAD = 256
MLA_V_PAD = 256
MEM_HEADS = 4
MEM_DH = 256
MEM_W = MEM_HEADS * MEM_DH
EPS = 1e-6
LANES = 128
LOG2E = 1.4426950408889634

COL_G_V = 0
COL_G_GATE = 1536
COL_M_CQ = 3072
COL_M_GATE = 4608
COL_G_Q = 6144
COL_G_K = 6912
COL_M_CKV = 7680
COL_C_Q = 8192
COL_C_GATE = 9216
COL_G_LR = 10240
COL_M_KR = 10368
P_USED = 10496
P_TN = 1536
P_COLS = 10752

VMEM_LIMIT = 60 * 1024 * 1024


def _cparams(n_axes):
    return pltpu.CompilerParams(
        dimension_semantics=("arbitrary",) * n_axes, vmem_limit_bytes=VMEM_LIMIT)


def _dot(a, b):
    return jnp.dot(a, b, preferred_element_type=F32)


def _dot_nt(a, b):
    return lax.dot_general(a, b, (((1,), (1,)), ((), ())), preferred_element_type=F32)


def _dot_tn(a, b):
    return lax.dot_general(a, b, (((0,), (0,)), ((), ())), preferred_element_type=F32)


def _silu(x):
    return x * jax.nn.sigmoid(x)


NORM_ROWS = 64
BF16_SUBLANES = 16


def _norm_matmul_kernel(w_transposed, chunk, x_ref, g_ref, w_ref, o_ref, h_ref):
    i = pl.program_id(0)
    j = pl.program_id(1)
    tm = x_ref.shape[0]
    dot = _dot_nt if w_transposed else _dot

    def normalise(slot, r0, rows):
        x = x_ref[pl.ds(r0, rows), :]
        ms = jnp.mean(x * x, axis=-1, keepdims=True)
        h_ref[slot, pl.ds(r0, rows), :] = (x * lax.rsqrt(ms + EPS) * g_ref[...]).astype(BF16)

    @pl.when(j == 0)
    def _():
        @pl.when(i == 0)
        def _():
            def rows(c, carry):
                normalise(0, pl.multiple_of(c * NORM_ROWS, NORM_ROWS), NORM_ROWS)
                return carry
            lax.fori_loop(0, tm // NORM_ROWS, rows, 0)

        o_ref[...] = dot(h_ref[i % 2], w_ref[...]).astype(o_ref.dtype)

    @pl.when(j > 0)
    def _():
        r0 = pl.multiple_of(jnp.minimum((j - 1) * chunk, tm - chunk), BF16_SUBLANES)
        o_ref[...] = dot(h_ref[i % 2], w_ref[...]).astype(o_ref.dtype)
        normalise((i + 1) % 2, r0, chunk)


def _norm_matmul(x, g, w, tm, tn, w_transposed=False):
    m, k = x.shape
    n = w.shape[0] if w_transposed else w.shape[1]
    ni, nj = m // tm, n // tn
    chunk = -(-tm // (nj - 1))
    chunk = -(-chunk // BF16_SUBLANES) * BF16_SUBLANES
    assert chunk * (nj - 1) >= tm and (tm - chunk) % BF16_SUBLANES == 0

    def x_map(i, j):
        return (jnp.where((i == 0) & (j == 0), 0, jnp.minimum(i + 1, ni - 1)), 0)

    if w_transposed:
        w_spec = pl.BlockSpec((tn, k), lambda i, j: (j, 0))
    else:
        w_spec = pl.BlockSpec((k, tn), lambda i, j: (0, j))
    return pl.pallas_call(
        functools.partial(_norm_matmul_kernel, w_transposed, chunk),
        grid=(ni, nj),
        in_specs=[
            pl.BlockSpec((tm, k), x_map),
            pl.BlockSpec((1, k), lambda i, j: (0, 0)),
            w_spec,
        ],
        out_specs=pl.BlockSpec((tm, tn), lambda i, j: (i, j)),
        out_shape=jax.ShapeDtypeStruct((m, n), BF16),
        scratch_shapes=[pltpu.VMEM((2, tm, k), BF16)],
        compiler_params=_cparams(2),
        name="norm_matmul",
    )(x, g, w)


def _gla_kernel(rev, nchunk, *refs):
    if rev:
        (q_ref, k_ref, v_ref, lr_ref, wg_ref, bg_ref, gate_ref, of_ref, gh_ref,
         o_ref, s_ref) = refs
    else:
        q_ref, k_ref, v_ref, lr_ref, wg_ref, bg_ref, o_ref, s_ref = refs
    C = GLA_CHUNK

    @pl.when(pl.program_id(1) == 0)
    def _():
        s_ref[...] = jnp.zeros_like(s_ref)

    row = lax.broadcasted_iota(jnp.int32, (C, C), 0)
    col = lax.broadcasted_iota(jnp.int32, (C, C), 1)
    tri = (col >= row) if rev else (col <= row)
    tri_bf = jnp.where(tri, 1.0, 0.0).astype(BF16)
    row2 = lax.broadcasted_iota(jnp.int32, (2 * C, 2 * C), 0)
    col2 = lax.broadcasted_iota(jnp.int32, (2 * C, 2 * C), 1)
    same_head = (row2 < C) == (col2 < C)
    amask = same_head & ((col2 > row2) if rev else (col2 <= row2))
    q_lo = lax.broadcasted_iota(jnp.int32, (C, LANES), 1) < GLA_DK

    npair = GLA_HEADS // 2

    def gate_logits(c):
        rows = pl.ds(c * C, C)
        z = _dot(lr_ref[rows, :], wg_ref[...]) + bg_ref[...]
        lg = (jnp.minimum(z, 0.0) - jnp.log1p(jnp.exp(-jnp.abs(z)))) * (1.0 / GLA_TAU)
        hi = lg.astype(BF16)
        return hi, (lg - hi.astype(F32)).astype(BF16)

    def decayed_qk(c, hi, lo):
        rows = pl.ds(c * C, C)
        b = _dot(tri_bf, hi) + _dot(tri_bf, lo)
        bl = b[0:1, :] if rev else b[C - 1:C, :]
        q = q_ref[rows, :].astype(F32)
        k = k_ref[rows, :].astype(F32)
        q_in = (q * jnp.exp(b) * (GLA_DK ** -0.5)).astype(BF16)
        k_out = (k * jnp.exp(-b)).astype(BF16)
        k_last = (k * jnp.exp(bl - b)).astype(BF16)
        return q_in, k_out, k_last, jnp.exp(bl)

    def intra(c, prep):
        rows = pl.ds(c * C, C)
        q_in, k_out, k_last, _ = prep
        out = []
        for p in range(npair):
            sl = slice(p * LANES, (p + 1) * LANES)
            ql, ko, kl = q_in[:, sl], k_out[:, sl], k_last[:, sl]
            zero = jnp.zeros_like(ql)
            qs = jnp.concatenate([jnp.where(q_lo, ql, zero), jnp.where(q_lo, zero, ql)], axis=0)
            ks = jnp.concatenate([jnp.where(q_lo, kl, zero), jnp.where(q_lo, zero, kl)], axis=0)
            vs = jnp.concatenate([v_ref[rows, (2 * p + i) * GLA_DV:(2 * p + i + 1) * GLA_DV]
                                  for i in range(2)], axis=0)
            a_raw = _dot_nt(qs, jnp.concatenate([ko, ko], axis=0))
            out.append((qs, vs, a_raw, _dot_tn(ks, vs)))
        return out

    def outputs(c, prep, pairs):
        rows = pl.ds(c * C, C)
        dec = prep[3]
        for p, (qs, vs, a_raw, kv) in enumerate(pairs):
            a = jnp.where(amask, a_raw, 0.0).astype(BF16)
            st = s_ref[p]
            o2 = _dot(jnp.concatenate([a, qs], axis=1),
                      jnp.concatenate([vs, st.astype(BF16)], axis=0))
            dec_col = jnp.broadcast_to(dec[:, p * LANES:(p + 1) * LANES], (LANES, LANES)).T
            s_ref[p] = dec_col * st + kv
            for i in range(2):
                hs = slice((2 * p + i) * GLA_DV, (2 * p + i + 1) * GLA_DV)
                o = o2[i * C:(i + 1) * C]
                if rev:
                    tot = o + of_ref[rows, hs]
                    ms = jnp.mean(tot * tot, axis=-1, keepdims=True)
                    y = tot * lax.rsqrt(ms + EPS) * gh_ref[...]
                    o_ref[rows, hs] = (y * _silu(gate_ref[rows, hs].astype(F32))).astype(o_ref.dtype)
                else:
                    o_ref[rows, hs] = o

    order = [(nchunk - 1 - i) if rev else i for i in range(nchunk)]
    prep = decayed_qk(order[0], *gate_logits(order[0]))
    for i, c in enumerate(order):
        nxt = order[i + 1] if i + 1 < nchunk else None
        if nxt is not None:
            hi_lo = gate_logits(nxt)
        pairs = intra(c, prep)
        if nxt is not None:
            prep_next = decayed_qk(nxt, *hi_lo)
        outputs(c, prep, pairs)
        if nxt is not None:
            prep = prep_next


def _gla_call(p_act, wg, bg, rev, bsz, t, tb, extra=None):
    nblk = t // tb
    nchunk = tb // GLA_CHUNK

    def rowblk(b, i):
        return b * nblk + ((nblk - 1 - i) if rev else i)

    def act(width, colblk):
        return pl.BlockSpec((tb, width), lambda b, i: (rowblk(b, i), colblk))

    def full(shape):
        return pl.BlockSpec(shape, lambda b, i: (0,) * len(shape))

    in_specs = [
        act(GLA_HK, COL_G_Q // GLA_HK),
        act(GLA_HK, COL_G_K // GLA_HK),
        act(GLA_W, COL_G_V // GLA_W),
        act(LANES, COL_G_LR // LANES),
        full(wg.shape),
        full(bg.shape),
    ]
    args = [p_act, p_act, p_act, p_act, wg, bg]
    if rev:
        o_f, g_head = extra
        in_specs += [act(GLA_W, COL_G_GATE // GLA_W), act(GLA_W, 0), full(g_head.shape)]
        args += [p_act, o_f, g_head]
        out_dtype = BF16
    else:
        out_dtype = F32
    return pl.pallas_call(
        functools.partial(_gla_kernel, rev, nchunk),
        grid=(bsz, nblk),
        in_specs=in_specs,
        out_specs=act(GLA_W, 0),
        out_shape=jax.ShapeDtypeStruct((bsz * t, GLA_W), out_dtype),
        scratch_shapes=[pltpu.VMEM((GLA_HEADS // 2, GLA_DV, LANES), F32)],
        compiler_params=_cparams(2),
        name="gla_bwd" if rev else "gla_fwd",
    )(*args)


def _rope(r, cos_t, sin_t):
    quarter = MLA_ROPE // 2
    swapped = pltpu.roll(r, quarter, 1) + pltpu.roll(r, LANES - quarter, 1)
    return r * cos_t + swapped * sin_t


def _mla_prep_kernel(cq_ref, ckv_ref, kr_ref, gq_ref, gkv_ref, wq_ref, wk_ref, wv_ref,
                     cos_ref, sin_ref, q_ref, k_ref, v_ref):
    scale = (MLA_NOPE + MLA_ROPE) ** -0.5 * LOG2E
    cos_t = cos_ref[...]
    sin_t = sin_ref[...]

    cq = cq_ref[...].astype(F32)
    ms = jnp.mean(cq * cq, axis=-1, keepdims=True)
    qn = (cq * lax.rsqrt(ms + EPS) * gq_ref[...]).astype(BF16)
    ckv = ckv_ref[...].astype(F32)
    ms = jnp.mean(ckv * ckv, axis=-1, keepdims=True)
    kvn = (ckv * lax.rsqrt(ms + EPS) * gkv_ref[...]).astype(BF16)

    k_rope = _rope(kr_ref[...].astype(F32), cos_t, sin_t).astype(BF16)
    k_nope = _dot(kvn, wk_ref[...]).astype(BF16)
    v = _dot(kvn, wv_ref[...]).astype(BF16)
    ones_col = jnp.where(
        lax.broadcasted_iota(jnp.int32, (v.shape[0], MLA_V_PAD - MLA_V), 1) == 0,
        1.0, 0.0).astype(BF16)
    for h in range(MLA_HEADS):
        c0 = h * MLA_QK_PAD
        qh = _dot(qn, wq_ref[:, c0:c0 + MLA_QK_PAD])
        q_ref[:, c0:c0 + MLA_NOPE] = (qh[:, :MLA_NOPE] * scale).astype(BF16)
        q_ref[:, c0 + MLA_NOPE:c0 + MLA_QK_PAD] = (
            _rope(qh[:, MLA_NOPE:], cos_t, sin_t) * scale).astype(BF16)
        k_ref[:, c0:c0 + MLA_NOPE] = k_nope[:, h * MLA_NOPE:(h + 1) * MLA_NOPE]
        k_ref[:, c0 + MLA_NOPE:c0 + MLA_QK_PAD] = k_rope
        v0 = h * MLA_V_PAD
        v_ref[:, v0:v0 + MLA_V] = v[:, h * MLA_V:(h + 1) * MLA_V]
        v_ref[:, v0 + MLA_V:v0 + MLA_V_PAD] = ones_col


def _mla_prep(p_act, g_q, g_kv, wq, wk, wv, cos_t, sin_t, t, tm):
    m = p_act.shape[0]
    nt = t // tm

    def act(width, colblk):
        return pl.BlockSpec((tm, width), lambda i: (i, colblk))

    def full(shape):
        return pl.BlockSpec(shape, lambda i: (0,) * len(shape))

    table = pl.BlockSpec((tm, LANES), lambda i: (i % nt, 0))
    qk_w = MLA_HEADS * MLA_QK_PAD
    return pl.pallas_call(
        _mla_prep_kernel,
        grid=(m // tm,),
        in_specs=[
            act(MLA_Q_LORA, COL_M_CQ // MLA_Q_LORA),
            act(MLA_KV_LORA, COL_M_CKV // MLA_KV_LORA),
            act(LANES, COL_M_KR // LANES),
            full(g_q.shape), full(g_kv.shape), full(wq.shape), full(wk.shape), full(wv.shape),
            table, table,
        ],
        out_specs=[act(qk_w, 0), act(qk_w, 0), act(MLA_HEADS * MLA_V_PAD, 0)],
        out_shape=[
            jax.ShapeDtypeStruct((m, qk_w), BF16),
            jax.ShapeDtypeStruct((m, qk_w), BF16),
            jax.ShapeDtypeStruct((m, MLA_HEADS * MLA_V_PAD), BF16),
        ],
        compiler_params=_cparams(1),
        name="mla_prep",
    )(p_act, p_act, p_act, g_q, g_kv, wq, wk, wv, cos_t, sin_t)


ATTN_SPLIT = 2


def _mla_attn_kernel(nkv, bk, q_ref, k_ref, v_ref, gate_ref, o_ref, s_ref):
    tq = q_ref.shape[0] // ATTN_SPLIT
    q = [q_ref[h * tq:(h + 1) * tq, :] for h in range(ATTN_SPLIT)]

    def scores(h, j):
        s_ref[2 * h + j % 2] = _dot_nt(q[h], k_ref[j * bk:(j + 1) * bk, :])

    m = [jnp.full((tq, 1), -jnp.inf, F32) for _ in range(ATTN_SPLIT)]
    acc = [jnp.zeros((tq, MLA_V_PAD), F32) for _ in range(ATTN_SPLIT)]
    for h in range(ATTN_SPLIT):
        scores(h, 0)
    for j in range(nkv):
        for h in range(ATTN_SPLIT):
            if j + 1 < nkv:
                scores(h, j + 1)
            s = s_ref[2 * h + j % 2]
            m_new = jnp.maximum(m[h], jnp.max(s, axis=-1, keepdims=True))
            alpha = jnp.exp2(m[h] - m_new)
            p = jnp.exp2(s - m_new).astype(BF16)
            acc[h] = alpha * acc[h] + _dot(p, v_ref[j * bk:(j + 1) * bk, :])
            m[h] = m_new
    for h in range(ATTN_SPLIT):
        rows = slice(h * tq, (h + 1) * tq)
        o = acc[h][:, :MLA_V] / acc[h][:, MLA_V:MLA_V + 1]
        o_ref[rows, :] = (o * _silu(gate_ref[rows, :].astype(F32))).astype(o_ref.dtype)


def _mla_attn(q, k, v, p_act, bsz, t, tq, bk):
    nq = t // tq
    nkv = t // bk
    gate_col0 = COL_M_GATE // MLA_V
    return pl.pallas_call(
        functools.partial(_mla_attn_kernel, nkv, bk),
        grid=(bsz, MLA_HEADS, nq),
        in_specs=[
            pl.BlockSpec((tq, MLA_QK_PAD), lambda b, h, i: (b * nq + i, h)),
            pl.BlockSpec((t, MLA_QK_PAD), lambda b, h, i: (b, h)),
            pl.BlockSpec((t, MLA_V_PAD), lambda b, h, i: (b, h)),
            pl.BlockSpec((tq, MLA_V), lambda b, h, i: (b * nq + i, gate_col0 + h)),
        ],
        out_specs=pl.BlockSpec((tq, MLA_V), lambda b, h, i: (b * nq + i, h)),
        out_shape=jax.ShapeDtypeStruct((bsz * t, MLA_W), BF16),
        scratch_shapes=[pltpu.VMEM((2 * ATTN_SPLIT, tq // ATTN_SPLIT, bk), F32)],
        compiler_params=_cparams(3),
        name="mla_attn",
    )(q, k, v, p_act)


def _mem_attn_kernel(q_ref, gate_ref, kv_ref, o_ref):
    heads = [slice(h * MEM_DH, (h + 1) * MEM_DH) for h in range(MEM_HEADS)]
    scores = [_dot_nt(q_ref[:, hs], kv_ref[:, hs]) for hs in heads]
    for h, hs in enumerate(heads):
        s = scores[h] * (MEM_DH ** -0.5)
        p = jnp.exp(s - jnp.max(s, axis=-1, keepdims=True))
        l = jnp.sum(p, axis=-1, keepdims=True)
        o = _dot(p.astype(BF16), kv_ref[:, MEM_W + h * MEM_DH:MEM_W + (h + 1) * MEM_DH]) / l
        o_ref[:, hs] = (o * _silu(gate_ref[:, hs].astype(F32))).astype(o_ref.dtype)


def _mem_attn(p_act, kv_mem, bsz, t, tq):
    nq = t // tq
    return pl.pallas_call(
        _mem_attn_kernel,
        grid=(bsz, nq),
        in_specs=[
            pl.BlockSpec((tq, MEM_W), lambda b, i: (b * nq + i, COL_C_Q // MEM_W)),
            pl.BlockSpec((tq, MEM_W), lambda b, i: (b * nq + i, COL_C_GATE // MEM_W)),
            pl.BlockSpec((N_MEM, 2 * MEM_W), lambda b, i: (b, 0)),
        ],
        out_specs=pl.BlockSpec((tq, MEM_W), lambda b, i: (b * nq + i, 0)),
        out_shape=jax.ShapeDtypeStruct((bsz * t, MEM_W), BF16),
        compiler_params=_cparams(2),
        name="mem_attn",
    )(p_act, p_act, kv_mem)


OUT_TN = 1024


def _out_kernel(oa_ref, ob_ref, oc_ref, w_ref, x_ref, g_ref, y_ref):
    ssq = jnp.zeros((y_ref.shape[0], 1), F32)
    for c0 in range(0, D_MODEL, OUT_TN):
        cs = slice(c0, c0 + OUT_TN)
        acc = (_dot(oa_ref[...], w_ref[0:GLA_W, cs])
               + _dot(ob_ref[...], w_ref[GLA_W:GLA_W + MLA_W, cs])
               + _dot(oc_ref[...], w_ref[GLA_W + MLA_W:, cs]) + x_ref[:, cs])
        ssq = ssq + jnp.sum(acc * acc, axis=-1, keepdims=True)
        y_ref[:, cs] = acc
    r = lax.rsqrt(ssq * (1.0 / D_MODEL) + EPS)
    for c0 in range(0, D_MODEL, OUT_TN):
        cs = slice(c0, c0 + OUT_TN)
        y_ref[:, cs] = y_ref[:, cs] * r * g_ref[:, cs]


def _out_proj(o_a, o_b, o_c, w_out, x, g_final, tm):
    m = x.shape[0]
    return pl.pallas_call(
        _out_kernel,
        grid=(m // tm,),
        in_specs=[
            pl.BlockSpec((tm, GLA_W), lambda i: (i, 0)),
            pl.BlockSpec((tm, MLA_W), lambda i: (i, 0)),
            pl.BlockSpec((tm, MEM_W), lambda i: (i, 0)),
            pl.BlockSpec((D_MODEL, D_MODEL), lambda i: (0, 0), pipeline_mode=pl.Buffered(1)),
            pl.BlockSpec((tm, D_MODEL), lambda i: (i, 0)),
            pl.BlockSpec((1, D_MODEL), lambda i: (0, 0)),
        ],
        out_specs=pl.BlockSpec((tm, D_MODEL), lambda i: (i, 0)),
        out_shape=jax.ShapeDtypeStruct((m, D_MODEL), F32),
        compiler_params=_cparams(1),
        name="out_proj",
    )(o_a, o_b, o_c, w_out, x, g_final)


IN_WIDTHS = (GLA_HK, GLA_HK, GLA_W, 2 * GLA_LR, GLA_W, MLA_Q_LORA, MLA_KV_LORA, MLA_ROPE,
             MLA_W, MEM_W, MEM_W)
IN_DST = (COL_G_Q, COL_G_K, COL_G_V, COL_G_LR, COL_G_GATE, COL_M_CQ, COL_M_CKV, COL_M_KR,
          COL_M_GATE, COL_C_Q, COL_C_GATE)
N_IN = sum(IN_WIDTHS)
W_IN_COLS = 256


def _w_in_relayout_kernel(w_ref, o_ref):
    o_ref[COL_G_LR:, :] = jnp.zeros((P_COLS - COL_G_LR, o_ref.shape[1]), o_ref.dtype)
    src = 0
    for width, dst in zip(IN_WIDTHS, IN_DST):
        o_ref[dst:dst + width, :] = w_ref[src:src + width, :].astype(o_ref.dtype)
        src += width


def _prep_w_in(w_in):
    k = w_in.shape[0]
    return pl.pallas_call(
        _w_in_relayout_kernel,
        grid=(k // W_IN_COLS,),
        in_specs=[pl.BlockSpec((N_IN, W_IN_COLS), lambda i: (0, i))],
        out_specs=pl.BlockSpec((P_COLS, W_IN_COLS), lambda i: (0, i)),
        out_shape=jax.ShapeDtypeStruct((P_COLS, k), BF16),
        compiler_params=_cparams(1),
        name="w_in_relayout",
    )(w_in.T)


def _prep_gla_gate(w_g2, b_g):
    wgs = []
    for d in range(2):
        w = jnp.zeros((LANES, GLA_HK), F32).at[d * GLA_LR:(d + 1) * GLA_LR].set(w_g2[d])
        wgs.append(w.astype(BF16))
    return wgs, [b_g[0][None, :], b_g[1][None, :]]


def _prep_mla(w_uq, w_ukv):
    wq = w_uq.reshape(MLA_Q_LORA, MLA_HEADS, MLA_NOPE + MLA_ROPE)
    wq = jnp.pad(wq, ((0, 0), (0, 0), (0, MLA_QK_PAD - MLA_NOPE - MLA_ROPE)))
    wq = wq.reshape(MLA_Q_LORA, MLA_HEADS * MLA_QK_PAD).astype(BF16)
    wkv = w_ukv.reshape(MLA_KV_LORA, MLA_HEADS, MLA_NOPE + MLA_V)
    wk = wkv[:, :, :MLA_NOPE].reshape(MLA_KV_LORA, MLA_HEADS * MLA_NOPE).astype(BF16)
    wv = wkv[:, :, MLA_NOPE:].reshape(MLA_KV_LORA, MLA_W).astype(BF16)
    return wq, wk, wv


def _rope_tables(t):
    pos = jnp.arange(t, dtype=F32)
    inv = 1.0 / (ROPE_THETA ** (jnp.arange(0, MLA_ROPE, 2, dtype=F32) / MLA_ROPE))
    ang = pos[:, None] * inv[None, :]
    cos, sin = jnp.cos(ang), jnp.sin(ang)
    z = jnp.zeros((t, LANES - MLA_ROPE), F32)
    return (jnp.concatenate([cos, cos, z], axis=-1), jnp.concatenate([-sin, sin, z], axis=-1))


def _pick(t, pref):
    return pref if t % pref == 0 else t


def _trunk(x, mem, w):
    bsz, t, _ = x.shape
    m = bsz * t
    xf = x.reshape(m, D_MODEL)
    p_act = _norm_matmul(xf, w["g_in"], w["w_in_t"], _pick(m, 512), P_TN, w_transposed=True)

    tb = _pick(t, 1024)
    o_f = _gla_call(p_act, w["wg"][0], w["bg"][0], False, bsz, t, tb)
    o_a = _gla_call(p_act, w["wg"][1], w["bg"][1], True, bsz, t, tb, (o_f, w["g_head"]))

    cos_t, sin_t = _rope_tables(t)
    q, k, v = _mla_prep(p_act, w["g_q"], w["g_kv"], w["wq"], w["wk"], w["wv"], cos_t, sin_t,
                        t, _pick(t, 512))
    o_b = _mla_attn(q, k, v, p_act, bsz, t, _pick(t, 512 * ATTN_SPLIT), _pick(t, 1024))

    kv_mem = _norm_matmul(mem.reshape(bsz * N_MEM, D_MODEL), w["mem_g"], w["mem_w_kv"],
                          bsz * N_MEM, 512)
    o_c = _mem_attn(p_act, kv_mem, bsz, t, _pick(t, 512))

    y = _out_proj(o_a, o_b, o_c, w["w_out"], xf, w["g_final"], _pick(m, 256))
    return y.reshape(bsz, t, D_MODEL)


def kernel(x_prompt, x_sample, mem_prompt, mem_sample, g_in, w_in, gla_w_g2, gla_b_g, gla_g_head,
           mla_g_q, mla_w_uq, mla_g_kv, mla_w_ukv, mem_g, mem_w_kv, w_out, g_final):
    wg, bg = _prep_gla_gate(gla_w_g2[0], gla_b_g[0])
    wq, wk, wv = _prep_mla(mla_w_uq[0], mla_w_ukv[0])
    w = {
        "g_in": g_in[0][None, :],
        "w_in_t": _prep_w_in(w_in[0]),
        "wg": wg, "bg": bg,
        "g_head": gla_g_head[0][None, :],
        "g_q": mla_g_q[0][None, :], "g_kv": mla_g_kv[0][None, :],
        "wq": wq, "wk": wk, "wv": wv,
        "mem_g": mem_g[0][None, :],
        "mem_w_kv": mem_w_kv[0].astype(BF16),
        "w_out": w_out[0].astype(BF16),
        "g_final": g_final[None, :],
    }
    return (_trunk(x_prompt, mem_prompt, w), _trunk(x_sample, mem_sample, w))
```

```python
import functools
import math

import jax
import jax.numpy as jnp
from jax import lax
from jax.experimental import pallas as pl
from jax.experimental.pallas import tpu as pltpu

F32 = jnp.float32
BF16 = jnp.bfloat16

D_MODEL = 4096
N_MEM = 256
GLA_HEADS = 12
GLA_DK = 64
GLA_DV = 128
GLA_LR = 16
GLA_TAU = 16.0
assert math.log2(GLA_TAU).is_integer()
GLA_CHUNK = 64
GLA_HK = GLA_HEADS * GLA_DK
GLA_W = GLA_HEADS * GLA_DV
MLA_HEADS = 12
MLA_Q_LORA = 1536
MLA_KV_LORA = 512
MLA_NOPE = 128
MLA_ROPE = 64
MLA_V = 128
ROPE_THETA = 10000.0
MLA_W = MLA_HEADS * MLA_V
MLA_QK_PAD = 256
MLA_V_PAD = 256
MEM_HEADS = 4
MEM_DH = 256
MEM_W = MEM_HEADS * MEM_DH
EPS = 1e-6
LANES = 128
LOG2E = 1.4426950408889634

COL_G_V = 0
COL_G_GATE = 1536
COL_M_CQ = 3072
COL_M_GATE = 4608
COL_G_Q = 6144
COL_G_K = 6912
COL_M_CKV = 7680
COL_C_Q = 8192
COL_C_GATE = 9216
COL_G_LR = 10240
COL_M_KR = 10368
P_USED = 10496
P_TN = 1536
P_COLS = 10752

VMEM_LIMIT = 60 * 1024 * 1024


def _cparams(n_axes):
    return pltpu.CompilerParams(
        dimension_semantics=("arbitrary",) * n_axes, vmem_limit_bytes=VMEM_LIMIT)


def _dot(a, b):
    return jnp.dot(a, b, preferred_element_type=F32)


def _dot_nt(a, b):
    return lax.dot_general(a, b, (((1,), (1,)), ((), ())), preferred_element_type=F32)


def _dot_tn(a, b):
    return lax.dot_general(a, b, (((0,), (0,)), ((), ())), preferred_element_type=F32)


def _silu(x):
    return x * jax.nn.sigmoid(x)


NORM_ROWS = 64
BF16_SUBLANES = 16


def _norm_matmul_kernel(w_transposed, chunk, x_ref, g_ref, w_ref, o_ref, h_ref):
    i = pl.program_id(0)
    j = pl.program_id(1)
    tm = x_ref.shape[0]
    dot = _dot_nt if w_transposed else _dot

    def normalise(slot, r0, rows):
        x = x_ref[pl.ds(r0, rows), :]
        ms = jnp.mean(x * x, axis=-1, keepdims=True)
        h_ref[slot, pl.ds(r0, rows), :] = (x * lax.rsqrt(ms + EPS) * g_ref[...]).astype(BF16)

    @pl.when(j == 0)
    def _():
        @pl.when(i == 0)
        def _():
            def rows(c, carry):
                normalise(0, pl.multiple_of(c * NORM_ROWS, NORM_ROWS), NORM_ROWS)
                return carry
            lax.fori_loop(0, tm // NORM_ROWS, rows, 0)

        o_ref[...] = dot(h_ref[i % 2], w_ref[...]).astype(o_ref.dtype)

    @pl.when(j > 0)
    def _():
        r0 = pl.multiple_of(jnp.minimum((j - 1) * chunk, tm - chunk), BF16_SUBLANES)
        o_ref[...] = dot(h_ref[i % 2], w_ref[...]).astype(o_ref.dtype)
        normalise((i + 1) % 2, r0, chunk)


def _norm_matmul(x, g, w, tm, tn, w_transposed=False):
    m, k = x.shape
    n = w.shape[0] if w_transposed else w.shape[1]
    ni, nj = m // tm, n // tn
    chunk = -(-tm // (nj - 1))
    chunk = -(-chunk // BF16_SUBLANES) * BF16_SUBLANES
    assert chunk * (nj - 1) >= tm and (tm - chunk) % BF16_SUBLANES == 0

    def x_map(i, j):
        return (jnp.where((i == 0) & (j == 0), 0, jnp.minimum(i + 1, ni - 1)), 0)

    if w_transposed:
        w_spec = pl.BlockSpec((tn, k), lambda i, j: (j, 0))
    else:
        w_spec = pl.BlockSpec((k, tn), lambda i, j: (0, j))
    return pl.pallas_call(
        functools.partial(_norm_matmul_kernel, w_transposed, chunk),
        grid=(ni, nj),
        in_specs=[
            pl.BlockSpec((tm, k), x_map),
            pl.BlockSpec((1, k), lambda i, j: (0, 0)),
            w_spec,
        ],
        out_specs=pl.BlockSpec((tm, tn), lambda i, j: (i, j)),
        out_shape=jax.ShapeDtypeStruct((m, n), BF16),
        scratch_shapes=[pltpu.VMEM((2, tm, k), BF16)],
        compiler_params=_cparams(2),
        name="norm_matmul",
    )(x, g, w)


def _gla_kernel(rev, nchunk, *refs):
    if rev:
        (q_ref, k_ref, v_ref, lr_ref, wg_ref, bg_ref, gate_ref, of_ref, gh_ref,
         o_ref, s_ref) = refs
    else:
        q_ref, k_ref, v_ref, lr_ref, wg_ref, bg_ref, o_ref, s_ref = refs
    C = GLA_CHUNK

    @pl.when(pl.program_id(1) == 0)
    def _():
        s_ref[...] = jnp.zeros_like(s_ref)

    row = lax.broadcasted_iota(jnp.int32, (C, C), 0)
    col = lax.broadcasted_iota(jnp.int32, (C, C), 1)
    tri = (col >= row) if rev else (col <= row)
    tri_bf = jnp.where(tri, 1.0 / GLA_TAU, 0.0).astype(BF16)
    row2 = lax.broadcasted_iota(jnp.int32, (2 * C, 2 * C), 0)
    col2 = lax.broadcasted_iota(jnp.int32, (2 * C, 2 * C), 1)
    same_head = (row2 < C) == (col2 < C)
    amask = same_head & ((col2 > row2) if rev else (col2 <= row2))
    q_lo = lax.broadcasted_iota(jnp.int32, (C, LANES), 1) < GLA_DK

    npair = GLA_HEADS // 2

    def gate_logits(c):
        rows = pl.ds(c * C, C)
        z = _dot(lr_ref[rows, :], wg_ref[...]) + bg_ref[...]
        lg = jnp.minimum(z, 0.0) - jnp.log(1.0 + jnp.exp(-jnp.abs(z)))
        hi = lg.astype(BF16)
        return hi, (lg - hi.astype(F32)).astype(BF16)

    def decayed_qk(c, hi, lo):
        rows = pl.ds(c * C, C)
        b = _dot(tri_bf, hi) + _dot(tri_bf, lo)
        bl = b[0:1, :] if rev else b[C - 1:C, :]
        q = q_ref[rows, :].astype(F32)
        k = k_ref[rows, :].astype(F32)
        q_in = (q * jnp.exp(b) * (GLA_DK ** -0.5)).astype(BF16)
        k_out = (k * jnp.exp(-b)).astype(BF16)
        k_last = (k * jnp.exp(bl - b)).astype(BF16)
        return q_in, k_out, k_last, jnp.exp(bl)

    def intra(c, prep):
        rows = pl.ds(c * C, C)
        q_in, k_out, k_last, _ = prep
        out = []
        for p in range(npair):
            sl = slice(p * LANES, (p + 1) * LANES)
            ql, ko, kl = q_in[:, sl], k_out[:, sl], k_last[:, sl]
            zero = jnp.zeros_like(ql)
            qs = jnp.concatenate([jnp.where(q_lo, ql, zero), jnp.where(q_lo, zero, ql)], axis=0)
            ks = jnp.concatenate([jnp.where(q_lo, kl, zero), jnp.where(q_lo, zero, kl)], axis=0)
            vs = jnp.concatenate([v_ref[rows, (2 * p + i) * GLA_DV:(2 * p + i + 1) * GLA_DV]
                                  for i in range(2)], axis=0)
            a_raw = _dot_nt(qs, jnp.concatenate([ko, ko], axis=0))
            out.append((qs, vs, a_raw, _dot_tn(ks, vs)))
        return out

    def outputs(c, prep, pairs):
        rows = pl.ds(c * C, C)
        dec = prep[3]
        for p, (qs, vs, a_raw, kv) in enumerate(pairs):
            a = jnp.where(amask, a_raw, 0.0).astype(BF16)
            st = s_ref[p]
            o2 = _dot(jnp.concatenate([a, qs], axis=1),
                      jnp.concatenate([vs, st.astype(BF16)], axis=0))
            dec_col = jnp.broadcast_to(dec[:, p * LANES:(p + 1) * LANES], (LANES, LANES)).T
            s_ref[p] = dec_col * st + kv
            for i in range(2):
                hs = slice((2 * p + i) * GLA_DV, (2 * p + i + 1) * GLA_DV)
                o = o2[i * C:(i + 1) * C]
                if rev:
                    tot = o + of_ref[rows, hs]
                    ms = jnp.mean(tot * tot, axis=-1, keepdims=True)
                    y = tot * lax.rsqrt(ms + EPS) * gh_ref[...]
                    o_ref[rows, hs] = (y * _silu(gate_ref[rows, hs].astype(F32))).astype(o_ref.dtype)
                else:
                    o_ref[rows, hs] = o

    order = [(nchunk - 1 - i) if rev else i for i in range(nchunk)]
    prep = decayed_qk(order[0], *gate_logits(order[0]))
    for i, c in enumerate(order):
        nxt = order[i + 1] if i + 1 < nchunk else None
        if nxt is not None:
            hi_lo = gate_logits(nxt)
        pairs = intra(c, prep)
        if nxt is not None:
            prep_next = decayed_qk(nxt, *hi_lo)
        outputs(c, prep, pairs)
        if nxt is not None:
            prep = prep_next


def _gla_call(p_act, wg, bg, rev, bsz, t, tb, extra=None):
    nblk = t // tb
    nchunk = tb // GLA_CHUNK

    def rowblk(b, i):
        return b * nblk + ((nblk - 1 - i) if rev else i)

    def act(width, colblk):
        return pl.BlockSpec((tb, width), lambda b, i: (rowblk(b, i), colblk))

    def full(shape):
        return pl.BlockSpec(shape, lambda b, i: (0,) * len(shape))

    in_specs = [
        act(GLA_HK, COL_G_Q // GLA_HK),
        act(GLA_HK, COL_G_K // GLA_HK),
        act(GLA_W, COL_G_V // GLA_W),
        act(LANES, COL_G_LR // LANES),
        full(wg.shape),
        full(bg.shape),
    ]
    args = [p_act, p_act, p_act, p_act, wg, bg]
    if rev:
        o_f, g_head = extra
        in_specs += [act(GLA_W, COL_G_GATE // GLA_W), act(GLA_W, 0), full(g_head.shape)]
        args += [p_act, o_f, g_head]
        out_dtype = BF16
    else:
        out_dtype = F32
    return pl.pallas_call(
        functools.partial(_gla_kernel, rev, nchunk),
        grid=(bsz, nblk),
        in_specs=in_specs,
        out_specs=act(GLA_W, 0),
        out_shape=jax.ShapeDtypeStruct((bsz * t, GLA_W), out_dtype),
        scratch_shapes=[pltpu.VMEM((GLA_HEADS // 2, GLA_DV, LANES), F32)],
        compiler_params=_cparams(2),
        name="gla_bwd" if rev else "gla_fwd",
    )(*args)


def _rope(r, cos_t, sin_t):
    quarter = MLA_ROPE // 2
    swapped = pltpu.roll(r, quarter, 1) + pltpu.roll(r, LANES - quarter, 1)
    return r * cos_t + swapped * sin_t


def _mla_prep_kernel(cq_ref, ckv_ref, kr_ref, gq_ref, gkv_ref, wq_ref, wk_ref, wv_ref,
                     cos_ref, sin_ref, q_ref, k_ref, v_ref):
    scale = (MLA_NOPE + MLA_ROPE) ** -0.5 * LOG2E
    cos_t = cos_ref[...]
    sin_t = sin_ref[...]

    cq = cq_ref[...].astype(F32)
    ms = jnp.mean(cq * cq, axis=-1, keepdims=True)
    qn = (cq * lax.rsqrt(ms + EPS) * gq_ref[...]).astype(BF16)
    ckv = ckv_ref[...].astype(F32)
    ms = jnp.mean(ckv * ckv, axis=-1, keepdims=True)
    kvn = (ckv * lax.rsqrt(ms + EPS) * gkv_ref[...]).astype(BF16)

    k_rope = _rope(kr_ref[...].astype(F32), cos_t, sin_t).astype(BF16)
    k_nope = _dot(kvn, wk_ref[...]).astype(BF16)
    v = _dot(kvn, wv_ref[...]).astype(BF16)
    ones_col = jnp.where(
        lax.broadcasted_iota(jnp.int32, (v.shape[0], MLA_V_PAD - MLA_V), 1) == 0,
        1.0, 0.0).astype(BF16)
    for h in range(MLA_HEADS):
        c0 = h * MLA_QK_PAD
        qh = _dot(qn, wq_ref[:, c0:c0 + MLA_QK_PAD])
        q_ref[:, c0:c0 + MLA_NOPE] = (qh[:, :MLA_NOPE] * scale).astype(BF16)
        q_ref[:, c0 + MLA_NOPE:c0 + MLA_QK_PAD] = (
            _rope(qh[:, MLA_NOPE:], cos_t, sin_t) * scale).astype(BF16)
        k_ref[:, c0:c0 + MLA_NOPE] = k_nope[:, h * MLA_NOPE:(h + 1) * MLA_NOPE]
        k_ref[:, c0 + MLA_NOPE:c0 + MLA_QK_PAD] = k_rope
        v0 = h * MLA_V_PAD
        v_ref[:, v0:v0 + MLA_V] = v[:, h * MLA_V:(h + 1) * MLA_V]
        v_ref[:, v0 + MLA_V:v0 + MLA_V_PAD] = ones_col


def _mla_prep(p_act, g_q, g_kv, wq, wk, wv, cos_t, sin_t, t, tm):
    m = p_act.shape[0]
    nt = t // tm

    def act(width, colblk):
        return pl.BlockSpec((tm, width), lambda i: (i, colblk))

    def full(shape):
        return pl.BlockSpec(shape, lambda i: (0,) * len(shape))

    table = pl.BlockSpec((tm, LANES), lambda i: (i % nt, 0))
    qk_w = MLA_HEADS * MLA_QK_PAD
    return pl.pallas_call(
        _mla_prep_kernel,
        grid=(m // tm,),
        in_specs=[
            act(MLA_Q_LORA, COL_M_CQ // MLA_Q_LORA),
            act(MLA_KV_LORA, COL_M_CKV // MLA_KV_LORA),
            act(LANES, COL_M_KR // LANES),
            full(g_q.shape), full(g_kv.shape), full(wq.shape), full(wk.shape), full(wv.shape),
            table, table,
        ],
        out_specs=[act(qk_w, 0), act(qk_w, 0), act(MLA_HEADS * MLA_V_PAD, 0)],
        out_shape=[
            jax.ShapeDtypeStruct((m, qk_w), BF16),
            jax.ShapeDtypeStruct((m, qk_w), BF16),
            jax.ShapeDtypeStruct((m, MLA_HEADS * MLA_V_PAD), BF16),
        ],
        compiler_params=_cparams(1),
        name="mla_prep",
    )(p_act, p_act, p_act, g_q, g_kv, wq, wk, wv, cos_t, sin_t)


ATTN_SPLIT = 2


def _mla_attn_kernel(nkv, bk, q_ref, k_ref, v_ref, gate_ref, o_ref, s_ref):
    tq = q_ref.shape[0] // ATTN_SPLIT
    q = [q_ref[h * tq:(h + 1) * tq, :] for h in range(ATTN_SPLIT)]

    def scores(h, j):
        s_ref[2 * h + j % 2] = _dot_nt(q[h], k_ref[j * bk:(j + 1) * bk, :])

    m = [jnp.full((tq, 1), -jnp.inf, F32) for _ in range(ATTN_SPLIT)]
    acc = [jnp.zeros((tq, MLA_V_PAD), F32) for _ in range(ATTN_SPLIT)]
    for h in range(ATTN_SPLIT):
        scores(h, 0)
    for j in range(nkv):
        for h in range(ATTN_SPLIT):
            if j + 1 < nkv:
                scores(h, j + 1)
            s = s_ref[2 * h + j % 2]
            m_new = jnp.maximum(m[h], jnp.max(s, axis=-1, keepdims=True))
            alpha = jnp.exp2(m[h] - m_new)
            p = jnp.exp2(s - m_new).astype(BF16)
            acc[h] = alpha * acc[h] + _dot(p, v_ref[j * bk:(j + 1) * bk, :])
            m[h] = m_new
    for h in range(ATTN_SPLIT):
        rows = slice(h * tq, (h + 1) * tq)
        o = acc[h][:, :MLA_V] / acc[h][:, MLA_V:MLA_V + 1]
        o_ref[rows, :] = (o * _silu(gate_ref[rows, :].astype(F32))).astype(o_ref.dtype)


def _mla_attn(q, k, v, p_act, bsz, t, tq, bk):
    nq = t // tq
    nkv = t // bk
    gate_col0 = COL_M_GATE // MLA_V
    return pl.pallas_call(
        functools.partial(_mla_attn_kernel, nkv, bk),
        grid=(bsz, MLA_HEADS, nq),
        in_specs=[
            pl.BlockSpec((tq, MLA_QK_PAD), lambda b, h, i: (b * nq + i, h)),
            pl.BlockSpec((t, MLA_QK_PAD), lambda b, h, i: (b, h)),
            pl.BlockSpec((t, MLA_V_PAD), lambda b, h, i: (b, h)),
            pl.BlockSpec((tq, MLA_V), lambda b, h, i: (b * nq + i, gate_col0 + h)),
        ],
        out_specs=pl.BlockSpec((tq, MLA_V), lambda b, h, i: (b * nq + i, h)),
        out_shape=jax.ShapeDtypeStruct((bsz * t, MLA_W), BF16),
        scratch_shapes=[pltpu.VMEM((2 * ATTN_SPLIT, tq // ATTN_SPLIT, bk), F32)],
        compiler_params=_cparams(3),
        name="mla_attn",
    )(q, k, v, p_act)


def _mem_attn_kernel(q_ref, gate_ref, kv_ref, o_ref):
    heads = [slice(h * MEM_DH, (h + 1) * MEM_DH) for h in range(MEM_HEADS)]
    scores = [_dot_nt(q_ref[:, hs], kv_ref[:, hs]) for hs in heads]
    for h, hs in enumerate(heads):
        s = scores[h] * (MEM_DH ** -0.5)
        p = jnp.exp(s - jnp.max(s, axis=-1, keepdims=True))
        l = jnp.sum(p, axis=-1, keepdims=True)
        o = _dot(p.astype(BF16), kv_ref[:, MEM_W + h * MEM_DH:MEM_W + (h + 1) * MEM_DH]) / l
        o_ref[:, hs] = (o * _silu(gate_ref[:, hs].astype(F32))).astype(o_ref.dtype)


def _mem_attn(p_act, kv_mem, bsz, t, tq):
    nq = t // tq
    return pl.pallas_call(
        _mem_attn_kernel,
        grid=(bsz, nq),
        in_specs=[
            pl.BlockSpec((tq, MEM_W), lambda b, i: (b * nq + i, COL_C_Q // MEM_W)),
            pl.BlockSpec((tq, MEM_W), lambda b, i: (b * nq + i, COL_C_GATE // MEM_W)),
            pl.BlockSpec((N_MEM, 2 * MEM_W), lambda b, i: (b, 0)),
        ],
        out_specs=pl.BlockSpec((tq, MEM_W), lambda b, i: (b * nq + i, 0)),
        out_shape=jax.ShapeDtypeStruct((bsz * t, MEM_W), BF16),
        compiler_params=_cparams(2),
        name="mem_attn",
    )(p_act, p_act, kv_mem)


OUT_TN = 1024


def _out_kernel(oa_ref, ob_ref, oc_ref, w_ref, x_ref, g_ref, y_ref):
    ssq = jnp.zeros((y_ref.shape[0], 1), F32)
    for c0 in range(0, D_MODEL, OUT_TN):
        cs = slice(c0, c0 + OUT_TN)
        acc = (_dot(oa_ref[...], w_ref[0:GLA_W, cs])
               + _dot(ob_ref[...], w_ref[GLA_W:GLA_W + MLA_W, cs])
               + _dot(oc_ref[...], w_ref[GLA_W + MLA_W:, cs]) + x_ref[:, cs])
        ssq = ssq + jnp.sum(acc * acc, axis=-1, keepdims=True)
        y_ref[:, cs] = acc
    r = lax.rsqrt(ssq * (1.0 / D_MODEL) + EPS)
    for c0 in range(0, D_MODEL, OUT_TN):
        cs = slice(c0, c0 + OUT_TN)
        y_ref[:, cs] = y_ref[:, cs] * r * g_ref[:, cs]


def _out_proj(o_a, o_b, o_c, w_out, x, g_final, tm):
    m = x.shape[0]
    return pl.pallas_call(
        _out_kernel,
        grid=(m // tm,),
        in_specs=[
            pl.BlockSpec((tm, GLA_W), lambda i: (i, 0)),
            pl.BlockSpec((tm, MLA_W), lambda i: (i, 0)),
            pl.BlockSpec((tm, MEM_W), lambda i: (i, 0)),
            pl.BlockSpec((D_MODEL, D_MODEL), lambda i: (0, 0), pipeline_mode=pl.Buffered(1)),
            pl.BlockSpec((tm, D_MODEL), lambda i: (i, 0)),
            pl.BlockSpec((1, D_MODEL), lambda i: (0, 0)),
        ],
        out_specs=pl.BlockSpec((tm, D_MODEL), lambda i: (i, 0)),
        out_shape=jax.ShapeDtypeStruct((m, D_MODEL), F32),
        compiler_params=_cparams(1),
        name="out_proj",
    )(o_a, o_b, o_c, w_out, x, g_final)


IN_WIDTHS = (GLA_HK, GLA_HK, GLA_W, 2 * GLA_LR, GLA_W, MLA_Q_LORA, MLA_KV_LORA, MLA_ROPE,
             MLA_W, MEM_W, MEM_W)
IN_DST = (COL_G_Q, COL_G_K, COL_G_V, COL_G_LR, COL_G_GATE, COL_M_CQ, COL_M_CKV, COL_M_KR,
          COL_M_GATE, COL_C_Q, COL_C_GATE)
N_IN = sum(IN_WIDTHS)
W_IN_COLS = 256


def _w_in_relayout_kernel(w_ref, o_ref):
    o_ref[COL_G_LR:, :] = jnp.zeros((P_COLS - COL_G_LR, o_ref.shape[1]), o_ref.dtype)
    src = 0
    for width, dst in zip(IN_WIDTHS, IN_DST):
        o_ref[dst:dst + width, :] = w_ref[src:src + width, :].astype(o_ref.dtype)
        src += width


def _prep_w_in(w_in):
    k = w_in.shape[0]
    return pl.pallas_call(
        _w_in_relayout_kernel,
        grid=(k // W_IN_COLS,),
        in_specs=[pl.BlockSpec((N_IN, W_IN_COLS), lambda i: (0, i))],
        out_specs=pl.BlockSpec((P_COLS, W_IN_COLS), lambda i: (0, i)),
        out_shape=jax.ShapeDtypeStruct((P_COLS, k), BF16),
        compiler_params=_cparams(1),
        name="w_in_relayout",
    )(w_in.T)


def _prep_gla_gate(w_g2, b_g):
    wgs = []
    for d in range(2):
        w = jnp.zeros((LANES, GLA_HK), F32).at[d * GLA_LR:(d + 1) * GLA_LR].set(w_g2[d])
        wgs.append(w.astype(BF16))
    return wgs, [b_g[0][None, :], b_g[1][None, :]]


def _prep_mla(w_uq, w_ukv):
    wq = w_uq.reshape(MLA_Q_LORA, MLA_HEADS, MLA_NOPE + MLA_ROPE)
    wq = jnp.pad(wq, ((0, 0), (0, 0), (0, MLA_QK_PAD - MLA_NOPE - MLA_ROPE)))
    wq = wq.reshape(MLA_Q_LORA, MLA_HEADS * MLA_QK_PAD).astype(BF16)
    wkv = w_ukv.reshape(MLA_KV_LORA, MLA_HEADS, MLA_NOPE + MLA_V)
    wk = wkv[:, :, :MLA_NOPE].reshape(MLA_KV_LORA, MLA_HEADS * MLA_NOPE).astype(BF16)
    wv = wkv[:, :, MLA_NOPE:].reshape(MLA_KV_LORA, MLA_W).astype(BF16)
    return wq, wk, wv


def _rope_tables(t):
    pos = jnp.arange(t, dtype=F32)
    inv = 1.0 / (ROPE_THETA ** (jnp.arange(0, MLA_ROPE, 2, dtype=F32) / MLA_ROPE))
    ang = pos[:, None] * inv[None, :]
    cos, sin = jnp.cos(ang), jnp.sin(ang)
    z = jnp.zeros((t, LANES - MLA_ROPE), F32)
    return (jnp.concatenate([cos, cos, z], axis=-1), jnp.concatenate([-sin, sin, z], axis=-1))


def _pick(t, pref):
    return pref if t % pref == 0 else t


def _trunk(x, mem, w):
    bsz, t, _ = x.shape
    m = bsz * t
    xf = x.reshape(m, D_MODEL)
    p_act = _norm_matmul(xf, w["g_in"], w["w_in_t"], _pick(m, 512), P_TN, w_transposed=True)

    tb = _pick(t, 1024)
    o_f = _gla_call(p_act, w["wg"][0], w["bg"][0], False, bsz, t, tb)
    o_a = _gla_call(p_act, w["wg"][1], w["bg"][1], True, bsz, t, tb, (o_f, w["g_head"]))

    cos_t, sin_t = _rope_tables(t)
    q, k, v = _mla_prep(p_act, w["g_q"], w["g_kv"], w["wq"], w["wk"], w["wv"], cos_t, sin_t,
                        t, _pick(t, 512))
    o_b = _mla_attn(q, k, v, p_act, bsz, t, _pick(t, 512 * ATTN_SPLIT), _pick(t, 1024))

    kv_mem = _norm_matmul(mem.reshape(bsz * N_MEM, D_MODEL), w["mem_g"], w["mem_w_kv"],
                          bsz * N_MEM, 512)
    o_c = _mem_attn(p_act, kv_mem, bsz, t, _pick(t, 512))

    y = _out_proj(o_a, o_b, o_c, w["w_out"], xf, w["g_final"], _pick(m, 256))
    return y.reshape(bsz, t, D_MODEL)


def kernel(x_prompt, x_sample, mem_prompt, mem_sample, g_in, w_in, gla_w_g2, gla_b_g, gla_g_head,
           mla_g_q, mla_w_uq, mla_g_kv, mla_w_ukv, mem_g, mem_w_kv, w_out, g_final):
    wg, bg = _prep_gla_gate(gla_w_g2[0], gla_b_g[0])
    wq, wk, wv = _prep_mla(mla_w_uq[0], mla_w_ukv[0])
    w = {
        "g_in": g_in[0][None, :],
        "w_in_t": _prep_w_in(w_in[0]),
        "wg": wg, "bg": bg,
        "g_head": gla_g_head[0][None, :],
        "g_q": mla_g_q[0][None, :], "g_kv": mla_g_kv[0][None, :],
        "wq": wq, "wk": wk, "wv": wv,
        "mem_g": mem_g[0][None, :],
        "mem_w_kv": mem_w_kv[0].astype(BF16),
        "w_out": w_out[0].astype(BF16),
        "g_final": g_final[None, :],
    }
    return (_trunk(x_prompt, mem_prompt, w), _trunk(x_sample, mem_sample, w))
```

```python
import functools
import math

import jax
import jax.numpy as jnp
from jax import lax
from jax.experimental import pallas as pl
from jax.experimental.pallas import tpu as pltpu

F32 = jnp.float32
BF16 = jnp.bfloat16

D_MODEL = 4096
N_MEM = 256
GLA_HEADS = 12
GLA_DK = 64
GLA_DV = 128
GLA_LR = 16
GLA_TAU = 16.0
assert math.log2(GLA_TAU).is_integer()
GLA_CHUNK = 64
GLA_HK = GLA_HEADS * GLA_DK
GLA_W = GLA_HEADS * GLA_DV
MLA_HEADS = 12
MLA_Q_LORA = 1536
MLA_KV_LORA = 512
MLA_NOPE = 128
MLA_ROPE = 64
MLA_V = 128
ROPE_THETA = 10000.0
MLA_W = MLA_HEADS * MLA_V
MLA_QK_PAD = 256
MLA_V_PAD = 256
MEM_HEADS = 4
MEM_DH = 256
MEM_W = MEM_HEADS * MEM_DH
EPS = 1e-6
LANES = 128
LOG2E = 1.4426950408889634

COL_G_V = 0
COL_G_GATE = 1536
COL_M_CQ = 3072
COL_M_GATE = 4608
COL_G_Q = 6144
COL_G_K = 6912
COL_M_CKV = 7680
COL_C_Q = 8192
COL_C_GATE = 9216
COL_G_LR = 10240
COL_M_KR = 10368
P_USED = 10496
P_TN = 1536
P_COLS = 10752

VMEM_LIMIT = 60 * 1024 * 1024


def _cparams(n_axes):
    return pltpu.CompilerParams(
        dimension_semantics=("arbitrary",) * n_axes, vmem_limit_bytes=VMEM_LIMIT)


def _dot(a, b):
    return jnp.dot(a, b, preferred_element_type=F32)


def _dot_nt(a, b):
    return lax.dot_general(a, b, (((1,), (1,)), ((), ())), preferred_element_type=F32)


def _dot_tn(a, b):
    return lax.dot_general(a, b, (((0,), (0,)), ((), ())), preferred_element_type=F32)


def _silu(x):
    return x * jax.nn.sigmoid(x)


NORM_ROWS = 64
BF16_SUBLANES = 16


def _norm_matmul_kernel(w_transposed, chunk, x_ref, g_ref, w_ref, o_ref, h_ref):
    i = pl.program_id(0)
    j = pl.program_id(1)
    tm = x_ref.shape[0]
    dot = _dot_nt if w_transposed else _dot

    def normalise(slot, r0, rows):
        x = x_ref[pl.ds(r0, rows), :]
        ms = jnp.mean(x * x, axis=-1, keepdims=True)
        h_ref[slot, pl.ds(r0, rows), :] = (x * lax.rsqrt(ms + EPS) * g_ref[...]).astype(BF16)

    @pl.when(j == 0)
    def _():
        @pl.when(i == 0)
        def _():
            def rows(c, carry):
                normalise(0, pl.multiple_of(c * NORM_ROWS, NORM_ROWS), NORM_ROWS)
                return carry
            lax.fori_loop(0, tm // NORM_ROWS, rows, 0)

        o_ref[...] = dot(h_ref[i % 2], w_ref[...]).astype(o_ref.dtype)

    @pl.when(j > 0)
    def _():
        r0 = pl.multiple_of(jnp.minimum((j - 1) * chunk, tm - chunk), BF16_SUBLANES)
        o_ref[...] = dot(h_ref[i % 2], w_ref[...]).astype(o_ref.dtype)
        normalise((i + 1) % 2, r0, chunk)


def _norm_matmul(x, g, w, tm, tn, w_transposed=False):
    m, k = x.shape
    n = w.shape[0] if w_transposed else w.shape[1]
    ni, nj = m // tm, n // tn
    chunk = -(-tm // (nj - 1))
    chunk = -(-chunk // BF16_SUBLANES) * BF16_SUBLANES
    assert chunk * (nj - 1) >= tm and (tm - chunk) % BF16_SUBLANES == 0

    def x_map(i, j):
        return (jnp.where((i == 0) & (j == 0), 0, jnp.minimum(i + 1, ni - 1)), 0)

    if w_transposed:
        w_spec = pl.BlockSpec((tn, k), lambda i, j: (j, 0))
    else:
        w_spec = pl.BlockSpec((k, tn), lambda i, j: (0, j))
    return pl.pallas_call(
        functools.partial(_norm_matmul_kernel, w_transposed, chunk),
        grid=(ni, nj),
        in_specs=[
            pl.BlockSpec((tm, k), x_map),
            pl.BlockSpec((1, k), lambda i, j: (0, 0)),
            w_spec,
        ],
        out_specs=pl.BlockSpec((tm, tn), lambda i, j: (i, j)),
        out_shape=jax.ShapeDtypeStruct((m, n), BF16),
        scratch_shapes=[pltpu.VMEM((2, tm, k), BF16)],
        compiler_params=_cparams(2),
        name="norm_matmul",
    )(x, g, w)


def _gla_kernel(rev, nchunk, *refs):
    if rev:
        (q_ref, k_ref, v_ref, lr_ref, wg_ref, bg_ref, gate_ref, of_ref, gh_ref,
         o_ref, s_ref) = refs
    else:
        q_ref, k_ref, v_ref, lr_ref, wg_ref, bg_ref, o_ref, s_ref = refs
    C = GLA_CHUNK

    @pl.when(pl.program_id(1) == 0)
    def _():
        s_ref[...] = jnp.zeros_like(s_ref)

    row = lax.broadcasted_iota(jnp.int32, (C, C), 0)
    col = lax.broadcasted_iota(jnp.int32, (C, C), 1)
    tri = (col >= row) if rev else (col <= row)
    tri_bf = jnp.where(tri, 1.0 / GLA_TAU, 0.0).astype(BF16)
    row2 = lax.broadcasted_iota(jnp.int32, (2 * C, 2 * C), 0)
    col2 = lax.broadcasted_iota(jnp.int32, (2 * C, 2 * C), 1)
    same_head = (row2 < C) == (col2 < C)
    amask = same_head & ((col2 > row2) if rev else (col2 <= row2))
    q_lo = lax.broadcasted_iota(jnp.int32, (C, LANES), 1) < GLA_DK

    npair = GLA_HEADS // 2

    def gate_logits(c):
        rows = pl.ds(c * C, C)
        z = _dot(lr_ref[rows, :], wg_ref[...]) + bg_ref[...]
        lg = jnp.minimum(z, 0.0) - jnp.log(1.0 + jnp.exp(-jnp.abs(z)))
        hi = lg.astype(BF16)
        return hi, (lg - hi.astype(F32)).astype(BF16)

    def decayed_qk(c, hi, lo):
        rows = pl.ds(c * C, C)
        b = _dot(tri_bf, hi) + _dot(tri_bf, lo)
        bl = b[0:1, :] if rev else b[C - 1:C, :]
        q = q_ref[rows, :].astype(F32)
        k = k_ref[rows, :].astype(F32)
        q_in = (q * jnp.exp(b) * (GLA_DK ** -0.5)).astype(BF16)
        k_out = (k * jnp.exp(-b)).astype(BF16)
        k_last = (k * jnp.exp(bl - b)).astype(BF16)
        return q_in, k_out, k_last, jnp.exp(bl)

    def intra(c, prep):
        rows = pl.ds(c * C, C)
        q_in, k_out, k_last, _ = prep
        out = []
        for p in range(npair):
            sl = slice(p * LANES, (p + 1) * LANES)
            ql, ko, kl = q_in[:, sl], k_out[:, sl], k_last[:, sl]
            zero = jnp.zeros_like(ql)
            qs = jnp.concatenate([jnp.where(q_lo, ql, zero), jnp.where(q_lo, zero, ql)], axis=0)
            ks = jnp.concatenate([jnp.where(q_lo, kl, zero), jnp.where(q_lo, zero, kl)], axis=0)
            vs = jnp.concatenate([v_ref[rows, (2 * p + i) * GLA_DV:(2 * p + i + 1) * GLA_DV]
                                  for i in range(2)], axis=0)
            a_raw = _dot_nt(qs, jnp.concatenate([ko, ko], axis=0))
            out.append((qs, vs, a_raw, _dot_tn(ks, vs)))
        return out

    def outputs(c, prep, pairs):
        rows = pl.ds(c * C, C)
        dec = prep[3]
        for p, (qs, vs, a_raw, kv) in enumerate(pairs):
            a = jnp.where(amask, a_raw, 0.0).astype(BF16)
            st = s_ref[p]
            o2 = _dot(jnp.concatenate([a, qs], axis=1),
                      jnp.concatenate([vs, st.astype(BF16)], axis=0))
            dec_col = jnp.broadcast_to(dec[:, p * LANES:(p + 1) * LANES], (LANES, LANES)).T
            s_ref[p] = dec_col * st + kv
            for i in range(2):
                hs = slice((2 * p + i) * GLA_DV, (2 * p + i + 1) * GLA_DV)
                o = o2[i * C:(i + 1) * C]
                if rev:
                    tot = o + of_ref[rows, hs]
                    ms = jnp.mean(tot * tot, axis=-1, keepdims=True)
                    y = tot * lax.rsqrt(ms + EPS) * gh_ref[...]
                    o_ref[rows, hs] = (y * _silu(gate_ref[rows, hs].astype(F32))).astype(o_ref.dtype)
                else:
                    o_ref[rows, hs] = o

    order = [(nchunk - 1 - i) if rev else i for i in range(nchunk)]
    prep = decayed_qk(order[0], *gate_logits(order[0]))
    for i, c in enumerate(order):
        nxt = order[i + 1] if i + 1 < nchunk else None
        if nxt is not None:
            hi_lo = gate_logits(nxt)
        pairs = intra(c, prep)
        if nxt is not None:
            prep_next = decayed_qk(nxt, *hi_lo)
        outputs(c, prep, pairs)
        if nxt is not None:
            prep = prep_next


def _gla_call(p_act, wg, bg, rev, bsz, t, tb, extra=None):
    nblk = t // tb
    nchunk = tb // GLA_CHUNK

    def rowblk(b, i):
        return b * nblk + ((nblk - 1 - i) if rev else i)

    def act(width, colblk):
        return pl.BlockSpec((tb, width), lambda b, i: (rowblk(b, i), colblk))

    def full(shape):
        return pl.BlockSpec(shape, lambda b, i: (0,) * len(shape))

    in_specs = [
        act(GLA_HK, COL_G_Q // GLA_HK),
        act(GLA_HK, COL_G_K // GLA_HK),
        act(GLA_W, COL_G_V // GLA_W),
        act(LANES, COL_G_LR // LANES),
        full(wg.shape),
        full(bg.shape),
    ]
    args = [p_act, p_act, p_act, p_act, wg, bg]
    if rev:
        o_f, g_head = extra
        in_specs += [act(GLA_W, COL_G_GATE // GLA_W), act(GLA_W, 0), full(g_head.shape)]
        args += [p_act, o_f, g_head]
        out_dtype = BF16
    else:
        out_dtype = F32
    return pl.pallas_call(
        functools.partial(_gla_kernel, rev, nchunk),
        grid=(bsz, nblk),
        in_specs=in_specs,
        out_specs=act(GLA_W, 0),
        out_shape=jax.ShapeDtypeStruct((bsz * t, GLA_W), out_dtype),
        scratch_shapes=[pltpu.VMEM((GLA_HEADS // 2, GLA_DV, LANES), F32)],
        compiler_params=_cparams(2),
        name="gla_bwd" if rev else "gla_fwd",
    )(*args)


def _rope(r, cos_t, sin_t):
    half = MLA_ROPE // 2
    lane = lax.broadcasted_iota(jnp.int32, r.shape, 1)
    swapped = jnp.where(lane % MLA_ROPE < half,
                        pltpu.roll(r, LANES - half, 1), pltpu.roll(r, half, 1))
    return r * cos_t + swapped * sin_t


def _mla_prep_kernel(cq_ref, ckv_ref, kr_ref, gq_ref, gkv_ref, wqn_ref, wqr_ref, wk_ref, wv_ref,
                     cos_ref, sin_ref, q_ref, k_ref, v_ref):
    scale = (MLA_NOPE + MLA_ROPE) ** -0.5 * LOG2E
    cos_t = cos_ref[...]
    sin_t = sin_ref[...]

    cq = cq_ref[...].astype(F32)
    ms = jnp.mean(cq * cq, axis=-1, keepdims=True)
    qn = (cq * lax.rsqrt(ms + EPS) * gq_ref[...]).astype(BF16)
    ckv = ckv_ref[...].astype(F32)
    ms = jnp.mean(ckv * ckv, axis=-1, keepdims=True)
    kvn = (ckv * lax.rsqrt(ms + EPS) * gkv_ref[...]).astype(BF16)

    tm = qn.shape[0]
    low = lax.broadcasted_iota(jnp.int32, (tm, LANES), 1) < MLA_ROPE
    k_rope_even = _rope(kr_ref[...].astype(F32), cos_t, sin_t)
    k_rope = (k_rope_even.astype(BF16), pltpu.roll(k_rope_even, MLA_ROPE, 1).astype(BF16))
    k_nope = _dot(kvn, wk_ref[...]).astype(BF16)
    v = _dot(kvn, wv_ref[...]).astype(BF16)
    ones_col = jnp.where(
        lax.broadcasted_iota(jnp.int32, (tm, MLA_V_PAD - MLA_V), 1) == 0, 1.0, 0.0).astype(BF16)
    for h in range(MLA_HEADS):
        c0 = h * MLA_QK_PAD
        k_ref[:, c0:c0 + MLA_NOPE] = k_nope[:, h * MLA_NOPE:(h + 1) * MLA_NOPE]
        k_ref[:, c0 + MLA_NOPE:c0 + MLA_QK_PAD] = k_rope[h % 2]
        v0 = h * MLA_V_PAD
        v_ref[:, v0:v0 + MLA_V] = v[:, h * MLA_V:(h + 1) * MLA_V]
        v_ref[:, v0 + MLA_V:v0 + MLA_V_PAD] = ones_col
    for p in range(MLA_HEADS // 2):
        q2 = _dot(qn, wqn_ref[:, p * 2 * MLA_NOPE:(p + 1) * 2 * MLA_NOPE])
        for i in range(2):
            c0 = (2 * p + i) * MLA_QK_PAD
            q_ref[:, c0:c0 + MLA_NOPE] = (
                q2[:, i * MLA_NOPE:(i + 1) * MLA_NOPE] * scale).astype(BF16)
    for g in range(MLA_HEADS // 4):
        q4 = _dot(qn, wqr_ref[:, g * 4 * MLA_ROPE:(g + 1) * 4 * MLA_ROPE])
        for s in range(2):
            tile = _rope(q4[:, s * LANES:(s + 1) * LANES], cos_t, sin_t) * scale
            zero = jnp.zeros_like(tile)
            c0 = (4 * g + 2 * s) * MLA_QK_PAD + MLA_NOPE
            q_ref[:, c0:c0 + LANES] = jnp.where(low, tile, zero).astype(BF16)
            c1 = c0 + MLA_QK_PAD
            q_ref[:, c1:c1 + LANES] = jnp.where(low, zero, tile).astype(BF16)


def _mla_prep(p_act, g_q, g_kv, wqn, wqr, wk, wv, cos_t, sin_t, t, tm):
    m = p_act.shape[0]
    nt = t // tm

    def act(width, colblk):
        return pl.BlockSpec((tm, width), lambda i: (i, colblk))

    def full(shape):
        return pl.BlockSpec(shape, lambda i: (0,) * len(shape))

    table = pl.BlockSpec((tm, LANES), lambda i: (i % nt, 0))
    qk_w = MLA_HEADS * MLA_QK_PAD
    return pl.pallas_call(
        _mla_prep_kernel,
        grid=(m // tm,),
        in_specs=[
            act(MLA_Q_LORA, COL_M_CQ // MLA_Q_LORA),
            act(MLA_KV_LORA, COL_M_CKV // MLA_KV_LORA),
            act(LANES, COL_M_KR // LANES),
            full(g_q.shape), full(g_kv.shape), full(wqn.shape), full(wqr.shape),
            full(wk.shape), full(wv.shape),
            table, table,
        ],
        out_specs=[act(qk_w, 0), act(qk_w, 0), act(MLA_HEADS * MLA_V_PAD, 0)],
        out_shape=[
            jax.ShapeDtypeStruct((m, qk_w), BF16),
            jax.ShapeDtypeStruct((m, qk_w), BF16),
            jax.ShapeDtypeStruct((m, MLA_HEADS * MLA_V_PAD), BF16),
        ],
        compiler_params=_cparams(1),
        name="mla_prep",
    )(p_act, p_act, p_act, g_q, g_kv, wqn, wqr, wk, wv, cos_t, sin_t)


ATTN_SPLIT = 2


def _mla_attn_kernel(nkv, bk, q_ref, k_ref, v_ref, gate_ref, o_ref, s_ref):
    tq = q_ref.shape[0] // ATTN_SPLIT
    q = [q_ref[h * tq:(h + 1) * tq, :] for h in range(ATTN_SPLIT)]

    def scores(h, j):
        s_ref[2 * h + j % 2] = _dot_nt(q[h], k_ref[j * bk:(j + 1) * bk, :])

    m = [jnp.full((tq, 1), -jnp.inf, F32) for _ in range(ATTN_SPLIT)]
    acc = [jnp.zeros((tq, MLA_V_PAD), F32) for _ in range(ATTN_SPLIT)]
    for h in range(ATTN_SPLIT):
        scores(h, 0)
    for j in range(nkv):
        for h in range(ATTN_SPLIT):
            if j + 1 < nkv:
                scores(h, j + 1)
            s = s_ref[2 * h + j % 2]
            m_new = jnp.maximum(m[h], jnp.max(s, axis=-1, keepdims=True))
            alpha = jnp.exp2(m[h] - m_new)
            p = jnp.exp2(s - m_new).astype(BF16)
            acc[h] = alpha * acc[h] + _dot(p, v_ref[j * bk:(j + 1) * bk, :])
            m[h] = m_new
    for h in range(ATTN_SPLIT):
        rows = slice(h * tq, (h + 1) * tq)
        o = acc[h][:, :MLA_V] / acc[h][:, MLA_V:MLA_V + 1]
        o_ref[rows, :] = (o * _silu(gate_ref[rows, :].astype(F32))).astype(o_ref.dtype)


def _mla_attn(q, k, v, p_act, bsz, t, tq, bk):
    nq = t // tq
    nkv = t // bk
    gate_col0 = COL_M_GATE // MLA_V
    return pl.pallas_call(
        functools.partial(_mla_attn_kernel, nkv, bk),
        grid=(bsz, MLA_HEADS, nq),
        in_specs=[
            pl.BlockSpec((tq, MLA_QK_PAD), lambda b, h, i: (b * nq + i, h)),
            pl.BlockSpec((t, MLA_QK_PAD), lambda b, h, i: (b, h)),
            pl.BlockSpec((t, MLA_V_PAD), lambda b, h, i: (b, h)),
            pl.BlockSpec((tq, MLA_V), lambda b, h, i: (b * nq + i, gate_col0 + h)),
        ],
        out_specs=pl.BlockSpec((tq, MLA_V), lambda b, h, i: (b * nq + i, h)),
        out_shape=jax.ShapeDtypeStruct((bsz * t, MLA_W), BF16),
        scratch_shapes=[pltpu.VMEM((2 * ATTN_SPLIT, tq // ATTN_SPLIT, bk), F32)],
        compiler_params=_cparams(3),
        name="mla_attn",
    )(q, k, v, p_act)


def _mem_attn_kernel(q_ref, gate_ref, kv_ref, o_ref):
    heads = [slice(h * MEM_DH, (h + 1) * MEM_DH) for h in range(MEM_HEADS)]
    scores = [_dot_nt(q_ref[:, hs], kv_ref[:, hs]) for hs in heads]
    for h, hs in enumerate(heads):
        s = scores[h] * (MEM_DH ** -0.5)
        p = jnp.exp(s - jnp.max(s, axis=-1, keepdims=True))
        l = jnp.sum(p, axis=-1, keepdims=True)
        o = _dot(p.astype(BF16), kv_ref[:, MEM_W + h * MEM_DH:MEM_W + (h + 1) * MEM_DH]) / l
        o_ref[:, hs] = (o * _silu(gate_ref[:, hs].astype(F32))).astype(o_ref.dtype)


def _mem_attn(p_act, kv_mem, bsz, t, tq):
    nq = t // tq
    return pl.pallas_call(
        _mem_attn_kernel,
        grid=(bsz, nq),
        in_specs=[
            pl.BlockSpec((tq, MEM_W), lambda b, i: (b * nq + i, COL_C_Q // MEM_W)),
            pl.BlockSpec((tq, MEM_W), lambda b, i: (b * nq + i, COL_C_GATE // MEM_W)),
            pl.BlockSpec((N_MEM, 2 * MEM_W), lambda b, i: (b, 0)),
        ],
        out_specs=pl.BlockSpec((tq, MEM_W), lambda b, i: (b * nq + i, 0)),
        out_shape=jax.ShapeDtypeStruct((bsz * t, MEM_W), BF16),
        compiler_params=_cparams(2),
        name="mem_attn",
    )(p_act, p_act, kv_mem)


OUT_TN = 1024


def _out_kernel(oa_ref, ob_ref, oc_ref, w_ref, x_ref, g_ref, y_ref):
    ssq = jnp.zeros((y_ref.shape[0], 1), F32)
    for c0 in range(0, D_MODEL, OUT_TN):
        cs = slice(c0, c0 + OUT_TN)
        acc = (_dot(oa_ref[...], w_ref[0:GLA_W, cs])
               + _dot(ob_ref[...], w_ref[GLA_W:GLA_W + MLA_W, cs])
               + _dot(oc_ref[...], w_ref[GLA_W + MLA_W:, cs]) + x_ref[:, cs])
        ssq = ssq + jnp.sum(acc * acc, axis=-1, keepdims=True)
        y_ref[:, cs] = acc
    r = lax.rsqrt(ssq * (1.0 / D_MODEL) + EPS)
    for c0 in range(0, D_MODEL, OUT_TN):
        cs = slice(c0, c0 + OUT_TN)
        y_ref[:, cs] = y_ref[:, cs] * r * g_ref[:, cs]


def _out_proj(o_a, o_b, o_c, w_out, x, g_final, tm):
    m = x.shape[0]
    return pl.pallas_call(
        _out_kernel,
        grid=(m // tm,),
        in_specs=[
            pl.BlockSpec((tm, GLA_W), lambda i: (i, 0)),
            pl.BlockSpec((tm, MLA_W), lambda i: (i, 0)),
            pl.BlockSpec((tm, MEM_W), lambda i: (i, 0)),
            pl.BlockSpec((D_MODEL, D_MODEL), lambda i: (0, 0), pipeline_mode=pl.Buffered(1)),
            pl.BlockSpec((tm, D_MODEL), lambda i: (i, 0)),
            pl.BlockSpec((1, D_MODEL), lambda i: (0, 0)),
        ],
        out_specs=pl.BlockSpec((tm, D_MODEL), lambda i: (i, 0)),
        out_shape=jax.ShapeDtypeStruct((m, D_MODEL), F32),
        compiler_params=_cparams(1),
        name="out_proj",
    )(o_a, o_b, o_c, w_out, x, g_final)


IN_WIDTHS = (GLA_HK, GLA_HK, GLA_W, 2 * GLA_LR, GLA_W, MLA_Q_LORA, MLA_KV_LORA, MLA_ROPE,
             MLA_W, MEM_W, MEM_W)
IN_DST = (COL_G_Q, COL_G_K, COL_G_V, COL_G_LR, COL_G_GATE, COL_M_CQ, COL_M_CKV, COL_M_KR,
          COL_M_GATE, COL_C_Q, COL_C_GATE)
N_IN = sum(IN_WIDTHS)
W_IN_COLS = 256


def _w_in_relayout_kernel(w_ref, o_ref):
    o_ref[COL_G_LR:, :] = jnp.zeros((P_COLS - COL_G_LR, o_ref.shape[1]), o_ref.dtype)
    src = 0
    for width, dst in zip(IN_WIDTHS, IN_DST):
        o_ref[dst:dst + width, :] = w_ref[src:src + width, :].astype(o_ref.dtype)
        src += width


def _prep_w_in(w_in):
    k = w_in.shape[0]
    return pl.pallas_call(
        _w_in_relayout_kernel,
        grid=(k // W_IN_COLS,),
        in_specs=[pl.BlockSpec((N_IN, W_IN_COLS), lambda i: (0, i))],
        out_specs=pl.BlockSpec((P_COLS, W_IN_COLS), lambda i: (0, i)),
        out_shape=jax.ShapeDtypeStruct((P_COLS, k), BF16),
        compiler_params=_cparams(1),
        name="w_in_relayout",
    )(w_in.T)


def _prep_gla_gate(w_g2, b_g):
    wgs = []
    for d in range(2):
        w = jnp.zeros((LANES, GLA_HK), F32).at[d * GLA_LR:(d + 1) * GLA_LR].set(w_g2[d])
        wgs.append(w.astype(BF16))
    return wgs, [b_g[0][None, :], b_g[1][None, :]]


def _prep_mla(w_uq, w_ukv):
    wq = w_uq.reshape(MLA_Q_LORA, MLA_HEADS, MLA_NOPE + MLA_ROPE)
    wqn = wq[:, :, :MLA_NOPE].reshape(MLA_Q_LORA, MLA_HEADS * MLA_NOPE).astype(BF16)
    wqr = wq[:, :, MLA_NOPE:].reshape(MLA_Q_LORA, MLA_HEADS * MLA_ROPE).astype(BF16)
    wkv = w_ukv.reshape(MLA_KV_LORA, MLA_HEADS, MLA_NOPE + MLA_V)
    wk = wkv[:, :, :MLA_NOPE].reshape(MLA_KV_LORA, MLA_HEADS * MLA_NOPE).astype(BF16)
    wv = wkv[:, :, MLA_NOPE:].reshape(MLA_KV_LORA, MLA_W).astype(BF16)
    return wqn, wqr, wk, wv


def _rope_tables(t):
    pos = jnp.arange(t, dtype=F32)
    inv = 1.0 / (ROPE_THETA ** (jnp.arange(0, MLA_ROPE, 2, dtype=F32) / MLA_ROPE))
    ang = pos[:, None] * inv[None, :]
    cos, sin = jnp.cos(ang), jnp.sin(ang)
    return (jnp.concatenate([cos, cos, cos, cos], axis=-1),
            jnp.concatenate([-sin, sin, -sin, sin], axis=-1))


def _pick(t, pref):
    return pref if t % pref == 0 else t


ROW_TILE = 512
GLA_BLOCK = 1024
ATTN_Q_TILE = 512 * ATTN_SPLIT
ATTN_KV_TILE = 1024
MEM_KV_TN = 512
OUT_ROW_TILE = 256


def _trunk(x, mem, w):
    bsz, t, _ = x.shape
    m = bsz * t
    xf = x.reshape(m, D_MODEL)
    p_act = _norm_matmul(xf, w["g_in"], w["w_in_t"], _pick(m, ROW_TILE), P_TN,
                         w_transposed=True)

    tb = _pick(t, GLA_BLOCK)
    o_f = _gla_call(p_act, w["wg"][0], w["bg"][0], False, bsz, t, tb)
    o_a = _gla_call(p_act, w["wg"][1], w["bg"][1], True, bsz, t, tb, (o_f, w["g_head"]))

    cos_t, sin_t = _rope_tables(t)
    q, k, v = _mla_prep(p_act, w["g_q"], w["g_kv"], w["wqn"], w["wqr"], w["wk"], w["wv"],
                        cos_t, sin_t, t, _pick(t, ROW_TILE))
    o_b = _mla_attn(q, k, v, p_act, bsz, t, _pick(t, ATTN_Q_TILE), _pick(t, ATTN_KV_TILE))

    kv_mem = _norm_matmul(mem.reshape(bsz * N_MEM, D_MODEL), w["mem_g"], w["mem_w_kv"],
                          bsz * N_MEM, MEM_KV_TN)
    o_c = _mem_attn(p_act, kv_mem, bsz, t, _pick(t, ROW_TILE))

    y = _out_proj(o_a, o_b, o_c, w["w_out"], xf, w["g_final"], _pick(m, OUT_ROW_TILE))
    return y.reshape(bsz, t, D_MODEL)


def kernel(x_prompt, x_sample, mem_prompt, mem_sample, g_in, w_in, gla_w_g2, gla_b_g, gla_g_head,
           mla_g_q, mla_w_uq, mla_g_kv, mla_w_ukv, mem_g, mem_w_kv, w_out, g_final):
    wg, bg = _prep_gla_gate(gla_w_g2[0], gla_b_g[0])
    wqn, wqr, wk, wv = _prep_mla(mla_w_uq[0], mla_w_ukv[0])
    w = {
        "g_in": g_in[0][None, :],
        "w_in_t": _prep_w_in(w_in[0]),
        "wg": wg, "bg": bg,
        "g_head": gla_g_head[0][None, :],
        "g_q": mla_g_q[0][None, :], "g_kv": mla_g_kv[0][None, :],
        "wqn": wqn, "wqr": wqr, "wk": wk, "wv": wv,
        "mem_g": mem_g[0][None, :],
        "mem_w_kv": mem_w_kv[0].astype(BF16),
        "w_out": w_out[0].astype(BF16),
        "g_final": g_final[None, :],
    }
    return (_trunk(x_prompt, mem_prompt, w), _trunk(x_sample, mem_sample, w))
```

```python
import functools
import math

import jax
import jax.numpy as jnp
from jax import lax
from jax.experimental import pallas as pl
from jax.experimental.pallas import tpu as pltpu

F32 = jnp.float32
BF16 = jnp.bfloat16

D_MODEL = 4096
N_MEM = 256
GLA_HEADS = 12
GLA_DK = 64
GLA_DV = 128
GLA_LR = 16
GLA_TAU = 16.0
assert math.log2(GLA_TAU).is_integer()
GLA_CHUNK = 64
GLA_HK = GLA_HEADS * GLA_DK
GLA_W = GLA_HEADS * GLA_DV
MLA_HEADS = 12
MLA_Q_LORA = 1536
MLA_KV_LORA = 512
MLA_NOPE = 128
MLA_ROPE = 64
MLA_V = 128
ROPE_THETA = 10000.0
MLA_W = MLA_HEADS * MLA_V
MLA_QK_PAD = 256
MLA_V_PAD = 256
MEM_HEADS = 4
MEM_DH = 256
MEM_W = MEM_HEADS * MEM_DH
EPS = 1e-6
LANES = 128
LOG2E = 1.4426950408889634

COL_G_V = 0
COL_G_GATE = 1536
COL_M_CQ = 3072
COL_M_GATE = 4608
COL_G_Q = 6144
COL_G_K = 6912
COL_M_CKV = 7680
COL_C_Q = 8192
COL_C_GATE = 9216
P_COLS = 10240
P_TN = 1280
AUX_G_LR = 0
AUX_M_KR = 128
AUX_COLS = 256

VMEM_LIMIT = 60 * 1024 * 1024


def _cparams(n_axes):
    return pltpu.CompilerParams(
        dimension_semantics=("arbitrary",) * n_axes, vmem_limit_bytes=VMEM_LIMIT)


def _dot(a, b):
    return jnp.dot(a, b, preferred_element_type=F32)


def _dot_nt(a, b):
    return lax.dot_general(a, b, (((1,), (1,)), ((), ())), preferred_element_type=F32)


def _dot_tn(a, b):
    return lax.dot_general(a, b, (((0,), (0,)), ((), ())), preferred_element_type=F32)


def _silu(x):
    return x * jax.nn.sigmoid(x)


NORM_ROWS = 64
BF16_SUBLANES = 16


def _norm_matmul_kernel(w_transposed, chunk, has_aux, *refs):
    if has_aux:
        x_ref, g_ref, w_ref, wa_ref, o_ref, oa_ref, h_ref = refs
    else:
        x_ref, g_ref, w_ref, o_ref, h_ref = refs
    i = pl.program_id(0)
    j = pl.program_id(1)
    tm = x_ref.shape[0]
    dot = _dot_nt if w_transposed else _dot

    def normalise(slot, r0, rows):
        x = x_ref[pl.ds(r0, rows), :]
        ms = jnp.mean(x * x, axis=-1, keepdims=True)
        h_ref[slot, pl.ds(r0, rows), :] = (x * lax.rsqrt(ms + EPS) * g_ref[...]).astype(BF16)

    @pl.when(j == 0)
    def _():
        @pl.when(i == 0)
        def _():
            def rows(c, carry):
                normalise(0, pl.multiple_of(c * NORM_ROWS, NORM_ROWS), NORM_ROWS)
                return carry
            lax.fori_loop(0, tm // NORM_ROWS, rows, 0)

        o_ref[...] = dot(h_ref[i % 2], w_ref[...]).astype(o_ref.dtype)
        if has_aux:
            oa_ref[...] = _dot_nt(h_ref[i % 2], wa_ref[...]).astype(oa_ref.dtype)

    @pl.when(j > 0)
    def _():
        r0 = pl.multiple_of(jnp.minimum((j - 1) * chunk, tm - chunk), BF16_SUBLANES)
        o_ref[...] = dot(h_ref[i % 2], w_ref[...]).astype(o_ref.dtype)
        normalise((i + 1) % 2, r0, chunk)


def _norm_matmul(x, g, w, tm, tn, w_transposed=False, w_aux_t=None):
    m, k = x.shape
    n = w.shape[0] if w_transposed else w.shape[1]
    ni, nj = m // tm, n // tn
    chunk = -(-tm // (nj - 1))
    chunk = -(-chunk // BF16_SUBLANES) * BF16_SUBLANES
    assert chunk * (nj - 1) >= tm and (tm - chunk) % BF16_SUBLANES == 0

    def x_map(i, j):
        return (jnp.where((i == 0) & (j == 0), 0, jnp.minimum(i + 1, ni - 1)), 0)

    if w_transposed:
        w_spec = pl.BlockSpec((tn, k), lambda i, j: (j, 0))
    else:
        w_spec = pl.BlockSpec((k, tn), lambda i, j: (0, j))
    in_specs = [pl.BlockSpec((tm, k), x_map), pl.BlockSpec((1, k), lambda i, j: (0, 0)), w_spec]
    out_specs = pl.BlockSpec((tm, tn), lambda i, j: (i, j))
    out_shape = jax.ShapeDtypeStruct((m, n), BF16)
    args = [x, g, w]
    if w_aux_t is not None:
        n_aux = w_aux_t.shape[0]
        in_specs.append(pl.BlockSpec((n_aux, k), lambda i, j: (0, 0)))
        out_specs = [out_specs, pl.BlockSpec((tm, n_aux), lambda i, j: (i, 0))]
        out_shape = [out_shape, jax.ShapeDtypeStruct((m, n_aux), BF16)]
        args.append(w_aux_t)
    return pl.pallas_call(
        functools.partial(_norm_matmul_kernel, w_transposed, chunk, w_aux_t is not None),
        grid=(ni, nj),
        in_specs=in_specs,
        out_specs=out_specs,
        out_shape=out_shape,
        scratch_shapes=[pltpu.VMEM((2, tm, k), BF16)],
        compiler_params=_cparams(2),
        name="norm_matmul",
    )(*args)


def _gla_kernel(rev, nchunk, *refs):
    if rev:
        (q_ref, k_ref, v_ref, lr_ref, wg_ref, bg_ref, gate_ref, of_ref, gh_ref,
         o_ref, s_ref) = refs
    else:
        q_ref, k_ref, v_ref, lr_ref, wg_ref, bg_ref, o_ref, s_ref = refs
    C = GLA_CHUNK

    @pl.when(pl.program_id(1) == 0)
    def _():
        s_ref[...] = jnp.zeros_like(s_ref)

    row = lax.broadcasted_iota(jnp.int32, (C, C), 0)
    col = lax.broadcasted_iota(jnp.int32, (C, C), 1)
    tri = (col >= row) if rev else (col <= row)
    tri_bf = jnp.where(tri, 1.0 / GLA_TAU, 0.0).astype(BF16)
    row2 = lax.broadcasted_iota(jnp.int32, (2 * C, 2 * C), 0)
    col2 = lax.broadcasted_iota(jnp.int32, (2 * C, 2 * C), 1)
    same_head = (row2 < C) == (col2 < C)
    amask = same_head & ((col2 > row2) if rev else (col2 <= row2))
    q_lo = lax.broadcasted_iota(jnp.int32, (C, LANES), 1) < GLA_DK

    npair = GLA_HEADS // 2

    def gate_logits(c):
        rows = pl.ds(c * C, C)
        z = _dot(lr_ref[rows, :], wg_ref[...]) + bg_ref[...]
        lg = jnp.minimum(z, 0.0) - jnp.log(1.0 + jnp.exp(-jnp.abs(z)))
        hi = lg.astype(BF16)
        return hi, (lg - hi.astype(F32)).astype(BF16)

    def decayed_qk(c, hi, lo):
        rows = pl.ds(c * C, C)
        b = _dot(tri_bf, hi) + _dot(tri_bf, lo)
        bl = b[0:1, :] if rev else b[C - 1:C, :]
        q = q_ref[rows, :].astype(F32)
        k = k_ref[rows, :].astype(F32)
        q_in = (q * jnp.exp(b) * (GLA_DK ** -0.5)).astype(BF16)
        k_out = (k * jnp.exp(-b)).astype(BF16)
        k_last = (k * jnp.exp(bl - b)).astype(BF16)
        return q_in, k_out, k_last, jnp.exp(bl)

    def intra(c, prep):
        rows = pl.ds(c * C, C)
        q_in, k_out, k_last, _ = prep
        out = []
        for p in range(npair):
            sl = slice(p * LANES, (p + 1) * LANES)
            ql, ko, kl = q_in[:, sl], k_out[:, sl], k_last[:, sl]
            zero = jnp.zeros_like(ql)
            qs = jnp.concatenate([jnp.where(q_lo, ql, zero), jnp.where(q_lo, zero, ql)], axis=0)
            ks = jnp.concatenate([jnp.where(q_lo, kl, zero), jnp.where(q_lo, zero, kl)], axis=0)
            vs = jnp.concatenate([v_ref[rows, (2 * p + i) * GLA_DV:(2 * p + i + 1) * GLA_DV]
                                  for i in range(2)], axis=0)
            a_raw = _dot_nt(qs, jnp.concatenate([ko, ko], axis=0))
            out.append((qs, vs, a_raw, _dot_tn(ks, vs)))
        return out

    def outputs(c, prep, pairs):
        rows = pl.ds(c * C, C)
        dec = prep[3]
        for p, (qs, vs, a_raw, kv) in enumerate(pairs):
            a = jnp.where(amask, a_raw, 0.0).astype(BF16)
            st = s_ref[p]
            o2 = _dot(jnp.concatenate([a, qs], axis=1),
                      jnp.concatenate([vs, st.astype(BF16)], axis=0))
            dec_col = jnp.broadcast_to(dec[:, p * LANES:(p + 1) * LANES], (LANES, LANES)).T
            s_ref[p] = dec_col * st + kv
            for i in range(2):
                hs = slice((2 * p + i) * GLA_DV, (2 * p + i + 1) * GLA_DV)
                o = o2[i * C:(i + 1) * C]
                if rev:
                    tot = o + of_ref[rows, hs]
                    ms = jnp.mean(tot * tot, axis=-1, keepdims=True)
                    y = tot * lax.rsqrt(ms + EPS) * gh_ref[...]
                    o_ref[rows, hs] = (y * _silu(gate_ref[rows, hs].astype(F32))).astype(o_ref.dtype)
                else:
                    o_ref[rows, hs] = o

    order = [(nchunk - 1 - i) if rev else i for i in range(nchunk)]
    prep = decayed_qk(order[0], *gate_logits(order[0]))
    for i, c in enumerate(order):
        nxt = order[i + 1] if i + 1 < nchunk else None
        if nxt is not None:
            hi_lo = gate_logits(nxt)
        pairs = intra(c, prep)
        if nxt is not None:
            prep_next = decayed_qk(nxt, *hi_lo)
        outputs(c, prep, pairs)
        if nxt is not None:
            prep = prep_next


def _gla_call(p_act, aux, wg, bg, rev, bsz, t, tb, extra=None):
    nblk = t // tb
    nchunk = tb // GLA_CHUNK

    def rowblk(b, i):
        return b * nblk + ((nblk - 1 - i) if rev else i)

    def act(width, colblk):
        return pl.BlockSpec((tb, width), lambda b, i: (rowblk(b, i), colblk))

    def full(shape):
        return pl.BlockSpec(shape, lambda b, i: (0,) * len(shape))

    in_specs = [
        act(GLA_HK, COL_G_Q // GLA_HK),
        act(GLA_HK, COL_G_K // GLA_HK),
        act(GLA_W, COL_G_V // GLA_W),
        act(LANES, AUX_G_LR // LANES),
        full(wg.shape),
        full(bg.shape),
    ]
    args = [p_act, p_act, p_act, aux, wg, bg]
    if rev:
        o_f, g_head = extra
        in_specs += [act(GLA_W, COL_G_GATE // GLA_W), act(GLA_W, 0), full(g_head.shape)]
        args += [p_act, o_f, g_head]
        out_dtype = BF16
    else:
        out_dtype = F32
    return pl.pallas_call(
        functools.partial(_gla_kernel, rev, nchunk),
        grid=(bsz, nblk),
        in_specs=in_specs,
        out_specs=act(GLA_W, 0),
        out_shape=jax.ShapeDtypeStruct((bsz * t, GLA_W), out_dtype),
        scratch_shapes=[pltpu.VMEM((GLA_HEADS // 2, GLA_DV, LANES), F32)],
        compiler_params=_cparams(2),
        name="gla_bwd" if rev else "gla_fwd",
    )(*args)


def _rope(r, cos_t, sin_t):
    half = MLA_ROPE // 2
    lane = lax.broadcasted_iota(jnp.int32, r.shape, 1)
    swapped = jnp.where(lane % MLA_ROPE < half,
                        pltpu.roll(r, LANES - half, 1), pltpu.roll(r, half, 1))
    return r * cos_t + swapped * sin_t


def _mla_prep_kernel(cq_ref, ckv_ref, kr_ref, gq_ref, gkv_ref, wqn_ref, wqr_ref, wk_ref, wv_ref,
                     cos_ref, sin_ref, q_ref, k_ref, v_ref):
    scale = (MLA_NOPE + MLA_ROPE) ** -0.5 * LOG2E
    cos_t = cos_ref[...]
    sin_t = sin_ref[...]

    cq = cq_ref[...].astype(F32)
    ms = jnp.mean(cq * cq, axis=-1, keepdims=True)
    qn = (cq * lax.rsqrt(ms + EPS) * gq_ref[...]).astype(BF16)
    ckv = ckv_ref[...].astype(F32)
    ms = jnp.mean(ckv * ckv, axis=-1, keepdims=True)
    kvn = (ckv * lax.rsqrt(ms + EPS) * gkv_ref[...]).astype(BF16)

    tm = qn.shape[0]
    low = lax.broadcasted_iota(jnp.int32, (tm, LANES), 1) < MLA_ROPE
    k_rope_even = _rope(kr_ref[...].astype(F32), cos_t, sin_t)
    k_rope = (k_rope_even.astype(BF16), pltpu.roll(k_rope_even, MLA_ROPE, 1).astype(BF16))
    k_nope = _dot(kvn, wk_ref[...]).astype(BF16)
    v = _dot(kvn, wv_ref[...]).astype(BF16)
    ones_col = jnp.where(
        lax.broadcasted_iota(jnp.int32, (tm, MLA_V_PAD - MLA_V), 1) == 0, 1.0, 0.0).astype(BF16)
    for h in range(MLA_HEADS):
        c0 = h * MLA_QK_PAD
        k_ref[:, c0:c0 + MLA_NOPE] = k_nope[:, h * MLA_NOPE:(h + 1) * MLA_NOPE]
        k_ref[:, c0 + MLA_NOPE:c0 + MLA_QK_PAD] = k_rope[h % 2]
        v0 = h * MLA_V_PAD
        v_ref[:, v0:v0 + MLA_V] = v[:, h * MLA_V:(h + 1) * MLA_V]
        v_ref[:, v0 + MLA_V:v0 + MLA_V_PAD] = ones_col
    for p in range(MLA_HEADS // 2):
        q2 = _dot(qn, wqn_ref[:, p * 2 * MLA_NOPE:(p + 1) * 2 * MLA_NOPE])
        for i in range(2):
            c0 = (2 * p + i) * MLA_QK_PAD
            q_ref[:, c0:c0 + MLA_NOPE] = (
                q2[:, i * MLA_NOPE:(i + 1) * MLA_NOPE] * scale).astype(BF16)
    for g in range(MLA_HEADS // 4):
        q4 = _dot(qn, wqr_ref[:, g * 4 * MLA_ROPE:(g + 1) * 4 * MLA_ROPE])
        for s in range(2):
            tile = _rope(q4[:, s * LANES:(s + 1) * LANES], cos_t, sin_t) * scale
            zero = jnp.zeros_like(tile)
            c0 = (4 * g + 2 * s) * MLA_QK_PAD + MLA_NOPE
            q_ref[:, c0:c0 + LANES] = jnp.where(low, tile, zero).astype(BF16)
            c1 = c0 + MLA_QK_PAD
            q_ref[:, c1:c1 + LANES] = jnp.where(low, zero, tile).astype(BF16)


def _mla_prep(p_act, aux, g_q, g_kv, wqn, wqr, wk, wv, cos_t, sin_t, t, tm):
    m = p_act.shape[0]
    nt = t // tm

    def act(width, colblk):
        return pl.BlockSpec((tm, width), lambda i: (i, colblk))

    def full(shape):
        return pl.BlockSpec(shape, lambda i: (0,) * len(shape))

    table = pl.BlockSpec((tm, LANES), lambda i: (i % nt, 0))
    qk_w = MLA_HEADS * MLA_QK_PAD
    return pl.pallas_call(
        _mla_prep_kernel,
        grid=(m // tm,),
        in_specs=[
            act(MLA_Q_LORA, COL_M_CQ // MLA_Q_LORA),
            act(MLA_KV_LORA, COL_M_CKV // MLA_KV_LORA),
            act(LANES, AUX_M_KR // LANES),
            full(g_q.shape), full(g_kv.shape), full(wqn.shape), full(wqr.shape),
            full(wk.shape), full(wv.shape),
            table, table,
        ],
        out_specs=[act(qk_w, 0), act(qk_w, 0), act(MLA_HEADS * MLA_V_PAD, 0)],
        out_shape=[
            jax.ShapeDtypeStruct((m, qk_w), BF16),
            jax.ShapeDtypeStruct((m, qk_w), BF16),
            jax.ShapeDtypeStruct((m, MLA_HEADS * MLA_V_PAD), BF16),
        ],
        compiler_params=_cparams(1),
        name="mla_prep",
    )(p_act, p_act, aux, g_q, g_kv, wqn, wqr, wk, wv, cos_t, sin_t)


ATTN_SPLIT = 2


def _mla_attn_kernel(nkv, bk, q_ref, k_ref, v_ref, gate_ref, o_ref, s_ref):
    tq = q_ref.shape[0] // ATTN_SPLIT
    q = [q_ref[h * tq:(h + 1) * tq, :] for h in range(ATTN_SPLIT)]

    def scores(h, j):
        s_ref[2 * h + j % 2] = _dot_nt(q[h], k_ref[j * bk:(j + 1) * bk, :])

    m = [jnp.full((tq, 1), -jnp.inf, F32) for _ in range(ATTN_SPLIT)]
    acc = [jnp.zeros((tq, MLA_V_PAD), F32) for _ in range(ATTN_SPLIT)]
    for h in range(ATTN_SPLIT):
        scores(h, 0)
    for j in range(nkv):
        for h in range(ATTN_SPLIT):
            if j + 1 < nkv:
                scores(h, j + 1)
            s = s_ref[2 * h + j % 2]
            m_new = jnp.maximum(m[h], jnp.max(s, axis=-1, keepdims=True))
            alpha = jnp.exp2(m[h] - m_new)
            p = jnp.exp2(s - m_new).astype(BF16)
            acc[h] = alpha * acc[h] + _dot(p, v_ref[j * bk:(j + 1) * bk, :])
            m[h] = m_new
    for h in range(ATTN_SPLIT):
        rows = slice(h * tq, (h + 1) * tq)
        o = acc[h][:, :MLA_V] / acc[h][:, MLA_V:MLA_V + 1]
        o_ref[rows, :] = (o * _silu(gate_ref[rows, :].astype(F32))).astype(o_ref.dtype)


def _mla_attn(q, k, v, p_act, bsz, t, tq, bk):
    nq = t // tq
    nkv = t // bk
    gate_col0 = COL_M_GATE // MLA_V
    return pl.pallas_call(
        functools.partial(_mla_attn_kernel, nkv, bk),
        grid=(bsz, MLA_HEADS, nq),
        in_specs=[
            pl.BlockSpec((tq, MLA_QK_PAD), lambda b, h, i: (b * nq + i, h)),
            pl.BlockSpec((t, MLA_QK_PAD), lambda b, h, i: (b, h)),
            pl.BlockSpec((t, MLA_V_PAD), lambda b, h, i: (b, h)),
            pl.BlockSpec((tq, MLA_V), lambda b, h, i: (b * nq + i, gate_col0 + h)),
        ],
        out_specs=pl.BlockSpec((tq, MLA_V), lambda b, h, i: (b * nq + i, h)),
        out_shape=jax.ShapeDtypeStruct((bsz * t, MLA_W), BF16),
        scratch_shapes=[pltpu.VMEM((2 * ATTN_SPLIT, tq // ATTN_SPLIT, bk), F32)],
        compiler_params=_cparams(3),
        name="mla_attn",
    )(q, k, v, p_act)


def _mem_attn_kernel(q_ref, gate_ref, kv_ref, o_ref):
    heads = [slice(h * MEM_DH, (h + 1) * MEM_DH) for h in range(MEM_HEADS)]
    scores = [_dot_nt(q_ref[:, hs], kv_ref[:, hs]) for hs in heads]
    for h, hs in enumerate(heads):
        s = scores[h] * (MEM_DH ** -0.5)
        p = jnp.exp(s - jnp.max(s, axis=-1, keepdims=True))
        l = jnp.sum(p, axis=-1, keepdims=True)
        o = _dot(p.astype(BF16), kv_ref[:, MEM_W + h * MEM_DH:MEM_W + (h + 1) * MEM_DH]) / l
        o_ref[:, hs] = (o * _silu(gate_ref[:, hs].astype(F32))).astype(o_ref.dtype)


def _mem_attn(p_act, kv_mem, bsz, t, tq):
    nq = t // tq
    return pl.pallas_call(
        _mem_attn_kernel,
        grid=(bsz, nq),
        in_specs=[
            pl.BlockSpec((tq, MEM_W), lambda b, i: (b * nq + i, COL_C_Q // MEM_W)),
            pl.BlockSpec((tq, MEM_W), lambda b, i: (b * nq + i, COL_C_GATE // MEM_W)),
            pl.BlockSpec((N_MEM, 2 * MEM_W), lambda b, i: (b, 0)),
        ],
        out_specs=pl.BlockSpec((tq, MEM_W), lambda b, i: (b * nq + i, 0)),
        out_shape=jax.ShapeDtypeStruct((bsz * t, MEM_W), BF16),
        compiler_params=_cparams(2),
        name="mem_attn",
    )(p_act, p_act, kv_mem)


OUT_TN = 1024


def _out_kernel(oa_ref, ob_ref, oc_ref, w_ref, x_ref, g_ref, y_ref):
    ssq = jnp.zeros((y_ref.shape[0], 1), F32)
    for c0 in range(0, D_MODEL, OUT_TN):
        cs = slice(c0, c0 + OUT_TN)
        acc = (_dot(oa_ref[...], w_ref[0:GLA_W, cs])
               + _dot(ob_ref[...], w_ref[GLA_W:GLA_W + MLA_W, cs])
               + _dot(oc_ref[...], w_ref[GLA_W + MLA_W:, cs]) + x_ref[:, cs])
        ssq = ssq + jnp.sum(acc * acc, axis=-1, keepdims=True)
        y_ref[:, cs] = acc
    r = lax.rsqrt(ssq * (1.0 / D_MODEL) + EPS)
    for c0 in range(0, D_MODEL, OUT_TN):
        cs = slice(c0, c0 + OUT_TN)
        y_ref[:, cs] = y_ref[:, cs] * r * g_ref[:, cs]


def _out_proj(o_a, o_b, o_c, w_out, x, g_final, tm):
    m = x.shape[0]
    return pl.pallas_call(
        _out_kernel,
        grid=(m // tm,),
        in_specs=[
            pl.BlockSpec((tm, GLA_W), lambda i: (i, 0)),
            pl.BlockSpec((tm, MLA_W), lambda i: (i, 0)),
            pl.BlockSpec((tm, MEM_W), lambda i: (i, 0)),
            pl.BlockSpec((D_MODEL, D_MODEL), lambda i: (0, 0), pipeline_mode=pl.Buffered(1)),
            pl.BlockSpec((tm, D_MODEL), lambda i: (i, 0)),
            pl.BlockSpec((1, D_MODEL), lambda i: (0, 0)),
        ],
        out_specs=pl.BlockSpec((tm, D_MODEL), lambda i: (i, 0)),
        out_shape=jax.ShapeDtypeStruct((m, D_MODEL), F32),
        compiler_params=_cparams(1),
        name="out_proj",
    )(o_a, o_b, o_c, w_out, x, g_final)


IN_WIDTHS = (GLA_HK, GLA_HK, GLA_W, 2 * GLA_LR, GLA_W, MLA_Q_LORA, MLA_KV_LORA, MLA_ROPE,
             MLA_W, MEM_W, MEM_W)
IN_DST = ((False, COL_G_Q), (False, COL_G_K), (False, COL_G_V), (True, AUX_G_LR),
          (False, COL_G_GATE), (False, COL_M_CQ), (False, COL_M_CKV), (True, AUX_M_KR),
          (False, COL_M_GATE), (False, COL_C_Q), (False, COL_C_GATE))
N_IN = sum(IN_WIDTHS)
W_IN_COLS = 256


def _w_in_relayout_kernel(w_ref, o_ref, a_ref):
    a_ref[...] = jnp.zeros_like(a_ref)
    src = 0
    for width, (is_aux, dst) in zip(IN_WIDTHS, IN_DST):
        dst_ref = a_ref if is_aux else o_ref
        dst_ref[dst:dst + width, :] = w_ref[src:src + width, :].astype(dst_ref.dtype)
        src += width


def _prep_w_in(w_in):
    k = w_in.shape[0]
    return pl.pallas_call(
        _w_in_relayout_kernel,
        grid=(k // W_IN_COLS,),
        in_specs=[pl.BlockSpec((N_IN, W_IN_COLS), lambda i: (0, i))],
        out_specs=[pl.BlockSpec((P_COLS, W_IN_COLS), lambda i: (0, i)),
                   pl.BlockSpec((AUX_COLS, W_IN_COLS), lambda i: (0, i))],
        out_shape=[jax.ShapeDtypeStruct((P_COLS, k), BF16),
                   jax.ShapeDtypeStruct((AUX_COLS, k), BF16)],
        compiler_params=_cparams(1),
        name="w_in_relayout",
    )(w_in.T)


def _prep_gla_gate(w_g2, b_g):
    wgs = []
    for d in range(2):
        w = jnp.zeros((LANES, GLA_HK), F32).at[d * GLA_LR:(d + 1) * GLA_LR].set(w_g2[d])
        wgs.append(w.astype(BF16))
    return wgs, [b_g[0][None, :], b_g[1][None, :]]


def _prep_mla(w_uq, w_ukv):
    wq = w_uq.reshape(MLA_Q_LORA, MLA_HEADS, MLA_NOPE + MLA_ROPE)
    wqn = wq[:, :, :MLA_NOPE].reshape(MLA_Q_LORA, MLA_HEADS * MLA_NOPE).astype(BF16)
    wqr = wq[:, :, MLA_NOPE:].reshape(MLA_Q_LORA, MLA_HEADS * MLA_ROPE).astype(BF16)
    wkv = w_ukv.reshape(MLA_KV_LORA, MLA_HEADS, MLA_NOPE + MLA_V)
    wk = wkv[:, :, :MLA_NOPE].reshape(MLA_KV_LORA, MLA_HEADS * MLA_NOPE).astype(BF16)
    wv = wkv[:, :, MLA_NOPE:].reshape(MLA_KV_LORA, MLA_W).astype(BF16)
    return wqn, wqr, wk, wv


def _rope_tables(t):
    pos = jnp.arange(t, dtype=F32)
    inv = 1.0 / (ROPE_THETA ** (jnp.arange(0, MLA_ROPE, 2, dtype=F32) / MLA_ROPE))
    ang = pos[:, None] * inv[None, :]
    cos, sin = jnp.cos(ang), jnp.sin(ang)
    return (jnp.concatenate([cos, cos, cos, cos], axis=-1),
            jnp.concatenate([-sin, sin, -sin, sin], axis=-1))


def _pick(t, pref):
    return pref if t % pref == 0 else t


ROW_TILE = 512
GLA_BLOCK = 1024
ATTN_Q_TILE = 512 * ATTN_SPLIT
ATTN_KV_TILE = 2048
MEM_KV_TN = 512
OUT_ROW_TILE = 256


def _trunk(x, mem, w):
    bsz, t, _ = x.shape
    m = bsz * t
    xf = x.reshape(m, D_MODEL)
    p_act, aux = _norm_matmul(xf, w["g_in"], w["w_in_t"], _pick(m, ROW_TILE), P_TN,
                              w_transposed=True, w_aux_t=w["w_aux_t"])

    tb = _pick(t, GLA_BLOCK)
    o_f = _gla_call(p_act, aux, w["wg"][0], w["bg"][0], False, bsz, t, tb)
    o_a = _gla_call(p_act, aux, w["wg"][1], w["bg"][1], True, bsz, t, tb, (o_f, w["g_head"]))

    cos_t, sin_t = _rope_tables(t)
    q, k, v = _mla_prep(p_act, aux, w["g_q"], w["g_kv"], w["wqn"], w["wqr"], w["wk"], w["wv"],
                        cos_t, sin_t, t, _pick(t, ROW_TILE))
    o_b = _mla_attn(q, k, v, p_act, bsz, t, _pick(t, ATTN_Q_TILE), _pick(t, ATTN_KV_TILE))

    kv_mem = _norm_matmul(mem.reshape(bsz * N_MEM, D_MODEL), w["mem_g"], w["mem_w_kv"],
                          bsz * N_MEM, MEM_KV_TN)
    o_c = _mem_attn(p_act, kv_mem, bsz, t, _pick(t, ROW_TILE))

    y = _out_proj(o_a, o_b, o_c, w["w_out"], xf, w["g_final"], _pick(m, OUT_ROW_TILE))
    return y.reshape(bsz, t, D_MODEL)


def kernel(x_prompt, x_sample, mem_prompt, mem_sample, g_in, w_in, gla_w_g2, gla_b_g, gla_g_head,
           mla_g_q, mla_w_uq, mla_g_kv, mla_w_ukv, mem_g, mem_w_kv, w_out, g_final):
    wg, bg = _prep_gla_gate(gla_w_g2[0], gla_b_g[0])
    wqn, wqr, wk, wv = _prep_mla(mla_w_uq[0], mla_w_ukv[0])
    w_in_t, w_aux_t = _prep_w_in(w_in[0])
    w = {
        "g_in": g_in[0][None, :],
        "w_in_t": w_in_t, "w_aux_t": w_aux_t,
        "wg": wg, "bg": bg,
        "g_head": gla_g_head[0][None, :],
        "g_q": mla_g_q[0][None, :], "g_kv": mla_g_kv[0][None, :],
        "wqn": wqn, "wqr": wqr, "wk": wk, "wv": wv,
        "mem_g": mem_g[0][None, :],
        "mem_w_kv": mem_w_kv[0].astype(BF16),
        "w_out": w_out[0].astype(BF16),
        "g_final": g_final[None, :],
    }
    return (_trunk(x_prompt, mem_prompt, w), _trunk(x_sample, mem_sample, w))
```

```python
import functools
import math

import jax
import jax.numpy as jnp
from jax import lax
from jax.experimental import pallas as pl
from jax.experimental.pallas import tpu as pltpu

F32 = jnp.float32
BF16 = jnp.bfloat16

D_MODEL = 4096
N_MEM = 256
GLA_HEADS = 12
GLA_DK = 64
GLA_DV = 128
GLA_LR = 16
GLA_TAU = 16.0
assert math.log2(GLA_TAU).is_integer()
GLA_CHUNK = 64
GLA_HK = GLA_HEADS * GLA_DK
GLA_W = GLA_HEADS * GLA_DV
MLA_HEADS = 12
MLA_Q_LORA = 1536
MLA_KV_LORA = 512
MLA_NOPE = 128
MLA_ROPE = 64
MLA_V = 128
ROPE_THETA = 10000.0
MLA_W = MLA_HEADS * MLA_V
MLA_QK_PAD = 256
MLA_V_PAD = 256
MEM_HEADS = 4
MEM_DH = 256
MEM_W = MEM_HEADS * MEM_DH
EPS = 1e-6
LANES = 128
LOG2E = 1.4426950408889634

COL_G_V = 0
COL_G_GATE = 1536
COL_M_CQ = 3072
COL_M_GATE = 4608
COL_G_Q = 6144
COL_G_K = 6912
COL_M_CKV = 7680
COL_C_Q = 8192
COL_C_GATE = 9216
COL_G_LR = 10240
COL_M_KR = 10368
P_USED = 10496
P_TN = 1536
P_COLS = 10752

VMEM_LIMIT = 60 * 1024 * 1024


def _cparams(n_axes):
    return pltpu.CompilerParams(
        dimension_semantics=("arbitrary",) * n_axes, vmem_limit_bytes=VMEM_LIMIT)


def _dot(a, b):
    return jnp.dot(a, b, preferred_element_type=F32)


def _dot_nt(a, b):
    return lax.dot_general(a, b, (((1,), (1,)), ((), ())), preferred_element_type=F32)


def _dot_tn(a, b):
    return lax.dot_general(a, b, (((0,), (0,)), ((), ())), preferred_element_type=F32)


def _silu(x):
    return x * jax.nn.sigmoid(x)


NORM_ROWS = 64
BF16_SUBLANES = 16


def _norm_matmul_kernel(w_transposed, chunk, x_ref, g_ref, w_ref, o_ref, h_ref):
    i = pl.program_id(0)
    j = pl.program_id(1)
    tm = x_ref.shape[0]
    dot = _dot_nt if w_transposed else _dot

    def normalise(slot, r0, rows):
        x = x_ref[pl.ds(r0, rows), :]
        ms = jnp.mean(x * x, axis=-1, keepdims=True)
        h_ref[slot, pl.ds(r0, rows), :] = (x * lax.rsqrt(ms + EPS) * g_ref[...]).astype(BF16)

    @pl.when(j == 0)
    def _():
        @pl.when(i == 0)
        def _():
            def rows(c, carry):
                normalise(0, pl.multiple_of(c * NORM_ROWS, NORM_ROWS), NORM_ROWS)
                return carry
            lax.fori_loop(0, tm // NORM_ROWS, rows, 0)

        o_ref[...] = dot(h_ref[i % 2], w_ref[...]).astype(o_ref.dtype)

    @pl.when(j > 0)
    def _():
        r0 = pl.multiple_of(jnp.minimum((j - 1) * chunk, tm - chunk), BF16_SUBLANES)
        o_ref[...] = dot(h_ref[i % 2], w_ref[...]).astype(o_ref.dtype)
        normalise((i + 1) % 2, r0, chunk)


def _norm_matmul(x, g, w, tm, tn, w_transposed=False):
    m, k = x.shape
    n = w.shape[0] if w_transposed else w.shape[1]
    ni, nj = m // tm, n // tn
    chunk = -(-tm // (nj - 1))
    chunk = -(-chunk // BF16_SUBLANES) * BF16_SUBLANES
    assert chunk * (nj - 1) >= tm and (tm - chunk) % BF16_SUBLANES == 0

    def x_map(i, j):
        return (jnp.where((i == 0) & (j == 0), 0, jnp.minimum(i + 1, ni - 1)), 0)

    if w_transposed:
        w_spec = pl.BlockSpec((tn, k), lambda i, j: (j, 0))
    else:
        w_spec = pl.BlockSpec((k, tn), lambda i, j: (0, j))
    return pl.pallas_call(
        functools.partial(_norm_matmul_kernel, w_transposed, chunk),
        grid=(ni, nj),
        in_specs=[
            pl.BlockSpec((tm, k), x_map),
            pl.BlockSpec((1, k), lambda i, j: (0, 0)),
            w_spec,
        ],
        out_specs=pl.BlockSpec((tm, tn), lambda i, j: (i, j)),
        out_shape=jax.ShapeDtypeStruct((m, n), BF16),
        scratch_shapes=[pltpu.VMEM((2, tm, k), BF16)],
        compiler_params=_cparams(2),
        name="norm_matmul",
    )(x, g, w)


def _gla_kernel(rev, nchunk, *refs):
    if rev:
        (q_ref, k_ref, v_ref, lr_ref, wg_ref, bg_ref, gate_ref, of_ref, gh_ref,
         o_ref, s_ref) = refs
    else:
        q_ref, k_ref, v_ref, lr_ref, wg_ref, bg_ref, o_ref, s_ref = refs
    C = GLA_CHUNK

    @pl.when(pl.program_id(1) == 0)
    def _():
        s_ref[...] = jnp.zeros_like(s_ref)

    row = lax.broadcasted_iota(jnp.int32, (C, C), 0)
    col = lax.broadcasted_iota(jnp.int32, (C, C), 1)
    tri = (col >= row) if rev else (col <= row)
    tri_bf = jnp.where(tri, 1.0 / GLA_TAU, 0.0).astype(BF16)
    row2 = lax.broadcasted_iota(jnp.int32, (2 * C, 2 * C), 0)
    col2 = lax.broadcasted_iota(jnp.int32, (2 * C, 2 * C), 1)
    same_head = (row2 < C) == (col2 < C)
    amask = same_head & ((col2 > row2) if rev else (col2 <= row2))
    q_lo = lax.broadcasted_iota(jnp.int32, (C, LANES), 1) < GLA_DK

    npair = GLA_HEADS // 2

    def gate_logits(c):
        rows = pl.ds(c * C, C)
        z = _dot(lr_ref[rows, :], wg_ref[...]) + bg_ref[...]
        lg = jnp.minimum(z, 0.0) - jnp.log(1.0 + jnp.exp(-jnp.abs(z)))
        hi = lg.astype(BF16)
        return hi, (lg - hi.astype(F32)).astype(BF16)

    def decayed_qk(c, hi, lo):
        rows = pl.ds(c * C, C)
        b = _dot(tri_bf, hi) + _dot(tri_bf, lo)
        bl = b[0:1, :] if rev else b[C - 1:C, :]
        q = q_ref[rows, :].astype(F32)
        k = k_ref[rows, :].astype(F32)
        q_in = (q * jnp.exp(b) * (GLA_DK ** -0.5)).astype(BF16)
        k_out = (k * jnp.exp(-b)).astype(BF16)
        k_last = (k * jnp.exp(bl - b)).astype(BF16)
        return q_in, k_out, k_last, jnp.exp(bl)

    def intra(c, prep):
        rows = pl.ds(c * C, C)
        q_in, k_out, k_last, _ = prep
        out = []
        for p in range(npair):
            sl = slice(p * LANES, (p + 1) * LANES)
            ql, ko, kl = q_in[:, sl], k_out[:, sl], k_last[:, sl]
            zero = jnp.zeros_like(ql)
            qs = jnp.concatenate([jnp.where(q_lo, ql, zero), jnp.where(q_lo, zero, ql)], axis=0)
            ks = jnp.concatenate([jnp.where(q_lo, kl, zero), jnp.where(q_lo, zero, kl)], axis=0)
            vs = jnp.concatenate([v_ref[rows, (2 * p + i) * GLA_DV:(2 * p + i + 1) * GLA_DV]
                                  for i in range(2)], axis=0)
            a_raw = _dot_nt(qs, jnp.concatenate([ko, ko], axis=0))
            out.append((qs, vs, a_raw, _dot_tn(ks, vs)))
        return out

    def outputs(c, prep, pairs):
        rows = pl.ds(c * C, C)
        dec = prep[3]
        for p, (qs, vs, a_raw, kv) in enumerate(pairs):
            a = jnp.where(amask, a_raw, 0.0).astype(BF16)
            st = s_ref[p]
            o2 = _dot(jnp.concatenate([a, qs], axis=1),
                      jnp.concatenate([vs, st.astype(BF16)], axis=0))
            dec_col = jnp.broadcast_to(dec[:, p * LANES:(p + 1) * LANES], (LANES, LANES)).T
            s_ref[p] = dec_col * st + kv
            for i in range(2):
                hs = slice((2 * p + i) * GLA_DV, (2 * p + i + 1) * GLA_DV)
                o = o2[i * C:(i + 1) * C]
                if rev:
                    tot = o + of_ref[rows, hs]
                    ms = jnp.mean(tot * tot, axis=-1, keepdims=True)
                    y = tot * lax.rsqrt(ms + EPS) * gh_ref[...]
                    o_ref[rows, hs] = (y * _silu(gate_ref[rows, hs].astype(F32))).astype(o_ref.dtype)
                else:
                    o_ref[rows, hs] = o

    order = [(nchunk - 1 - i) if rev else i for i in range(nchunk)]
    prep = decayed_qk(order[0], *gate_logits(order[0]))
    for i, c in enumerate(order):
        nxt = order[i + 1] if i + 1 < nchunk else None
        if nxt is not None:
            hi_lo = gate_logits(nxt)
        pairs = intra(c, prep)
        if nxt is not None:
            prep_next = decayed_qk(nxt, *hi_lo)
        outputs(c, prep, pairs)
        if nxt is not None:
            prep = prep_next


def _gla_call(p_act, wg, bg, rev, bsz, t, tb, extra=None):
    nblk = t // tb
    nchunk = tb // GLA_CHUNK

    def rowblk(b, i):
        return b * nblk + ((nblk - 1 - i) if rev else i)

    def act(width, colblk):
        return pl.BlockSpec((tb, width), lambda b, i: (rowblk(b, i), colblk))

    def full(shape):
        return pl.BlockSpec(shape, lambda b, i: (0,) * len(shape))

    in_specs = [
        act(GLA_HK, COL_G_Q // GLA_HK),
        act(GLA_HK, COL_G_K // GLA_HK),
        act(GLA_W, COL_G_V // GLA_W),
        act(LANES, COL_G_LR // LANES),
        full(wg.shape),
        full(bg.shape),
    ]
    args = [p_act, p_act, p_act, p_act, wg, bg]
    if rev:
        o_f, g_head = extra
        in_specs += [act(GLA_W, COL_G_GATE // GLA_W), act(GLA_W, 0), full(g_head.shape)]
        args += [p_act, o_f, g_head]
        out_dtype = BF16
    else:
        out_dtype = F32
    return pl.pallas_call(
        functools.partial(_gla_kernel, rev, nchunk),
        grid=(bsz, nblk),
        in_specs=in_specs,
        out_specs=act(GLA_W, 0),
        out_shape=jax.ShapeDtypeStruct((bsz * t, GLA_W), out_dtype),
        scratch_shapes=[pltpu.VMEM((GLA_HEADS // 2, GLA_DV, LANES), F32)],
        compiler_params=_cparams(2),
        name="gla_bwd" if rev else "gla_fwd",
    )(*args)


def _rope(r, cos_t, sin_t):
    half = MLA_ROPE // 2
    lane = lax.broadcasted_iota(jnp.int32, r.shape, 1)
    swapped = jnp.where(lane % MLA_ROPE < half,
                        pltpu.roll(r, LANES - half, 1), pltpu.roll(r, half, 1))
    return r * cos_t + swapped * sin_t


def _mla_prep_kernel(cq_ref, ckv_ref, kr_ref, gq_ref, gkv_ref, wqn_ref, wqr_ref, wk_ref, wv_ref,
                     cos_ref, sin_ref, q_ref, k_ref, v_ref):
    scale = (MLA_NOPE + MLA_ROPE) ** -0.5 * LOG2E
    cos_t = cos_ref[...]
    sin_t = sin_ref[...]

    cq = cq_ref[...].astype(F32)
    ms = jnp.mean(cq * cq, axis=-1, keepdims=True)
    qn = (cq * lax.rsqrt(ms + EPS) * gq_ref[...]).astype(BF16)
    ckv = ckv_ref[...].astype(F32)
    ms = jnp.mean(ckv * ckv, axis=-1, keepdims=True)
    kvn = (ckv * lax.rsqrt(ms + EPS) * gkv_ref[...]).astype(BF16)

    tm = qn.shape[0]
    low = lax.broadcasted_iota(jnp.int32, (tm, LANES), 1) < MLA_ROPE
    k_rope_even = _rope(kr_ref[...].astype(F32), cos_t, sin_t)
    k_rope = (k_rope_even.astype(BF16), pltpu.roll(k_rope_even, MLA_ROPE, 1).astype(BF16))
    k_nope = _dot(kvn, wk_ref[...]).astype(BF16)
    v = _dot(kvn, wv_ref[...]).astype(BF16)
    ones_col = jnp.where(
        lax.broadcasted_iota(jnp.int32, (tm, MLA_V_PAD - MLA_V), 1) == 0, 1.0, 0.0).astype(BF16)
    for h in range(MLA_HEADS):
        c0 = h * MLA_QK_PAD
        k_ref[:, c0:c0 + MLA_NOPE] = k_nope[:, h * MLA_NOPE:(h + 1) * MLA_NOPE]
        k_ref[:, c0 + MLA_NOPE:c0 + MLA_QK_PAD] = k_rope[h % 2]
        v0 = h * MLA_V_PAD
        v_ref[:, v0:v0 + MLA_V] = v[:, h * MLA_V:(h + 1) * MLA_V]
        v_ref[:, v0 + MLA_V:v0 + MLA_V_PAD] = ones_col
    for p in range(MLA_HEADS // 2):
        q2 = _dot(qn, wqn_ref[:, p * 2 * MLA_NOPE:(p + 1) * 2 * MLA_NOPE])
        for i in range(2):
            c0 = (2 * p + i) * MLA_QK_PAD
            q_ref[:, c0:c0 + MLA_NOPE] = (
                q2[:, i * MLA_NOPE:(i + 1) * MLA_NOPE] * scale).astype(BF16)
    for g in range(MLA_HEADS // 4):
        q4 = _dot(qn, wqr_ref[:, g * 4 * MLA_ROPE:(g + 1) * 4 * MLA_ROPE])
        for s in range(2):
            tile = _rope(q4[:, s * LANES:(s + 1) * LANES], cos_t, sin_t) * scale
            zero = jnp.zeros_like(tile)
            c0 = (4 * g + 2 * s) * MLA_QK_PAD + MLA_NOPE
            q_ref[:, c0:c0 + LANES] = jnp.where(low, tile, zero).astype(BF16)
            c1 = c0 + MLA_QK_PAD
            q_ref[:, c1:c1 + LANES] = jnp.where(low, zero, tile).astype(BF16)


def _mla_prep(p_act, g_q, g_kv, wqn, wqr, wk, wv, cos_t, sin_t, t, tm):
    m = p_act.shape[0]
    nt = t // tm

    def act(width, colblk):
        return pl.BlockSpec((tm, width), lambda i: (i, colblk))

    def full(shape):
        return pl.BlockSpec(shape, lambda i: (0,) * len(shape))

    table = pl.BlockSpec((tm, LANES), lambda i: (i % nt, 0))
    qk_w = MLA_HEADS * MLA_QK_PAD
    return pl.pallas_call(
        _mla_prep_kernel,
        grid=(m // tm,),
        in_specs=[
            act(MLA_Q_LORA, COL_M_CQ // MLA_Q_LORA),
            act(MLA_KV_LORA, COL_M_CKV // MLA_KV_LORA),
            act(LANES, COL_M_KR // LANES),
            full(g_q.shape), full(g_kv.shape), full(wqn.shape), full(wqr.shape),
            full(wk.shape), full(wv.shape),
            table, table,
        ],
        out_specs=[act(qk_w, 0), act(qk_w, 0), act(MLA_HEADS * MLA_V_PAD, 0)],
        out_shape=[
            jax.ShapeDtypeStruct((m, qk_w), BF16),
            jax.ShapeDtypeStruct((m, qk_w), BF16),
            jax.ShapeDtypeStruct((m, MLA_HEADS * MLA_V_PAD), BF16),
        ],
        compiler_params=_cparams(1),
        name="mla_prep",
    )(p_act, p_act, p_act, g_q, g_kv, wqn, wqr, wk, wv, cos_t, sin_t)


ATTN_SPLIT = 2


def _mla_attn_kernel(nkv, bk, q_ref, k_ref, v_ref, gate_ref, o_ref, s_ref):
    tq = q_ref.shape[0] // ATTN_SPLIT
    q = [q_ref[h * tq:(h + 1) * tq, :] for h in range(ATTN_SPLIT)]

    def scores(h, j):
        s_ref[2 * h + j % 2] = _dot_nt(q[h], k_ref[j * bk:(j + 1) * bk, :])

    m = [jnp.full((tq, 1), -jnp.inf, F32) for _ in range(ATTN_SPLIT)]
    acc = [jnp.zeros((tq, MLA_V_PAD), F32) for _ in range(ATTN_SPLIT)]
    for h in range(ATTN_SPLIT):
        scores(h, 0)
    for j in range(nkv):
        for h in range(ATTN_SPLIT):
            if j + 1 < nkv:
                scores(h, j + 1)
            s = s_ref[2 * h + j % 2]
            m_new = jnp.maximum(m[h], jnp.max(s, axis=-1, keepdims=True))
            alpha = jnp.exp2(m[h] - m_new)
            p = jnp.exp2(s - m_new).astype(BF16)
            acc[h] = alpha * acc[h] + _dot(p, v_ref[j * bk:(j + 1) * bk, :])
            m[h] = m_new
    for h in range(ATTN_SPLIT):
        rows = slice(h * tq, (h + 1) * tq)
        o = acc[h][:, :MLA_V] / acc[h][:, MLA_V:MLA_V + 1]
        o_ref[rows, :] = (o * _silu(gate_ref[rows, :].astype(F32))).astype(o_ref.dtype)


def _mla_attn(q, k, v, p_act, bsz, t, tq, bk):
    nq = t // tq
    nkv = t // bk
    gate_col0 = COL_M_GATE // MLA_V
    return pl.pallas_call(
        functools.partial(_mla_attn_kernel, nkv, bk),
        grid=(bsz, MLA_HEADS, nq),
        in_specs=[
            pl.BlockSpec((tq, MLA_QK_PAD), lambda b, h, i: (b * nq + i, h)),
            pl.BlockSpec((t, MLA_QK_PAD), lambda b, h, i: (b, h)),
            pl.BlockSpec((t, MLA_V_PAD), lambda b, h, i: (b, h)),
            pl.BlockSpec((tq, MLA_V), lambda b, h, i: (b * nq + i, gate_col0 + h)),
        ],
        out_specs=pl.BlockSpec((tq, MLA_V), lambda b, h, i: (b * nq + i, h)),
        out_shape=jax.ShapeDtypeStruct((bsz * t, MLA_W), BF16),
        scratch_shapes=[pltpu.VMEM((2 * ATTN_SPLIT, tq // ATTN_SPLIT, bk), F32)],
        compiler_params=_cparams(3),
        name="mla_attn",
    )(q, k, v, p_act)


def _mem_attn_kernel(q_ref, gate_ref, kv_ref, o_ref):
    heads = [slice(h * MEM_DH, (h + 1) * MEM_DH) for h in range(MEM_HEADS)]
    scores = [_dot_nt(q_ref[:, hs], kv_ref[:, hs]) for hs in heads]
    for h, hs in enumerate(heads):
        s = scores[h] * (MEM_DH ** -0.5)
        p = jnp.exp(s - jnp.max(s, axis=-1, keepdims=True))
        l = jnp.sum(p, axis=-1, keepdims=True)
        o = _dot(p.astype(BF16), kv_ref[:, MEM_W + h * MEM_DH:MEM_W + (h + 1) * MEM_DH]) / l
        o_ref[:, hs] = (o * _silu(gate_ref[:, hs].astype(F32))).astype(o_ref.dtype)


def _mem_attn(p_act, kv_mem, bsz, t, tq):
    nq = t // tq
    return pl.pallas_call(
        _mem_attn_kernel,
        grid=(bsz, nq),
        in_specs=[
            pl.BlockSpec((tq, MEM_W), lambda b, i: (b * nq + i, COL_C_Q // MEM_W)),
            pl.BlockSpec((tq, MEM_W), lambda b, i: (b * nq + i, COL_C_GATE // MEM_W)),
            pl.BlockSpec((N_MEM, 2 * MEM_W), lambda b, i: (b, 0)),
        ],
        out_specs=pl.BlockSpec((tq, MEM_W), lambda b, i: (b * nq + i, 0)),
        out_shape=jax.ShapeDtypeStruct((bsz * t, MEM_W), BF16),
        compiler_params=_cparams(2),
        name="mem_attn",
    )(p_act, p_act, kv_mem)


OUT_TN = 1024


def _out_kernel(oa_ref, ob_ref, oc_ref, w_ref, x_ref, g_ref, y_ref):
    ssq = jnp.zeros((y_ref.shape[0], 1), F32)
    for c0 in range(0, D_MODEL, OUT_TN):
        cs = slice(c0, c0 + OUT_TN)
        acc = (_dot(oa_ref[...], w_ref[0:GLA_W, cs])
               + _dot(ob_ref[...], w_ref[GLA_W:GLA_W + MLA_W, cs])
               + _dot(oc_ref[...], w_ref[GLA_W + MLA_W:, cs]) + x_ref[:, cs])
        ssq = ssq + jnp.sum(acc * acc, axis=-1, keepdims=True)
        y_ref[:, cs] = acc
    r = lax.rsqrt(ssq * (1.0 / D_MODEL) + EPS)
    for c0 in range(0, D_MODEL, OUT_TN):
        cs = slice(c0, c0 + OUT_TN)
        y_ref[:, cs] = y_ref[:, cs] * r * g_ref[:, cs]


def _out_proj(o_a, o_b, o_c, w_out, x, g_final, tm):
    m = x.shape[0]
    return pl.pallas_call(
        _out_kernel,
        grid=(m // tm,),
        in_specs=[
            pl.BlockSpec((tm, GLA_W), lambda i: (i, 0)),
            pl.BlockSpec((tm, MLA_W), lambda i: (i, 0)),
            pl.BlockSpec((tm, MEM_W), lambda i: (i, 0)),
            pl.BlockSpec((D_MODEL, D_MODEL), lambda i: (0, 0), pipeline_mode=pl.Buffered(1)),
            pl.BlockSpec((tm, D_MODEL), lambda i: (i, 0)),
            pl.BlockSpec((1, D_MODEL), lambda i: (0, 0)),
        ],
        out_specs=pl.BlockSpec((tm, D_MODEL), lambda i: (i, 0)),
        out_shape=jax.ShapeDtypeStruct((m, D_MODEL), F32),
        compiler_params=_cparams(1),
        name="out_proj",
    )(o_a, o_b, o_c, w_out, x, g_final)


IN_WIDTHS = (GLA_HK, GLA_HK, GLA_W, 2 * GLA_LR, GLA_W, MLA_Q_LORA, MLA_KV_LORA, MLA_ROPE,
             MLA_W, MEM_W, MEM_W)
IN_DST = (COL_G_Q, COL_G_K, COL_G_V, COL_G_LR, COL_G_GATE, COL_M_CQ, COL_M_CKV, COL_M_KR,
          COL_M_GATE, COL_C_Q, COL_C_GATE)
N_IN = sum(IN_WIDTHS)
W_IN_COLS = 256


def _w_in_relayout_kernel(w_ref, o_ref):
    o_ref[COL_G_LR:, :] = jnp.zeros((P_COLS - COL_G_LR, o_ref.shape[1]), o_ref.dtype)
    src = 0
    for width, dst in zip(IN_WIDTHS, IN_DST):
        o_ref[dst:dst + width, :] = w_ref[src:src + width, :].astype(o_ref.dtype)
        src += width


def _prep_w_in(w_in):
    k = w_in.shape[0]
    return pl.pallas_call(
        _w_in_relayout_kernel,
        grid=(k // W_IN_COLS,),
        in_specs=[pl.BlockSpec((N_IN, W_IN_COLS), lambda i: (0, i))],
        out_specs=pl.BlockSpec((P_COLS, W_IN_COLS), lambda i: (0, i)),
        out_shape=jax.ShapeDtypeStruct((P_COLS, k), BF16),
        compiler_params=_cparams(1),
        name="w_in_relayout",
    )(w_in.T)


def _prep_gla_gate(w_g2, b_g):
    wgs = []
    for d in range(2):
        w = jnp.zeros((LANES, GLA_HK), F32).at[d * GLA_LR:(d + 1) * GLA_LR].set(w_g2[d])
        wgs.append(w.astype(BF16))
    return wgs, [b_g[0][None, :], b_g[1][None, :]]


def _prep_mla(w_uq, w_ukv):
    wq = w_uq.reshape(MLA_Q_LORA, MLA_HEADS, MLA_NOPE + MLA_ROPE)
    wqn = wq[:, :, :MLA_NOPE].reshape(MLA_Q_LORA, MLA_HEADS * MLA_NOPE).astype(BF16)
    wqr = wq[:, :, MLA_NOPE:].reshape(MLA_Q_LORA, MLA_HEADS * MLA_ROPE).astype(BF16)
    wkv = w_ukv.reshape(MLA_KV_LORA, MLA_HEADS, MLA_NOPE + MLA_V)
    wk = wkv[:, :, :MLA_NOPE].reshape(MLA_KV_LORA, MLA_HEADS * MLA_NOPE).astype(BF16)
    wv = wkv[:, :, MLA_NOPE:].reshape(MLA_KV_LORA, MLA_W).astype(BF16)
    return wqn, wqr, wk, wv


def _rope_tables(t):
    pos = jnp.arange(t, dtype=F32)
    inv = 1.0 / (ROPE_THETA ** (jnp.arange(0, MLA_ROPE, 2, dtype=F32) / MLA_ROPE))
    ang = pos[:, None] * inv[None, :]
    cos, sin = jnp.cos(ang), jnp.sin(ang)
    return (jnp.concatenate([cos, cos, cos, cos], axis=-1),
            jnp.concatenate([-sin, sin, -sin, sin], axis=-1))


def _pick(t, pref):
    return pref if t % pref == 0 else t


ROW_TILE = 512
GLA_BLOCK = 1024
ATTN_Q_TILE = 512 * ATTN_SPLIT
ATTN_KV_TILE = 2048
MEM_KV_TN = 512
OUT_ROW_TILE = 256


def _trunk(x, mem, w):
    bsz, t, _ = x.shape
    m = bsz * t
    xf = x.reshape(m, D_MODEL)
    p_act = _norm_matmul(xf, w["g_in"], w["w_in_t"], _pick(m, ROW_TILE), P_TN,
                         w_transposed=True)

    tb = _pick(t, GLA_BLOCK)
    o_f = _gla_call(p_act, w["wg"][0], w["bg"][0], False, bsz, t, tb)
    o_a = _gla_call(p_act, w["wg"][1], w["bg"][1], True, bsz, t, tb, (o_f, w["g_head"]))

    cos_t, sin_t = _rope_tables(t)
    q, k, v = _mla_prep(p_act, w["g_q"], w["g_kv"], w["wqn"], w["wqr"], w["wk"], w["wv"],
                        cos_t, sin_t, t, _pick(t, ROW_TILE))
    o_b = _mla_attn(q, k, v, p_act, bsz, t, _pick(t, ATTN_Q_TILE), _pick(t, ATTN_KV_TILE))

    kv_mem = _norm_matmul(mem.reshape(bsz * N_MEM, D_MODEL), w["mem_g"], w["mem_w_kv"],
                          bsz * N_MEM, MEM_KV_TN)
    o_c = _mem_attn(p_act, kv_mem, bsz, t, _pick(t, ROW_TILE))

    y = _out_proj(o_a, o_b, o_c, w["w_out"], xf, w["g_final"], _pick(m, OUT_ROW_TILE))
    return y.reshape(bsz, t, D_MODEL)


def kernel(x_prompt, x_sample, mem_prompt, mem_sample, g_in, w_in, gla_w_g2, gla_b_g, gla_g_head,
           mla_g_q, mla_w_uq, mla_g_kv, mla_w_ukv, mem_g, mem_w_kv, w_out, g_final):
    wg, bg = _prep_gla_gate(gla_w_g2[0], gla_b_g[0])
    wqn, wqr, wk, wv = _prep_mla(mla_w_uq[0], mla_w_ukv[0])
    w = {
        "g_in": g_in[0][None, :],
        "w_in_t": _prep_w_in(w_in[0]),
        "wg": wg, "bg": bg,
        "g_head": gla_g_head[0][None, :],
        "g_q": mla_g_q[0][None, :], "g_kv": mla_g_kv[0][None, :],
        "wqn": wqn, "wqr": wqr, "wk": wk, "wv": wv,
        "mem_g": mem_g[0][None, :],
        "mem_w_kv": mem_w_kv[0].astype(BF16),
        "w_out": w_out[0].astype(BF16),
        "g_final": g_final[None, :],
    }
    return (_trunk(x_prompt, mem_prompt, w), _trunk(x_sample, mem_sample, w))
```

```python
import functools
import math

import jax
import jax.numpy as jnp
from jax import lax
from jax.experimental import pallas as pl
from jax.experimental.pallas import tpu as pltpu

F32 = jnp.float32
BF16 = jnp.bfloat16

D_MODEL = 4096
N_MEM = 256
GLA_HEADS = 12
GLA_DK = 64
GLA_DV = 128
GLA_LR = 16
GLA_TAU = 16.0
assert math.log2(GLA_TAU).is_integer()
GLA_CHUNK = 64
GLA_HK = GLA_HEADS * GLA_DK
GLA_W = GLA_HEADS * GLA_DV
MLA_HEADS = 12
MLA_Q_LORA = 1536
MLA_KV_LORA = 512
MLA_NOPE = 128
MLA_ROPE = 64
MLA_V = 128
ROPE_THETA = 10000.0
MLA_W = MLA_HEADS * MLA_V
MLA_QK_PAD = 256
MLA_V_PAD = 256
MEM_HEADS = 4
MEM_DH = 256
MEM_W = MEM_HEADS * MEM_DH
EPS = 1e-6
LANES = 128
LOG2E = 1.4426950408889634

COL_G_V = 0
COL_G_GATE = 1536
COL_M_CQ = 3072
COL_M_GATE = 4608
COL_G_Q = 6144
COL_G_K = 6912
COL_M_CKV = 7680
COL_C_Q = 8192
COL_C_GATE = 9216
COL_G_LR = 10240
COL_M_KR = 10368
P_USED = 10496
P_TN = 1536
P_COLS = 10752

VMEM_LIMIT = 60 * 1024 * 1024


def _cparams(n_axes):
    return pltpu.CompilerParams(
        dimension_semantics=("arbitrary",) * n_axes, vmem_limit_bytes=VMEM_LIMIT)


def _dot(a, b):
    return jnp.dot(a, b, preferred_element_type=F32)


def _dot_nt(a, b):
    return lax.dot_general(a, b, (((1,), (1,)), ((), ())), preferred_element_type=F32)


def _dot_tn(a, b):
    return lax.dot_general(a, b, (((0,), (0,)), ((), ())), preferred_element_type=F32)


def _silu(x):
    return x * jax.nn.sigmoid(x)


NORM_ROWS = 64
BF16_SUBLANES = 16


def _norm_matmul_kernel(w_transposed, chunk, x_ref, g_ref, w_ref, o_ref, h_ref):
    i = pl.program_id(0)
    j = pl.program_id(1)
    tm = x_ref.shape[0]
    dot = _dot_nt if w_transposed else _dot

    def normalise(slot, r0, rows):
        x = x_ref[pl.ds(r0, rows), :]
        ms = jnp.mean(x * x, axis=-1, keepdims=True)
        h_ref[slot, pl.ds(r0, rows), :] = (x * lax.rsqrt(ms + EPS) * g_ref[...]).astype(BF16)

    @pl.when(j == 0)
    def _():
        @pl.when(i == 0)
        def _():
            def rows(c, carry):
                normalise(0, pl.multiple_of(c * NORM_ROWS, NORM_ROWS), NORM_ROWS)
                return carry
            lax.fori_loop(0, tm // NORM_ROWS, rows, 0)

        o_ref[...] = dot(h_ref[i % 2], w_ref[...]).astype(o_ref.dtype)

    @pl.when(j > 0)
    def _():
        r0 = pl.multiple_of(jnp.minimum((j - 1) * chunk, tm - chunk), BF16_SUBLANES)
        o_ref[...] = dot(h_ref[i % 2], w_ref[...]).astype(o_ref.dtype)
        normalise((i + 1) % 2, r0, chunk)


def _norm_matmul(x, g, w, tm, tn, w_transposed=False):
    m, k = x.shape
    n = w.shape[0] if w_transposed else w.shape[1]
    ni, nj = m // tm, n // tn
    chunk = -(-tm // (nj - 1))
    chunk = -(-chunk // BF16_SUBLANES) * BF16_SUBLANES
    assert chunk * (nj - 1) >= tm and (tm - chunk) % BF16_SUBLANES == 0

    def x_map(i, j):
        return (jnp.where((i == 0) & (j == 0), 0, jnp.minimum(i + 1, ni - 1)), 0)

    if w_transposed:
        w_spec = pl.BlockSpec((tn, k), lambda i, j: (j, 0))
    else:
        w_spec = pl.BlockSpec((k, tn), lambda i, j: (0, j))
    return pl.pallas_call(
        functools.partial(_norm_matmul_kernel, w_transposed, chunk),
        grid=(ni, nj),
        in_specs=[
            pl.BlockSpec((tm, k), x_map),
            pl.BlockSpec((1, k), lambda i, j: (0, 0)),
            w_spec,
        ],
        out_specs=pl.BlockSpec((tm, tn), lambda i, j: (i, j)),
        out_shape=jax.ShapeDtypeStruct((m, n), BF16),
        scratch_shapes=[pltpu.VMEM((2, tm, k), BF16)],
        compiler_params=_cparams(2),
        name="norm_matmul",
    )(x, g, w)


def _gla_kernel(rev, nchunk, *refs):
    if rev:
        (q_ref, k_ref, v_ref, lr_ref, wg_ref, bg_ref, gate_ref, of_ref, gh_ref,
         o_ref, s_ref) = refs
    else:
        q_ref, k_ref, v_ref, lr_ref, wg_ref, bg_ref, o_ref, s_ref = refs
    C = GLA_CHUNK

    @pl.when(pl.program_id(1) == 0)
    def _():
        s_ref[...] = jnp.zeros_like(s_ref)

    row = lax.broadcasted_iota(jnp.int32, (C, C), 0)
    col = lax.broadcasted_iota(jnp.int32, (C, C), 1)
    tri = (col >= row) if rev else (col <= row)
    tri_bf = jnp.where(tri, 1.0 / GLA_TAU, 0.0).astype(BF16)
    row2 = lax.broadcasted_iota(jnp.int32, (2 * C, 2 * C), 0)
    col2 = lax.broadcasted_iota(jnp.int32, (2 * C, 2 * C), 1)
    same_head = (row2 < C) == (col2 < C)
    amask = same_head & ((col2 > row2) if rev else (col2 <= row2))
    q_lo = lax.broadcasted_iota(jnp.int32, (C, LANES), 1) < GLA_DK

    npair = GLA_HEADS // 2

    def gate_logits(c):
        rows = pl.ds(c * C, C)
        z = _dot(lr_ref[rows, :], wg_ref[...]) + bg_ref[...]
        lg = jnp.minimum(z, 0.0) - jnp.log(1.0 + jnp.exp(-jnp.abs(z)))
        hi = lg.astype(BF16)
        return hi, (lg - hi.astype(F32)).astype(BF16)

    def decayed_qk(c, hi, lo):
        rows = pl.ds(c * C, C)
        b = _dot(tri_bf, hi) + _dot(tri_bf, lo)
        bl = b[0:1, :] if rev else b[C - 1:C, :]
        q = q_ref[rows, :].astype(F32)
        k = k_ref[rows, :].astype(F32)
        q_in = (q * jnp.exp(b) * (GLA_DK ** -0.5)).astype(BF16)
        k_out = (k * jnp.exp(-b)).astype(BF16)
        k_last = (k * jnp.exp(bl - b)).astype(BF16)
        return q_in, k_out, k_last, jnp.exp(bl)

    def intra(c, prep):
        rows = pl.ds(c * C, C)
        q_in, k_out, k_last, _ = prep
        out = []
        for p in range(npair):
            sl = slice(p * LANES, (p + 1) * LANES)
            ql, ko, kl = q_in[:, sl], k_out[:, sl], k_last[:, sl]
            zero = jnp.zeros_like(ql)
            qs = jnp.concatenate([jnp.where(q_lo, ql, zero), jnp.where(q_lo, zero, ql)], axis=0)
            ks = jnp.concatenate([jnp.where(q_lo, kl, zero), jnp.where(q_lo, zero, kl)], axis=0)
            vs = jnp.concatenate([v_ref[rows, (2 * p + i) * GLA_DV:(2 * p + i + 1) * GLA_DV]
                                  for i in range(2)], axis=0)
            a_raw = _dot_nt(qs, jnp.concatenate([ko, ko], axis=0))
            out.append((qs, vs, a_raw, _dot_tn(ks, vs)))
        return out

    def outputs(c, prep, pairs):
        rows = pl.ds(c * C, C)
        dec = prep[3]
        for p, (qs, vs, a_raw, kv) in enumerate(pairs):
            a = jnp.where(amask, a_raw, 0.0).astype(BF16)
            st = s_ref[p]
            o2 = _dot(jnp.concatenate([a, qs], axis=1),
                      jnp.concatenate([vs, st.astype(BF16)], axis=0))
            dec_col = jnp.broadcast_to(dec[:, p * LANES:(p + 1) * LANES], (LANES, LANES)).T
            s_ref[p] = dec_col * st + kv
            for i in range(2):
                hs = slice((2 * p + i) * GLA_DV, (2 * p + i + 1) * GLA_DV)
                o = o2[i * C:(i + 1) * C]
                if rev:
                    tot = o + of_ref[rows, hs]
                    ms = jnp.mean(tot * tot, axis=-1, keepdims=True)
                    y = tot * lax.rsqrt(ms + EPS) * gh_ref[...]
                    o_ref[rows, hs] = (y * _silu(gate_ref[rows, hs].astype(F32))).astype(o_ref.dtype)
                else:
                    o_ref[rows, hs] = o

    order = [(nchunk - 1 - i) if rev else i for i in range(nchunk)]
    prep = decayed_qk(order[0], *gate_logits(order[0]))
    for i, c in enumerate(order):
        nxt = order[i + 1] if i + 1 < nchunk else None
        if nxt is not None:
            hi_lo = gate_logits(nxt)
        pairs = intra(c, prep)
        if nxt is not None:
            prep_next = decayed_qk(nxt, *hi_lo)
        outputs(c, prep, pairs)
        if nxt is not None:
            prep = prep_next


def _gla_call(p_act, wg, bg, rev, bsz, t, tb, extra=None):
    nblk = t // tb
    nchunk = tb // GLA_CHUNK

    def rowblk(b, i):
        return b * nblk + ((nblk - 1 - i) if rev else i)

    def act(width, colblk):
        return pl.BlockSpec((tb, width), lambda b, i: (rowblk(b, i), colblk))

    def full(shape):
        return pl.BlockSpec(shape, lambda b, i: (0,) * len(shape))

    in_specs = [
        act(GLA_HK, COL_G_Q // GLA_HK),
        act(GLA_HK, COL_G_K // GLA_HK),
        act(GLA_W, COL_G_V // GLA_W),
        act(LANES, COL_G_LR // LANES),
        full(wg.shape),
        full(bg.shape),
    ]
    args = [p_act, p_act, p_act, p_act, wg, bg]
    if rev:
        o_f, g_head = extra
        in_specs += [act(GLA_W, COL_G_GATE // GLA_W), act(GLA_W, 0), full(g_head.shape)]
        args += [p_act, o_f, g_head]
        out_dtype = BF16
    else:
        out_dtype = F32
    return pl.pallas_call(
        functools.partial(_gla_kernel, rev, nchunk),
        grid=(bsz, nblk),
        in_specs=in_specs,
        out_specs=act(GLA_W, 0),
        out_shape=jax.ShapeDtypeStruct((bsz * t, GLA_W), out_dtype),
        scratch_shapes=[pltpu.VMEM((GLA_HEADS // 2, GLA_DV, LANES), F32)],
        compiler_params=_cparams(2),
        name="gla_bwd" if rev else "gla_fwd",
    )(*args)


def _rope(r, cos_t, sin_t):
    half = MLA_ROPE // 2
    lane = lax.broadcasted_iota(jnp.int32, r.shape, 1)
    swapped = jnp.where(lane % MLA_ROPE < half,
                        pltpu.roll(r, LANES - half, 1), pltpu.roll(r, half, 1))
    return r * cos_t + swapped * sin_t


def _mla_prep_kernel(cq_ref, ckv_ref, kr_ref, gq_ref, gkv_ref, wqn_ref, wqr_ref, wk_ref, wv_ref,
                     cos_ref, sin_ref, q_ref, k_ref, v_ref):
    scale = (MLA_NOPE + MLA_ROPE) ** -0.5 * LOG2E
    cos_t = cos_ref[...]
    sin_t = sin_ref[...]

    cq = cq_ref[...].astype(F32)
    ms = jnp.mean(cq * cq, axis=-1, keepdims=True)
    qn = (cq * lax.rsqrt(ms + EPS) * gq_ref[...]).astype(BF16)
    ckv = ckv_ref[...].astype(F32)
    ms = jnp.mean(ckv * ckv, axis=-1, keepdims=True)
    kvn = (ckv * lax.rsqrt(ms + EPS) * gkv_ref[...]).astype(BF16)

    tm = qn.shape[0]
    low = lax.broadcasted_iota(jnp.int32, (tm, LANES), 1) < MLA_ROPE
    k_rope_even = _rope(kr_ref[...].astype(F32), cos_t, sin_t)
    k_rope = (k_rope_even.astype(BF16), pltpu.roll(k_rope_even, MLA_ROPE, 1).astype(BF16))
    k_nope = _dot(kvn, wk_ref[...]).astype(BF16)
    v = _dot(kvn, wv_ref[...]).astype(BF16)
    ones_col = jnp.where(
        lax.broadcasted_iota(jnp.int32, (tm, MLA_V_PAD - MLA_V), 1) == 0, 1.0, 0.0).astype(BF16)
    for h in range(MLA_HEADS):
        c0 = h * MLA_QK_PAD
        k_ref[:, c0:c0 + MLA_NOPE] = k_nope[:, h * MLA_NOPE:(h + 1) * MLA_NOPE]
        k_ref[:, c0 + MLA_NOPE:c0 + MLA_QK_PAD] = k_rope[h % 2]
        v0 = h * MLA_V_PAD
        v_ref[:, v0:v0 + MLA_V] = v[:, h * MLA_V:(h + 1) * MLA_V]
        v_ref[:, v0 + MLA_V:v0 + MLA_V_PAD] = ones_col
    for p in range(MLA_HEADS // 2):
        q2 = _dot(qn, wqn_ref[:, p * 2 * MLA_NOPE:(p + 1) * 2 * MLA_NOPE])
        for i in range(2):
            c0 = (2 * p + i) * MLA_QK_PAD
            q_ref[:, c0:c0 + MLA_NOPE] = (
                q2[:, i * MLA_NOPE:(i + 1) * MLA_NOPE] * scale).astype(BF16)
    for g in range(MLA_HEADS // 4):
        q4 = _dot(qn, wqr_ref[:, g * 4 * MLA_ROPE:(g + 1) * 4 * MLA_ROPE])
        for s in range(2):
            tile = _rope(q4[:, s * LANES:(s + 1) * LANES], cos_t, sin_t) * scale
            zero = jnp.zeros_like(tile)
            c0 = (4 * g + 2 * s) * MLA_QK_PAD + MLA_NOPE
            q_ref[:, c0:c0 + LANES] = jnp.where(low, tile, zero).astype(BF16)
            c1 = c0 + MLA_QK_PAD
            q_ref[:, c1:c1 + LANES] = jnp.where(low, zero, tile).astype(BF16)


def _mla_prep(p_act, g_q, g_kv, wqn, wqr, wk, wv, cos_t, sin_t, t, tm):
    m = p_act.shape[0]
    nt = t // tm

    def act(width, colblk):
        return pl.BlockSpec((tm, width), lambda i: (i, colblk))

    def full(shape):
        return pl.BlockSpec(shape, lambda i: (0,) * len(shape))

    table = pl.BlockSpec((tm, LANES), lambda i: (i % nt, 0))
    qk_w = MLA_HEADS * MLA_QK_PAD
    return pl.pallas_call(
        _mla_prep_kernel,
        grid=(m // tm,),
        in_specs=[
            act(MLA_Q_LORA, COL_M_CQ // MLA_Q_LORA),
            act(MLA_KV_LORA, COL_M_CKV // MLA_KV_LORA),
            act(LANES, COL_M_KR // LANES),
            full(g_q.shape), full(g_kv.shape), full(wqn.shape), full(wqr.shape),
            full(wk.shape), full(wv.shape),
            table, table,
        ],
        out_specs=[act(qk_w, 0), act(qk_w, 0), act(MLA_HEADS * MLA_V_PAD, 0)],
        out_shape=[
            jax.ShapeDtypeStruct((m, qk_w), BF16),
            jax.ShapeDtypeStruct((m, qk_w), BF16),
            jax.ShapeDtypeStruct((m, MLA_HEADS * MLA_V_PAD), BF16),
        ],
        compiler_params=_cparams(1),
        name="mla_prep",
    )(p_act, p_act, p_act, g_q, g_kv, wqn, wqr, wk, wv, cos_t, sin_t)


ATTN_SPLIT = 2


def _mla_attn_kernel(nkv, bk, q_ref, k_ref, v_ref, gate_ref, o_ref, s_ref):
    tq = q_ref.shape[0] // ATTN_SPLIT
    q = [q_ref[h * tq:(h + 1) * tq, :] for h in range(ATTN_SPLIT)]

    def scores(h, j):
        s_ref[2 * h + j % 2] = _dot_nt(q[h], k_ref[j * bk:(j + 1) * bk, :])

    m = [jnp.full((tq, 1), -jnp.inf, F32) for _ in range(ATTN_SPLIT)]
    acc = [jnp.zeros((tq, MLA_V_PAD), F32) for _ in range(ATTN_SPLIT)]
    for h in range(ATTN_SPLIT):
        scores(h, 0)
    for j in range(nkv):
        for h in range(ATTN_SPLIT):
            if j + 1 < nkv:
                scores(h, j + 1)
            s = s_ref[2 * h + j % 2]
            m_new = jnp.maximum(m[h], jnp.max(s, axis=-1, keepdims=True))
            alpha = jnp.exp2(m[h] - m_new)
            p = jnp.exp2(s - m_new).astype(BF16)
            acc[h] = alpha * acc[h] + _dot(p, v_ref[j * bk:(j + 1) * bk, :])
            m[h] = m_new
    for h in range(ATTN_SPLIT):
        rows = slice(h * tq, (h + 1) * tq)
        o = acc[h][:, :MLA_V] / acc[h][:, MLA_V:MLA_V + 1]
        o_ref[rows, :] = (o * _silu(gate_ref[rows, :].astype(F32))).astype(o_ref.dtype)


def _mla_attn(q, k, v, p_act, bsz, t, tq, bk):
    nq = t // tq
    nkv = t // bk
    gate_col0 = COL_M_GATE // MLA_V
    return pl.pallas_call(
        functools.partial(_mla_attn_kernel, nkv, bk),
        grid=(bsz, MLA_HEADS, nq),
        in_specs=[
            pl.BlockSpec((tq, MLA_QK_PAD), lambda b, h, i: (b * nq + i, h)),
            pl.BlockSpec((t, MLA_QK_PAD), lambda b, h, i: (b, h)),
            pl.BlockSpec((t, MLA_V_PAD), lambda b, h, i: (b, h)),
            pl.BlockSpec((tq, MLA_V), lambda b, h, i: (b * nq + i, gate_col0 + h)),
        ],
        out_specs=pl.BlockSpec((tq, MLA_V), lambda b, h, i: (b * nq + i, h)),
        out_shape=jax.ShapeDtypeStruct((bsz * t, MLA_W), BF16),
        scratch_shapes=[pltpu.VMEM((2 * ATTN_SPLIT, tq // ATTN_SPLIT, bk), F32)],
        compiler_params=_cparams(3),
        name="mla_attn",
    )(q, k, v, p_act)


def _mem_attn_kernel(q_ref, gate_ref, kv_ref, o_ref):
    heads = [slice(h * MEM_DH, (h + 1) * MEM_DH) for h in range(MEM_HEADS)]
    scores = [_dot_nt(q_ref[:, hs], kv_ref[:, hs]) for hs in heads]
    for h, hs in enumerate(heads):
        s = scores[h] * (MEM_DH ** -0.5)
        p = jnp.exp(s - jnp.max(s, axis=-1, keepdims=True))
        l = jnp.sum(p, axis=-1, keepdims=True)
        o = _dot(p.astype(BF16), kv_ref[:, MEM_W + h * MEM_DH:MEM_W + (h + 1) * MEM_DH]) / l
        o_ref[:, hs] = (o * _silu(gate_ref[:, hs].astype(F32))).astype(o_ref.dtype)


def _mem_attn(p_act, kv_mem, bsz, t, tq):
    nq = t // tq
    return pl.pallas_call(
        _mem_attn_kernel,
        grid=(bsz, nq),
        in_specs=[
            pl.BlockSpec((tq, MEM_W), lambda b, i: (b * nq + i, COL_C_Q // MEM_W)),
            pl.BlockSpec((tq, MEM_W), lambda b, i: (b * nq + i, COL_C_GATE // MEM_W)),
            pl.BlockSpec((N_MEM, 2 * MEM_W), lambda b, i: (b, 0)),
        ],
        out_specs=pl.BlockSpec((tq, MEM_W), lambda b, i: (b * nq + i, 0)),
        out_shape=jax.ShapeDtypeStruct((bsz * t, MEM_W), BF16),
        compiler_params=_cparams(2),
        name="mem_attn",
    )(p_act, p_act, kv_mem)


OUT_TN = 1024


def _out_kernel(oa_ref, ob_ref, oc_ref, w_ref, x_ref, g_ref, y_ref):
    ssq = jnp.zeros((y_ref.shape[0], 1), F32)
    for c0 in range(0, D_MODEL, OUT_TN):
        cs = slice(c0, c0 + OUT_TN)
        acc = (_dot(oa_ref[...], w_ref[0:GLA_W, cs])
               + _dot(ob_ref[...], w_ref[GLA_W:GLA_W + MLA_W, cs])
               + _dot(oc_ref[...], w_ref[GLA_W + MLA_W:, cs]) + x_ref[:, cs])
        ssq = ssq + jnp.sum(acc * acc, axis=-1, keepdims=True)
        y_ref[:, cs] = acc
    r = lax.rsqrt(ssq * (1.0 / D_MODEL) + EPS)
    for c0 in range(0, D_MODEL, OUT_TN):
        cs = slice(c0, c0 + OUT_TN)
        y_ref[:, cs] = y_ref[:, cs] * r * g_ref[:, cs]


def _out_proj(o_a, o_b, o_c, w_out, x, g_final, tm):
    m = x.shape[0]
    return pl.pallas_call(
        _out_kernel,
        grid=(m // tm,),
        in_specs=[
            pl.BlockSpec((tm, GLA_W), lambda i: (i, 0)),
            pl.BlockSpec((tm, MLA_W), lambda i: (i, 0)),
            pl.BlockSpec((tm, MEM_W), lambda i: (i, 0)),
            pl.BlockSpec((D_MODEL, D_MODEL), lambda i: (0, 0), pipeline_mode=pl.Buffered(1)),
            pl.BlockSpec((tm, D_MODEL), lambda i: (i, 0)),
            pl.BlockSpec((1, D_MODEL), lambda i: (0, 0)),
        ],
        out_specs=pl.BlockSpec((tm, D_MODEL), lambda i: (i, 0)),
        out_shape=jax.ShapeDtypeStruct((m, D_MODEL), F32),
        compiler_params=_cparams(1),
        name="out_proj",
    )(o_a, o_b, o_c, w_out, x, g_final)


IN_WIDTHS = (GLA_HK, GLA_HK, GLA_W, 2 * GLA_LR, GLA_W, MLA_Q_LORA, MLA_KV_LORA, MLA_ROPE,
             MLA_W, MEM_W, MEM_W)
IN_DST = (COL_G_Q, COL_G_K, COL_G_V, COL_G_LR, COL_G_GATE, COL_M_CQ, COL_M_CKV, COL_M_KR,
          COL_M_GATE, COL_C_Q, COL_C_GATE)
N_IN = sum(IN_WIDTHS)
W_IN_COLS = 256


def _w_in_relayout_kernel(w_ref, o_ref):
    o_ref[COL_G_LR:, :] = jnp.zeros((P_COLS - COL_G_LR, o_ref.shape[1]), o_ref.dtype)
    src = 0
    for width, dst in zip(IN_WIDTHS, IN_DST):
        o_ref[dst:dst + width, :] = w_ref[src:src + width, :].astype(o_ref.dtype)
        src += width


def _prep_w_in(w_in):
    k = w_in.shape[0]
    return pl.pallas_call(
        _w_in_relayout_kernel,
        grid=(k // W_IN_COLS,),
        in_specs=[pl.BlockSpec((N_IN, W_IN_COLS), lambda i: (0, i))],
        out_specs=pl.BlockSpec((P_COLS, W_IN_COLS), lambda i: (0, i)),
        out_shape=jax.ShapeDtypeStruct((P_COLS, k), BF16),
        compiler_params=_cparams(1),
        name="w_in_relayout",
    )(w_in.T)


def _prep_gla_gate(w_g2, b_g):
    wgs = []
    for d in range(2):
        w = jnp.zeros((LANES, GLA_HK), F32).at[d * GLA_LR:(d + 1) * GLA_LR].set(w_g2[d])
        wgs.append(w.astype(BF16))
    return wgs, [b_g[0][None, :], b_g[1][None, :]]


def _prep_mla(w_uq, w_ukv):
    wq = w_uq.reshape(MLA_Q_LORA, MLA_HEADS, MLA_NOPE + MLA_ROPE)
    wqn = wq[:, :, :MLA_NOPE].reshape(MLA_Q_LORA, MLA_HEADS * MLA_NOPE).astype(BF16)
    wqr = wq[:, :, MLA_NOPE:].reshape(MLA_Q_LORA, MLA_HEADS * MLA_ROPE).astype(BF16)
    wkv = w_ukv.reshape(MLA_KV_LORA, MLA_HEADS, MLA_NOPE + MLA_V)
    wk = wkv[:, :, :MLA_NOPE].reshape(MLA_KV_LORA, MLA_HEADS * MLA_NOPE).astype(BF16)
    wv = wkv[:, :, MLA_NOPE:].reshape(MLA_KV_LORA, MLA_W).astype(BF16)
    return wqn, wqr, wk, wv


def _rope_tables(t):
    pos = jnp.arange(t, dtype=F32)
    inv = 1.0 / (ROPE_THETA ** (jnp.arange(0, MLA_ROPE, 2, dtype=F32) / MLA_ROPE))
    ang = pos[:, None] * inv[None, :]
    cos, sin = jnp.cos(ang), jnp.sin(ang)
    return (jnp.concatenate([cos, cos, cos, cos], axis=-1),
            jnp.concatenate([-sin, sin, -sin, sin], axis=-1))


def _pick(t, pref):
    return pref if t % pref == 0 else t


ROW_TILE = 512
GLA_BLOCK = 1024
ATTN_Q_TILE = 512 * ATTN_SPLIT
ATTN_KV_TILE = 2048
ATTN_KV_SINGLE = 4096
MEM_KV_TN = 512
OUT_ROW_TILE = 256


def _trunk(x, mem, w):
    bsz, t, _ = x.shape
    m = bsz * t
    xf = x.reshape(m, D_MODEL)
    p_act = _norm_matmul(xf, w["g_in"], w["w_in_t"], _pick(m, ROW_TILE), P_TN,
                         w_transposed=True)

    tb = _pick(t, GLA_BLOCK)
    o_f = _gla_call(p_act, w["wg"][0], w["bg"][0], False, bsz, t, tb)
    o_a = _gla_call(p_act, w["wg"][1], w["bg"][1], True, bsz, t, tb, (o_f, w["g_head"]))

    cos_t, sin_t = _rope_tables(t)
    q, k, v = _mla_prep(p_act, w["g_q"], w["g_kv"], w["wqn"], w["wqr"], w["wk"], w["wv"],
                        cos_t, sin_t, t, _pick(t, ROW_TILE))
    bk = t if t <= ATTN_KV_SINGLE else _pick(t, ATTN_KV_TILE)
    o_b = _mla_attn(q, k, v, p_act, bsz, t, _pick(t, ATTN_Q_TILE), bk)

    kv_mem = _norm_matmul(mem.reshape(bsz * N_MEM, D_MODEL), w["mem_g"], w["mem_w_kv"],
                          bsz * N_MEM, MEM_KV_TN)
    o_c = _mem_attn(p_act, kv_mem, bsz, t, _pick(t, ROW_TILE))

    y = _out_proj(o_a, o_b, o_c, w["w_out"], xf, w["g_final"], _pick(m, OUT_ROW_TILE))
    return y.reshape(bsz, t, D_MODEL)


def kernel(x_prompt, x_sample, mem_prompt, mem_sample, g_in, w_in, gla_w_g2, gla_b_g, gla_g_head,
           mla_g_q, mla_w_uq, mla_g_kv, mla_w_ukv, mem_g, mem_w_kv, w_out, g_final):
    wg, bg = _prep_gla_gate(gla_w_g2[0], gla_b_g[0])
    wqn, wqr, wk, wv = _prep_mla(mla_w_uq[0], mla_w_ukv[0])
    w = {
        "g_in": g_in[0][None, :],
        "w_in_t": _prep_w_in(w_in[0]),
        "wg": wg, "bg": bg,
        "g_head": gla_g_head[0][None, :],
        "g_q": mla_g_q[0][None, :], "g_kv": mla_g_kv[0][None, :],
        "wqn": wqn, "wqr": wqr, "wk": wk, "wv": wv,
        "mem_g": mem_g[0][None, :],
        "mem_w_kv": mem_w_kv[0].astype(BF16),
        "w_out": w_out[0].astype(BF16),
        "g_final": g_final[None, :],
    }
    return (_trunk(x_prompt, mem_prompt, w), _trunk(x_sample, mem_sample, w))
```

```python
import functools
import math

import jax
import jax.numpy as jnp
from jax import lax
from jax.experimental import pallas as pl
from jax.experimental.pallas import tpu as pltpu

F32 = jnp.float32
BF16 = jnp.bfloat16

D_MODEL = 4096
N_MEM = 256
GLA_HEADS = 12
GLA_DK = 64
GLA_DV = 128
GLA_LR = 16
GLA_TAU = 16.0
assert math.log2(GLA_TAU).is_integer()
GLA_CHUNK = 64
GLA_HK = GLA_HEADS * GLA_DK
GLA_W = GLA_HEADS * GLA_DV
MLA_HEADS = 12
MLA_Q_LORA = 1536
MLA_KV_LORA = 512
MLA_NOPE = 128
MLA_ROPE = 64
MLA_V = 128
ROPE_THETA = 10000.0
MLA_W = MLA_HEADS * MLA_V
MLA_QK_PAD = 256
MLA_V_PAD = 256
MEM_HEADS = 4
MEM_DH = 256
MEM_W = MEM_HEADS * MEM_DH
EPS = 1e-6
LANES = 128
LOG2E = 1.4426950408889634

COL_G_V = 0
COL_G_GATE = 1536
COL_M_CQ = 3072
COL_M_GATE = 4608
COL_G_Q = 6144
COL_G_K = 6912
COL_M_CKV = 7680
COL_C_Q = 8192
COL_C_GATE = 9216
COL_G_LR = 10240
COL_M_KR = 10368
P_USED = 10496
P_TN = 1536
P_COLS = 10752

VMEM_LIMIT = 60 * 1024 * 1024


def _cparams(n_axes):
    return pltpu.CompilerParams(
        dimension_semantics=("arbitrary",) * n_axes, vmem_limit_bytes=VMEM_LIMIT)


def _dot(a, b):
    return jnp.dot(a, b, preferred_element_type=F32)


def _dot_nt(a, b):
    return lax.dot_general(a, b, (((1,), (1,)), ((), ())), preferred_element_type=F32)


def _dot_tn(a, b):
    return lax.dot_general(a, b, (((0,), (0,)), ((), ())), preferred_element_type=F32)


def _silu(x):
    return x * jax.nn.sigmoid(x)


NORM_ROWS = 64
BF16_SUBLANES = 16


def _norm_matmul_kernel(w_transposed, chunk, x_ref, g_ref, w_ref, o_ref, h_ref):
    i = pl.program_id(0)
    j = pl.program_id(1)
    tm = x_ref.shape[0]
    dot = _dot_nt if w_transposed else _dot

    def normalise(slot, r0, rows):
        x = x_ref[pl.ds(r0, rows), :]
        ms = jnp.mean(x * x, axis=-1, keepdims=True)
        h_ref[slot, pl.ds(r0, rows), :] = (x * lax.rsqrt(ms + EPS) * g_ref[...]).astype(BF16)

    @pl.when(j == 0)
    def _():
        @pl.when(i == 0)
        def _():
            def rows(c, carry):
                normalise(0, pl.multiple_of(c * NORM_ROWS, NORM_ROWS), NORM_ROWS)
                return carry
            lax.fori_loop(0, tm // NORM_ROWS, rows, 0)

        o_ref[...] = dot(h_ref[i % 2], w_ref[...]).astype(o_ref.dtype)

    @pl.when(j > 0)
    def _():
        r0 = pl.multiple_of(jnp.minimum((j - 1) * chunk, tm - chunk), BF16_SUBLANES)
        o_ref[...] = dot(h_ref[i % 2], w_ref[...]).astype(o_ref.dtype)
        normalise((i + 1) % 2, r0, chunk)


def _norm_matmul(x, g, w, tm, tn, w_transposed=False):
    m, k = x.shape
    n = w.shape[0] if w_transposed else w.shape[1]
    ni, nj = m // tm, n // tn
    chunk = -(-tm // (nj - 1))
    chunk = -(-chunk // BF16_SUBLANES) * BF16_SUBLANES
    assert chunk * (nj - 1) >= tm and (tm - chunk) % BF16_SUBLANES == 0

    def x_map(i, j):
        return (jnp.where((i == 0) & (j == 0), 0, jnp.minimum(i + 1, ni - 1)), 0)

    if w_transposed:
        w_spec = pl.BlockSpec((tn, k), lambda i, j: (j, 0))
    else:
        w_spec = pl.BlockSpec((k, tn), lambda i, j: (0, j))
    return pl.pallas_call(
        functools.partial(_norm_matmul_kernel, w_transposed, chunk),
        grid=(ni, nj),
        in_specs=[
            pl.BlockSpec((tm, k), x_map),
            pl.BlockSpec((1, k), lambda i, j: (0, 0)),
            w_spec,
        ],
        out_specs=pl.BlockSpec((tm, tn), lambda i, j: (i, j)),
        out_shape=jax.ShapeDtypeStruct((m, n), BF16),
        scratch_shapes=[pltpu.VMEM((2, tm, k), BF16)],
        compiler_params=_cparams(2),
        name="norm_matmul",
    )(x, g, w)


def _gla_kernel(rev, nchunk, *refs):
    if rev:
        (q_ref, k_ref, v_ref, lr_ref, wg_ref, bg_ref, gate_ref, of_ref, gh_ref,
         o_ref, s_ref) = refs
    else:
        q_ref, k_ref, v_ref, lr_ref, wg_ref, bg_ref, o_ref, s_ref = refs
    C = GLA_CHUNK

    @pl.when(pl.program_id(1) == 0)
    def _():
        s_ref[...] = jnp.zeros_like(s_ref)

    row = lax.broadcasted_iota(jnp.int32, (C, C), 0)
    col = lax.broadcasted_iota(jnp.int32, (C, C), 1)
    tri = (col >= row) if rev else (col <= row)
    tri_bf = jnp.where(tri, 1.0 / GLA_TAU, 0.0).astype(BF16)
    row2 = lax.broadcasted_iota(jnp.int32, (2 * C, 2 * C), 0)
    col2 = lax.broadcasted_iota(jnp.int32, (2 * C, 2 * C), 1)
    same_head = (row2 < C) == (col2 < C)
    amask = same_head & ((col2 > row2) if rev else (col2 <= row2))
    q_lo = lax.broadcasted_iota(jnp.int32, (C, LANES), 1) < GLA_DK

    npair = GLA_HEADS // 2

    def gate_logits(c):
        rows = pl.ds(c * C, C)
        z = _dot(lr_ref[rows, :], wg_ref[...]) + bg_ref[...]
        lg = jnp.minimum(z, 0.0) - jnp.log(1.0 + jnp.exp(-jnp.abs(z)))
        hi = lg.astype(BF16)
        return hi, (lg - hi.astype(F32)).astype(BF16)

    def decayed_qk(c, hi, lo):
        rows = pl.ds(c * C, C)
        b = _dot(tri_bf, hi) + _dot(tri_bf, lo)
        bl = b[0:1, :] if rev else b[C - 1:C, :]
        q = q_ref[rows, :].astype(F32)
        k = k_ref[rows, :].astype(F32)
        q_in = (q * jnp.exp(b) * (GLA_DK ** -0.5)).astype(BF16)
        k_out = (k * jnp.exp(-b)).astype(BF16)
        k_last = (k * jnp.exp(bl - b)).astype(BF16)
        return q_in, k_out, k_last, jnp.exp(bl)

    def intra(c, prep):
        rows = pl.ds(c * C, C)
        q_in, k_out, k_last, _ = prep
        out = []
        for p in range(npair):
            sl = slice(p * LANES, (p + 1) * LANES)
            ql, ko, kl = q_in[:, sl], k_out[:, sl], k_last[:, sl]
            zero = jnp.zeros_like(ql)
            qs = jnp.concatenate([jnp.where(q_lo, ql, zero), jnp.where(q_lo, zero, ql)], axis=0)
            ks = jnp.concatenate([jnp.where(q_lo, kl, zero), jnp.where(q_lo, zero, kl)], axis=0)
            vs = jnp.concatenate([v_ref[rows, (2 * p + i) * GLA_DV:(2 * p + i + 1) * GLA_DV]
                                  for i in range(2)], axis=0)
            a_raw = _dot_nt(qs, jnp.concatenate([ko, ko], axis=0))
            out.append((qs, vs, a_raw, _dot_tn(ks, vs)))
        return out

    def outputs(c, prep, pairs):
        rows = pl.ds(c * C, C)
        dec = prep[3]
        for p, (qs, vs, a_raw, kv) in enumerate(pairs):
            a = jnp.where(amask, a_raw, 0.0).astype(BF16)
            st = s_ref[p]
            o2 = _dot(jnp.concatenate([a, qs], axis=1),
                      jnp.concatenate([vs, st.astype(BF16)], axis=0))
            dec_col = jnp.broadcast_to(dec[:, p * LANES:(p + 1) * LANES], (LANES, LANES)).T
            s_ref[p] = dec_col * st + kv
            for i in range(2):
                hs = slice((2 * p + i) * GLA_DV, (2 * p + i + 1) * GLA_DV)
                o = o2[i * C:(i + 1) * C]
                if rev:
                    tot = o + of_ref[rows, hs]
                    ms = jnp.mean(tot * tot, axis=-1, keepdims=True)
                    y = tot * lax.rsqrt(ms + EPS) * gh_ref[...]
                    o_ref[rows, hs] = (y * _silu(gate_ref[rows, hs].astype(F32))).astype(o_ref.dtype)
                else:
                    o_ref[rows, hs] = o

    order = [(nchunk - 1 - i) if rev else i for i in range(nchunk)]
    prep = decayed_qk(order[0], *gate_logits(order[0]))
    for i, c in enumerate(order):
        nxt = order[i + 1] if i + 1 < nchunk else None
        if nxt is not None:
            hi_lo = gate_logits(nxt)
        pairs = intra(c, prep)
        if nxt is not None:
            prep_next = decayed_qk(nxt, *hi_lo)
        outputs(c, prep, pairs)
        if nxt is not None:
            prep = prep_next


def _gla_call(p_act, wg, bg, rev, bsz, t, tb, extra=None):
    nblk = t // tb
    nchunk = tb // GLA_CHUNK

    def rowblk(b, i):
        return b * nblk + ((nblk - 1 - i) if rev else i)

    def act(width, colblk):
        return pl.BlockSpec((tb, width), lambda b, i: (rowblk(b, i), colblk))

    def full(shape):
        return pl.BlockSpec(shape, lambda b, i: (0,) * len(shape))

    in_specs = [
        act(GLA_HK, COL_G_Q // GLA_HK),
        act(GLA_HK, COL_G_K // GLA_HK),
        act(GLA_W, COL_G_V // GLA_W),
        act(LANES, COL_G_LR // LANES),
        full(wg.shape),
        full(bg.shape),
    ]
    args = [p_act, p_act, p_act, p_act, wg, bg]
    if rev:
        o_f, g_head = extra
        in_specs += [act(GLA_W, COL_G_GATE // GLA_W), act(GLA_W, 0), full(g_head.shape)]
        args += [p_act, o_f, g_head]
        out_dtype = BF16
    else:
        out_dtype = F32
    return pl.pallas_call(
        functools.partial(_gla_kernel, rev, nchunk),
        grid=(bsz, nblk),
        in_specs=in_specs,
        out_specs=act(GLA_W, 0),
        out_shape=jax.ShapeDtypeStruct((bsz * t, GLA_W), out_dtype),
        scratch_shapes=[pltpu.VMEM((GLA_HEADS // 2, GLA_DV, LANES), F32)],
        compiler_params=_cparams(2),
        name="gla_bwd" if rev else "gla_fwd",
    )(*args)


def _rope(r, cos_t, sin_t):
    half = MLA_ROPE // 2
    lane = lax.broadcasted_iota(jnp.int32, r.shape, 1)
    swapped = jnp.where(lane % MLA_ROPE < half,
                        pltpu.roll(r, LANES - half, 1), pltpu.roll(r, half, 1))
    return r * cos_t + swapped * sin_t


def _mla_prep_kernel(cq_ref, ckv_ref, kr_ref, mq_ref, mgate_ref, kvm_ref, gq_ref, gkv_ref,
                     wqn_ref, wqr_ref, wk_ref, wv_ref, cos_ref, sin_ref,
                     q_ref, k_ref, v_ref, oc_ref):
    mem_scores = _mem_scores(mq_ref, kvm_ref)
    scale = (MLA_NOPE + MLA_ROPE) ** -0.5 * LOG2E
    cos_t = cos_ref[...]
    sin_t = sin_ref[...]

    cq = cq_ref[...].astype(F32)
    ms = jnp.mean(cq * cq, axis=-1, keepdims=True)
    qn = (cq * lax.rsqrt(ms + EPS) * gq_ref[...]).astype(BF16)
    ckv = ckv_ref[...].astype(F32)
    ms = jnp.mean(ckv * ckv, axis=-1, keepdims=True)
    kvn = (ckv * lax.rsqrt(ms + EPS) * gkv_ref[...]).astype(BF16)

    tm = qn.shape[0]
    low = lax.broadcasted_iota(jnp.int32, (tm, LANES), 1) < MLA_ROPE
    k_rope_even = _rope(kr_ref[...].astype(F32), cos_t, sin_t)
    k_rope = (k_rope_even.astype(BF16), pltpu.roll(k_rope_even, MLA_ROPE, 1).astype(BF16))
    k_nope = _dot(kvn, wk_ref[...]).astype(BF16)
    v = _dot(kvn, wv_ref[...]).astype(BF16)
    ones_col = jnp.where(
        lax.broadcasted_iota(jnp.int32, (tm, MLA_V_PAD - MLA_V), 1) == 0, 1.0, 0.0).astype(BF16)
    for h in range(MLA_HEADS):
        c0 = h * MLA_QK_PAD
        k_ref[:, c0:c0 + MLA_NOPE] = k_nope[:, h * MLA_NOPE:(h + 1) * MLA_NOPE]
        k_ref[:, c0 + MLA_NOPE:c0 + MLA_QK_PAD] = k_rope[h % 2]
        v0 = h * MLA_V_PAD
        v_ref[:, v0:v0 + MLA_V] = v[:, h * MLA_V:(h + 1) * MLA_V]
        v_ref[:, v0 + MLA_V:v0 + MLA_V_PAD] = ones_col
    for p in range(MLA_HEADS // 2):
        q2 = _dot(qn, wqn_ref[:, p * 2 * MLA_NOPE:(p + 1) * 2 * MLA_NOPE])
        for i in range(2):
            c0 = (2 * p + i) * MLA_QK_PAD
            q_ref[:, c0:c0 + MLA_NOPE] = (
                q2[:, i * MLA_NOPE:(i + 1) * MLA_NOPE] * scale).astype(BF16)
    for g in range(MLA_HEADS // 4):
        q4 = _dot(qn, wqr_ref[:, g * 4 * MLA_ROPE:(g + 1) * 4 * MLA_ROPE])
        for s in range(2):
            tile = _rope(q4[:, s * LANES:(s + 1) * LANES], cos_t, sin_t) * scale
            zero = jnp.zeros_like(tile)
            c0 = (4 * g + 2 * s) * MLA_QK_PAD + MLA_NOPE
            q_ref[:, c0:c0 + LANES] = jnp.where(low, tile, zero).astype(BF16)
            c1 = c0 + MLA_QK_PAD
            q_ref[:, c1:c1 + LANES] = jnp.where(low, zero, tile).astype(BF16)
    _mem_outputs(mem_scores, mgate_ref, kvm_ref, oc_ref)


def _mla_prep(p_act, kv_mem, g_q, g_kv, wqn, wqr, wk, wv, cos_t, sin_t, t, tm):
    m = p_act.shape[0]
    nt = t // tm

    def act(width, colblk):
        return pl.BlockSpec((tm, width), lambda i: (i, colblk))

    def full(shape):
        return pl.BlockSpec(shape, lambda i: (0,) * len(shape))

    table = pl.BlockSpec((tm, LANES), lambda i: (i % nt, 0))
    qk_w = MLA_HEADS * MLA_QK_PAD
    return pl.pallas_call(
        _mla_prep_kernel,
        grid=(m // tm,),
        in_specs=[
            act(MLA_Q_LORA, COL_M_CQ // MLA_Q_LORA),
            act(MLA_KV_LORA, COL_M_CKV // MLA_KV_LORA),
            act(LANES, COL_M_KR // LANES),
            act(MEM_W, COL_C_Q // MEM_W),
            act(MEM_W, COL_C_GATE // MEM_W),
            pl.BlockSpec((N_MEM, 2 * MEM_W), lambda i: (i // nt, 0)),
            full(g_q.shape), full(g_kv.shape), full(wqn.shape), full(wqr.shape),
            full(wk.shape), full(wv.shape),
            table, table,
        ],
        out_specs=[act(qk_w, 0), act(qk_w, 0), act(MLA_HEADS * MLA_V_PAD, 0), act(MEM_W, 0)],
        out_shape=[
            jax.ShapeDtypeStruct((m, qk_w), BF16),
            jax.ShapeDtypeStruct((m, qk_w), BF16),
            jax.ShapeDtypeStruct((m, MLA_HEADS * MLA_V_PAD), BF16),
            jax.ShapeDtypeStruct((m, MEM_W), BF16),
        ],
        compiler_params=_cparams(1),
        name="mla_prep",
    )(p_act, p_act, p_act, p_act, p_act, kv_mem, g_q, g_kv, wqn, wqr, wk, wv, cos_t, sin_t)


ATTN_SPLIT = 2


def _mla_attn_kernel(nkv, bk, q_ref, k_ref, v_ref, gate_ref, o_ref, s_ref):
    tq = q_ref.shape[0] // ATTN_SPLIT
    q = [q_ref[h * tq:(h + 1) * tq, :] for h in range(ATTN_SPLIT)]

    def scores(h, j):
        s_ref[2 * h + j % 2] = _dot_nt(q[h], k_ref[j * bk:(j + 1) * bk, :])

    m = [jnp.full((tq, 1), -jnp.inf, F32) for _ in range(ATTN_SPLIT)]
    acc = [jnp.zeros((tq, MLA_V_PAD), F32) for _ in range(ATTN_SPLIT)]
    for h in range(ATTN_SPLIT):
        scores(h, 0)
    for j in range(nkv):
        for h in range(ATTN_SPLIT):
            if j + 1 < nkv:
                scores(h, j + 1)
            s = s_ref[2 * h + j % 2]
            m_new = jnp.maximum(m[h], jnp.max(s, axis=-1, keepdims=True))
            alpha = jnp.exp2(m[h] - m_new)
            p = jnp.exp2(s - m_new).astype(BF16)
            acc[h] = alpha * acc[h] + _dot(p, v_ref[j * bk:(j + 1) * bk, :])
            m[h] = m_new
    for h in range(ATTN_SPLIT):
        rows = slice(h * tq, (h + 1) * tq)
        o = acc[h][:, :MLA_V] / acc[h][:, MLA_V:MLA_V + 1]
        o_ref[rows, :] = (o * _silu(gate_ref[rows, :].astype(F32))).astype(o_ref.dtype)


def _mla_attn(q, k, v, p_act, bsz, t, tq, bk):
    nq = t // tq
    nkv = t // bk
    gate_col0 = COL_M_GATE // MLA_V
    return pl.pallas_call(
        functools.partial(_mla_attn_kernel, nkv, bk),
        grid=(bsz, MLA_HEADS, nq),
        in_specs=[
            pl.BlockSpec((tq, MLA_QK_PAD), lambda b, h, i: (b * nq + i, h)),
            pl.BlockSpec((t, MLA_QK_PAD), lambda b, h, i: (b, h)),
            pl.BlockSpec((t, MLA_V_PAD), lambda b, h, i: (b, h)),
            pl.BlockSpec((tq, MLA_V), lambda b, h, i: (b * nq + i, gate_col0 + h)),
        ],
        out_specs=pl.BlockSpec((tq, MLA_V), lambda b, h, i: (b * nq + i, h)),
        out_shape=jax.ShapeDtypeStruct((bsz * t, MLA_W), BF16),
        scratch_shapes=[pltpu.VMEM((2 * ATTN_SPLIT, tq // ATTN_SPLIT, bk), F32)],
        compiler_params=_cparams(3),
        name="mla_attn",
    )(q, k, v, p_act)


_MEM_HEAD_COLS = [slice(h * MEM_DH, (h + 1) * MEM_DH) for h in range(MEM_HEADS)]


def _mem_scores(q_ref, kv_ref):
    return [_dot_nt(q_ref[:, hs], kv_ref[:, hs]) for hs in _MEM_HEAD_COLS]


def _mem_outputs(scores, gate_ref, kv_ref, o_ref):
    for h, hs in enumerate(_MEM_HEAD_COLS):
        s = scores[h] * (MEM_DH ** -0.5)
        p = jnp.exp(s - jnp.max(s, axis=-1, keepdims=True))
        l = jnp.sum(p, axis=-1, keepdims=True)
        o = _dot(p.astype(BF16), kv_ref[:, MEM_W + h * MEM_DH:MEM_W + (h + 1) * MEM_DH]) / l
        o_ref[:, hs] = (o * _silu(gate_ref[:, hs].astype(F32))).astype(o_ref.dtype)


OUT_TN = 1024


def _out_kernel(oa_ref, ob_ref, oc_ref, w_ref, x_ref, g_ref, y_ref):
    ssq = jnp.zeros((y_ref.shape[0], 1), F32)
    for c0 in range(0, D_MODEL, OUT_TN):
        cs = slice(c0, c0 + OUT_TN)
        acc = (_dot(oa_ref[...], w_ref[0:GLA_W, cs])
               + _dot(ob_ref[...], w_ref[GLA_W:GLA_W + MLA_W, cs])
               + _dot(oc_ref[...], w_ref[GLA_W + MLA_W:, cs]) + x_ref[:, cs])
        ssq = ssq + jnp.sum(acc * acc, axis=-1, keepdims=True)
        y_ref[:, cs] = acc
    r = lax.rsqrt(ssq * (1.0 / D_MODEL) + EPS)
    for c0 in range(0, D_MODEL, OUT_TN):
        cs = slice(c0, c0 + OUT_TN)
        y_ref[:, cs] = y_ref[:, cs] * r * g_ref[:, cs]


def _out_proj(o_a, o_b, o_c, w_out, x, g_final, tm):
    m = x.shape[0]
    return pl.pallas_call(
        _out_kernel,
        grid=(m // tm,),
        in_specs=[
            pl.BlockSpec((tm, GLA_W), lambda i: (i, 0)),
            pl.BlockSpec((tm, MLA_W), lambda i: (i, 0)),
            pl.BlockSpec((tm, MEM_W), lambda i: (i, 0)),
            pl.BlockSpec((D_MODEL, D_MODEL), lambda i: (0, 0), pipeline_mode=pl.Buffered(1)),
            pl.BlockSpec((tm, D_MODEL), lambda i: (i, 0)),
            pl.BlockSpec((1, D_MODEL), lambda i: (0, 0)),
        ],
        out_specs=pl.BlockSpec((tm, D_MODEL), lambda i: (i, 0)),
        out_shape=jax.ShapeDtypeStruct((m, D_MODEL), F32),
        compiler_params=_cparams(1),
        name="out_proj",
    )(o_a, o_b, o_c, w_out, x, g_final)


IN_WIDTHS = (GLA_HK, GLA_HK, GLA_W, 2 * GLA_LR, GLA_W, MLA_Q_LORA, MLA_KV_LORA, MLA_ROPE,
             MLA_W, MEM_W, MEM_W)
IN_DST = (COL_G_Q, COL_G_K, COL_G_V, COL_G_LR, COL_G_GATE, COL_M_CQ, COL_M_CKV, COL_M_KR,
          COL_M_GATE, COL_C_Q, COL_C_GATE)
N_IN = sum(IN_WIDTHS)
W_IN_COLS = 256


def _w_in_relayout_kernel(w_ref, o_ref):
    o_ref[COL_G_LR:, :] = jnp.zeros((P_COLS - COL_G_LR, o_ref.shape[1]), o_ref.dtype)
    src = 0
    for width, dst in zip(IN_WIDTHS, IN_DST):
        o_ref[dst:dst + width, :] = w_ref[src:src + width, :].astype(o_ref.dtype)
        src += width


def _prep_w_in(w_in):
    k = w_in.shape[0]
    return pl.pallas_call(
        _w_in_relayout_kernel,
        grid=(k // W_IN_COLS,),
        in_specs=[pl.BlockSpec((N_IN, W_IN_COLS), lambda i: (0, i))],
        out_specs=pl.BlockSpec((P_COLS, W_IN_COLS), lambda i: (0, i)),
        out_shape=jax.ShapeDtypeStruct((P_COLS, k), BF16),
        compiler_params=_cparams(1),
        name="w_in_relayout",
    )(w_in.T)


def _prep_gla_gate(w_g2, b_g):
    wgs = []
    for d in range(2):
        w = jnp.zeros((LANES, GLA_HK), F32).at[d * GLA_LR:(d + 1) * GLA_LR].set(w_g2[d])
        wgs.append(w.astype(BF16))
    return wgs, [b_g[0][None, :], b_g[1][None, :]]


def _prep_mla(w_uq, w_ukv):
    wq = w_uq.reshape(MLA_Q_LORA, MLA_HEADS, MLA_NOPE + MLA_ROPE)
    wqn = wq[:, :, :MLA_NOPE].reshape(MLA_Q_LORA, MLA_HEADS * MLA_NOPE).astype(BF16)
    wqr = wq[:, :, MLA_NOPE:].reshape(MLA_Q_LORA, MLA_HEADS * MLA_ROPE).astype(BF16)
    wkv = w_ukv.reshape(MLA_KV_LORA, MLA_HEADS, MLA_NOPE + MLA_V)
    wk = wkv[:, :, :MLA_NOPE].reshape(MLA_KV_LORA, MLA_HEADS * MLA_NOPE).astype(BF16)
    wv = wkv[:, :, MLA_NOPE:].reshape(MLA_KV_LORA, MLA_W).astype(BF16)
    return wqn, wqr, wk, wv


def _rope_tables(t):
    pos = jnp.arange(t, dtype=F32)
    inv = 1.0 / (ROPE_THETA ** (jnp.arange(0, MLA_ROPE, 2, dtype=F32) / MLA_ROPE))
    ang = pos[:, None] * inv[None, :]
    cos, sin = jnp.cos(ang), jnp.sin(ang)
    return (jnp.concatenate([cos, cos, cos, cos], axis=-1),
            jnp.concatenate([-sin, sin, -sin, sin], axis=-1))


def _pick(t, pref):
    return pref if t % pref == 0 else t


ROW_TILE = 512
GLA_BLOCK = 1024
ATTN_Q_TILE = 512 * ATTN_SPLIT
ATTN_KV_TILE = 2048
ATTN_KV_SINGLE = 4096
MEM_KV_TN = 512
OUT_ROW_TILE = 256


def _trunk(x, mem, w):
    bsz, t, _ = x.shape
    m = bsz * t
    xf = x.reshape(m, D_MODEL)
    p_act = _norm_matmul(xf, w["g_in"], w["w_in_t"], _pick(m, ROW_TILE), P_TN,
                         w_transposed=True)

    tb = _pick(t, GLA_BLOCK)
    o_f = _gla_call(p_act, w["wg"][0], w["bg"][0], False, bsz, t, tb)
    o_a = _gla_call(p_act, w["wg"][1], w["bg"][1], True, bsz, t, tb, (o_f, w["g_head"]))

    kv_mem = _norm_matmul(mem.reshape(bsz * N_MEM, D_MODEL), w["mem_g"], w["mem_w_kv"],
                          bsz * N_MEM, MEM_KV_TN)
    cos_t, sin_t = _rope_tables(t)
    q, k, v, o_c = _mla_prep(p_act, kv_mem, w["g_q"], w["g_kv"], w["wqn"], w["wqr"], w["wk"],
                             w["wv"], cos_t, sin_t, t, _pick(t, ROW_TILE))
    bk = t if t <= ATTN_KV_SINGLE else _pick(t, ATTN_KV_TILE)
    o_b = _mla_attn(q, k, v, p_act, bsz, t, _pick(t, ATTN_Q_TILE), bk)

    y = _out_proj(o_a, o_b, o_c, w["w_out"], xf, w["g_final"], _pick(m, OUT_ROW_TILE))
    return y.reshape(bsz, t, D_MODEL)


def kernel(x_prompt, x_sample, mem_prompt, mem_sample, g_in, w_in, gla_w_g2, gla_b_g, gla_g_head,
           mla_g_q, mla_w_uq, mla_g_kv, mla_w_ukv, mem_g, mem_w_kv, w_out, g_final):
    wg, bg = _prep_gla_gate(gla_w_g2[0], gla_b_g[0])
    wqn, wqr, wk, wv = _prep_mla(mla_w_uq[0], mla_w_ukv[0])
    w = {
        "g_in": g_in[0][None, :],
        "w_in_t": _prep_w_in(w_in[0]),
        "wg": wg, "bg": bg,
        "g_head": gla_g_head[0][None, :],
        "g_q": mla_g_q[0][None, :], "g_kv": mla_g_kv[0][None, :],
        "wqn": wqn, "wqr": wqr, "wk": wk, "wv": wv,
        "mem_g": mem_g[0][None, :],
        "mem_w_kv": mem_w_kv[0].astype(BF16),
        "w_out": w_out[0].astype(BF16),
        "g_final": g_final[None, :],
    }
    return (_trunk(x_prompt, mem_prompt, w), _trunk(x_sample, mem_sample, w))
```

```python
import functools
import math

import jax
import jax.numpy as jnp
from jax import lax
from jax.experimental import pallas as pl
from jax.experimental.pallas import tpu as pltpu

F32 = jnp.float32
BF16 = jnp.bfloat16

D_MODEL = 4096
N_MEM = 256
GLA_HEADS = 12
GLA_DK = 64
GLA_DV = 128
GLA_LR = 16
GLA_TAU = 16.0
assert math.log2(GLA_TAU).is_integer()
GLA_CHUNK = 64
GLA_HK = GLA_HEADS * GLA_DK
GLA_W = GLA_HEADS * GLA_DV
MLA_HEADS = 12
MLA_Q_LORA = 1536
MLA_KV_LORA = 512
MLA_NOPE = 128
MLA_ROPE = 64
MLA_V = 128
ROPE_THETA = 10000.0
MLA_W = MLA_HEADS * MLA_V
MLA_QK_PAD = 256
MLA_V_PAD = 256
MEM_HEADS = 4
MEM_DH = 256
MEM_W = MEM_HEADS * MEM_DH
EPS = 1e-6
LANES = 128
LOG2E = 1.4426950408889634

COL_G_V = 0
COL_G_GATE = 1536
COL_M_CQ = 3072
COL_M_GATE = 4608
COL_G_Q = 6144
COL_G_K = 6912
COL_M_CKV = 7680
COL_C_Q = 8192
COL_C_GATE = 9216
COL_G_LR = 10240
COL_M_KR = 10368
P_USED = 10496
P_TN = 1536
P_COLS = 10752

VMEM_LIMIT = 60 * 1024 * 1024


def _cparams(n_axes):
    return pltpu.CompilerParams(
        dimension_semantics=("arbitrary",) * n_axes, vmem_limit_bytes=VMEM_LIMIT)


def _dot(a, b):
    return jnp.dot(a, b, preferred_element_type=F32)


def _dot_nt(a, b):
    return lax.dot_general(a, b, (((1,), (1,)), ((), ())), preferred_element_type=F32)


def _dot_tn(a, b):
    return lax.dot_general(a, b, (((0,), (0,)), ((), ())), preferred_element_type=F32)


def _silu(x):
    return x * jax.nn.sigmoid(x)


NORM_ROWS = 64
BF16_SUBLANES = 16


def _norm_matmul_kernel(w_transposed, chunk, x_ref, g_ref, w_ref, o_ref, h_ref):
    i = pl.program_id(0)
    j = pl.program_id(1)
    tm = x_ref.shape[0]
    dot = _dot_nt if w_transposed else _dot

    def normalise(slot, r0, rows):
        x = x_ref[pl.ds(r0, rows), :]
        ms = jnp.mean(x * x, axis=-1, keepdims=True)
        h_ref[slot, pl.ds(r0, rows), :] = (x * lax.rsqrt(ms + EPS) * g_ref[...]).astype(BF16)

    @pl.when(j == 0)
    def _():
        @pl.when(i == 0)
        def _():
            def rows(c, carry):
                normalise(0, pl.multiple_of(c * NORM_ROWS, NORM_ROWS), NORM_ROWS)
                return carry
            lax.fori_loop(0, tm // NORM_ROWS, rows, 0)

        o_ref[...] = dot(h_ref[i % 2], w_ref[...]).astype(o_ref.dtype)

    @pl.when(j > 0)
    def _():
        r0 = pl.multiple_of(jnp.minimum((j - 1) * chunk, tm - chunk), BF16_SUBLANES)
        o_ref[...] = dot(h_ref[i % 2], w_ref[...]).astype(o_ref.dtype)
        normalise((i + 1) % 2, r0, chunk)


def _norm_matmul(x, g, w, tm, tn, w_transposed=False):
    m, k = x.shape
    n = w.shape[0] if w_transposed else w.shape[1]
    ni, nj = m // tm, n // tn
    chunk = -(-tm // (nj - 1))
    chunk = -(-chunk // BF16_SUBLANES) * BF16_SUBLANES
    assert chunk * (nj - 1) >= tm and (tm - chunk) % BF16_SUBLANES == 0

    def x_map(i, j):
        return (jnp.where((i == 0) & (j == 0), 0, jnp.minimum(i + 1, ni - 1)), 0)

    if w_transposed:
        w_spec = pl.BlockSpec((tn, k), lambda i, j: (j, 0))
    else:
        w_spec = pl.BlockSpec((k, tn), lambda i, j: (0, j))
    return pl.pallas_call(
        functools.partial(_norm_matmul_kernel, w_transposed, chunk),
        grid=(ni, nj),
        in_specs=[
            pl.BlockSpec((tm, k), x_map),
            pl.BlockSpec((1, k), lambda i, j: (0, 0)),
            w_spec,
        ],
        out_specs=pl.BlockSpec((tm, tn), lambda i, j: (i, j)),
        out_shape=jax.ShapeDtypeStruct((m, n), BF16),
        scratch_shapes=[pltpu.VMEM((2, tm, k), BF16)],
        compiler_params=_cparams(2),
        name="norm_matmul",
    )(x, g, w)


def _gla_kernel(rev, nchunk, *refs):
    if rev:
        (q_ref, k_ref, v_ref, lr_ref, wg_ref, bg_ref, gate_ref, of_ref, gh_ref,
         o_ref, s_ref) = refs
    else:
        q_ref, k_ref, v_ref, lr_ref, wg_ref, bg_ref, o_ref, s_ref = refs
    C = GLA_CHUNK

    @pl.when(pl.program_id(1) == 0)
    def _():
        s_ref[...] = jnp.zeros_like(s_ref)

    row = lax.broadcasted_iota(jnp.int32, (C, C), 0)
    col = lax.broadcasted_iota(jnp.int32, (C, C), 1)
    tri = (col >= row) if rev else (col <= row)
    tri_bf = jnp.where(tri, 1.0 / GLA_TAU, 0.0).astype(BF16)
    row2 = lax.broadcasted_iota(jnp.int32, (2 * C, 2 * C), 0)
    col2 = lax.broadcasted_iota(jnp.int32, (2 * C, 2 * C), 1)
    same_head = (row2 < C) == (col2 < C)
    amask = same_head & ((col2 > row2) if rev else (col2 <= row2))
    q_lo = lax.broadcasted_iota(jnp.int32, (C, LANES), 1) < GLA_DK

    npair = GLA_HEADS // 2

    def gate_logits(c):
        rows = pl.ds(c * C, C)
        z = _dot(lr_ref[rows, :], wg_ref[...]) + bg_ref[...]
        lg = jnp.minimum(z, 0.0) - jnp.log(1.0 + jnp.exp(-jnp.abs(z)))
        hi = lg.astype(BF16)
        return hi, (lg - hi.astype(F32)).astype(BF16)

    def decayed_qk(c, hi, lo):
        rows = pl.ds(c * C, C)
        b = _dot(tri_bf, hi) + _dot(tri_bf, lo)
        bl = b[0:1, :] if rev else b[C - 1:C, :]
        q = q_ref[rows, :].astype(F32)
        k = k_ref[rows, :].astype(F32)
        q_in = (q * jnp.exp(b) * (GLA_DK ** -0.5)).astype(BF16)
        k_out = (k * jnp.exp(-b)).astype(BF16)
        k_last = (k * jnp.exp(bl - b)).astype(BF16)
        return q_in, k_out, k_last, jnp.exp(bl)

    def intra(c, prep):
        rows = pl.ds(c * C, C)
        q_in, k_out, k_last, _ = prep
        out = []
        for p in range(npair):
            sl = slice(p * LANES, (p + 1) * LANES)
            ql, ko, kl = q_in[:, sl], k_out[:, sl], k_last[:, sl]
            zero = jnp.zeros_like(ql)
            qs = jnp.concatenate([jnp.where(q_lo, ql, zero), jnp.where(q_lo, zero, ql)], axis=0)
            ks = jnp.concatenate([jnp.where(q_lo, kl, zero), jnp.where(q_lo, zero, kl)], axis=0)
            vs = jnp.concatenate([v_ref[rows, (2 * p + i) * GLA_DV:(2 * p + i + 1) * GLA_DV]
                                  for i in range(2)], axis=0)
            a_raw = _dot_nt(qs, jnp.concatenate([ko, ko], axis=0))
            out.append((qs, vs, a_raw, _dot_tn(ks, vs)))
        return out

    def outputs(c, prep, pairs):
        rows = pl.ds(c * C, C)
        dec = prep[3]
        for p, (qs, vs, a_raw, kv) in enumerate(pairs):
            a = jnp.where(amask, a_raw, 0.0).astype(BF16)
            st = s_ref[p]
            o2 = _dot(jnp.concatenate([a, qs], axis=1),
                      jnp.concatenate([vs, st.astype(BF16)], axis=0))
            dec_col = jnp.broadcast_to(dec[:, p * LANES:(p + 1) * LANES], (LANES, LANES)).T
            s_ref[p] = dec_col * st + kv
            for i in range(2):
                hs = slice((2 * p + i) * GLA_DV, (2 * p + i + 1) * GLA_DV)
                o = o2[i * C:(i + 1) * C]
                if rev:
                    tot = o + of_ref[rows, hs]
                    ms = jnp.mean(tot * tot, axis=-1, keepdims=True)
                    y = tot * lax.rsqrt(ms + EPS) * gh_ref[...]
                    o_ref[rows, hs] = (y * _silu(gate_ref[rows, hs].astype(F32))).astype(o_ref.dtype)
                else:
                    o_ref[rows, hs] = o

    order = [(nchunk - 1 - i) if rev else i for i in range(nchunk)]
    prep = decayed_qk(order[0], *gate_logits(order[0]))
    for i, c in enumerate(order):
        nxt = order[i + 1] if i + 1 < nchunk else None
        if nxt is not None:
            hi_lo = gate_logits(nxt)
        pairs = intra(c, prep)
        if nxt is not None:
            prep_next = decayed_qk(nxt, *hi_lo)
        outputs(c, prep, pairs)
        if nxt is not None:
            prep = prep_next


def _gla_call(p_act, wg, bg, rev, bsz, t, tb, extra=None):
    nblk = t // tb
    nchunk = tb // GLA_CHUNK

    def rowblk(b, i):
        return b * nblk + ((nblk - 1 - i) if rev else i)

    def act(width, colblk):
        return pl.BlockSpec((tb, width), lambda b, i: (rowblk(b, i), colblk))

    def full(shape):
        return pl.BlockSpec(shape, lambda b, i: (0,) * len(shape))

    in_specs = [
        act(GLA_HK, COL_G_Q // GLA_HK),
        act(GLA_HK, COL_G_K // GLA_HK),
        act(GLA_W, COL_G_V // GLA_W),
        act(LANES, COL_G_LR // LANES),
        full(wg.shape),
        full(bg.shape),
    ]
    args = [p_act, p_act, p_act, p_act, wg, bg]
    if rev:
        o_f, g_head = extra
        in_specs += [act(GLA_W, COL_G_GATE // GLA_W), act(GLA_W, 0), full(g_head.shape)]
        args += [p_act, o_f, g_head]
        out_dtype = BF16
    else:
        out_dtype = F32
    return pl.pallas_call(
        functools.partial(_gla_kernel, rev, nchunk),
        grid=(bsz, nblk),
        in_specs=in_specs,
        out_specs=act(GLA_W, 0),
        out_shape=jax.ShapeDtypeStruct((bsz * t, GLA_W), out_dtype),
        scratch_shapes=[pltpu.VMEM((GLA_HEADS // 2, GLA_DV, LANES), F32)],
        compiler_params=_cparams(2),
        name="gla_bwd" if rev else "gla_fwd",
    )(*args)


def _rope(r, cos_t, sin_t):
    half = MLA_ROPE // 2
    lane = lax.broadcasted_iota(jnp.int32, r.shape, 1)
    swapped = jnp.where(lane % MLA_ROPE < half,
                        pltpu.roll(r, LANES - half, 1), pltpu.roll(r, half, 1))
    return r * cos_t + swapped * sin_t


def _mla_prep_kernel(cq_ref, ckv_ref, kr_ref, mq_ref, mgate_ref, kvm_ref, gq_ref, gkv_ref,
                     wqn_ref, wqr_ref, wk_ref, wv_ref, cos_ref, sin_ref,
                     q_ref, k_ref, v_ref, oc_ref):
    mem_scores = _mem_scores(mq_ref, kvm_ref)
    scale = (MLA_NOPE + MLA_ROPE) ** -0.5 * LOG2E
    cos_t = cos_ref[...]
    sin_t = sin_ref[...]

    cq = cq_ref[...].astype(F32)
    ms = jnp.mean(cq * cq, axis=-1, keepdims=True)
    qn = (cq * lax.rsqrt(ms + EPS) * gq_ref[...]).astype(BF16)
    ckv = ckv_ref[...].astype(F32)
    ms = jnp.mean(ckv * ckv, axis=-1, keepdims=True)
    kvn = (ckv * lax.rsqrt(ms + EPS) * gkv_ref[...]).astype(BF16)

    tm = qn.shape[0]
    low = lax.broadcasted_iota(jnp.int32, (tm, LANES), 1) < MLA_ROPE
    k_rope_even = _rope(kr_ref[...].astype(F32), cos_t, sin_t)
    k_rope = (k_rope_even.astype(BF16), pltpu.roll(k_rope_even, MLA_ROPE, 1).astype(BF16))
    k_nope = _dot(kvn, wk_ref[...]).astype(BF16)
    v = _dot(kvn, wv_ref[...]).astype(BF16)
    ones_col = jnp.where(
        lax.broadcasted_iota(jnp.int32, (tm, MLA_V_PAD - MLA_V), 1) == 0, 1.0, 0.0).astype(BF16)
    for h in range(MLA_HEADS):
        c0 = h * MLA_QK_PAD
        k_ref[:, c0:c0 + MLA_NOPE] = k_nope[:, h * MLA_NOPE:(h + 1) * MLA_NOPE]
        k_ref[:, c0 + MLA_NOPE:c0 + MLA_QK_PAD] = k_rope[h % 2]
        v0 = h * MLA_V_PAD
        v_ref[:, v0:v0 + MLA_V] = v[:, h * MLA_V:(h + 1) * MLA_V]
        v_ref[:, v0 + MLA_V:v0 + MLA_V_PAD] = ones_col
    for p in range(MLA_HEADS // 2):
        q2 = _dot(qn, wqn_ref[:, p * 2 * MLA_NOPE:(p + 1) * 2 * MLA_NOPE])
        for i in range(2):
            c0 = (2 * p + i) * MLA_QK_PAD
            q_ref[:, c0:c0 + MLA_NOPE] = (
                q2[:, i * MLA_NOPE:(i + 1) * MLA_NOPE] * scale).astype(BF16)
    for g in range(MLA_HEADS // 4):
        q4 = _dot(qn, wqr_ref[:, g * 4 * MLA_ROPE:(g + 1) * 4 * MLA_ROPE])
        for s in range(2):
            tile = _rope(q4[:, s * LANES:(s + 1) * LANES], cos_t, sin_t) * scale
            zero = jnp.zeros_like(tile)
            c0 = (4 * g + 2 * s) * MLA_QK_PAD + MLA_NOPE
            q_ref[:, c0:c0 + LANES] = jnp.where(low, tile, zero).astype(BF16)
            c1 = c0 + MLA_QK_PAD
            q_ref[:, c1:c1 + LANES] = jnp.where(low, zero, tile).astype(BF16)
    _mem_outputs(mem_scores, mgate_ref, kvm_ref, oc_ref)


def _mla_prep(p_act, kv_mem, g_q, g_kv, wqn, wqr, wk, wv, cos_t, sin_t, t, tm):
    m = p_act.shape[0]
    nt = t // tm

    def act(width, colblk):
        return pl.BlockSpec((tm, width), lambda i: (i, colblk))

    def full(shape):
        return pl.BlockSpec(shape, lambda i: (0,) * len(shape))

    table = pl.BlockSpec((tm, LANES), lambda i: (i % nt, 0))
    qk_w = MLA_HEADS * MLA_QK_PAD
    return pl.pallas_call(
        _mla_prep_kernel,
        grid=(m // tm,),
        in_specs=[
            act(MLA_Q_LORA, COL_M_CQ // MLA_Q_LORA),
            act(MLA_KV_LORA, COL_M_CKV // MLA_KV_LORA),
            act(LANES, COL_M_KR // LANES),
            act(MEM_W, COL_C_Q // MEM_W),
            act(MEM_W, COL_C_GATE // MEM_W),
            pl.BlockSpec((N_MEM, 2 * MEM_W), lambda i: (i // nt, 0)),
            full(g_q.shape), full(g_kv.shape), full(wqn.shape), full(wqr.shape),
            full(wk.shape), full(wv.shape),
            table, table,
        ],
        out_specs=[act(qk_w, 0), act(qk_w, 0), act(MLA_HEADS * MLA_V_PAD, 0), act(MEM_W, 0)],
        out_shape=[
            jax.ShapeDtypeStruct((m, qk_w), BF16),
            jax.ShapeDtypeStruct((m, qk_w), BF16),
            jax.ShapeDtypeStruct((m, MLA_HEADS * MLA_V_PAD), BF16),
            jax.ShapeDtypeStruct((m, MEM_W), BF16),
        ],
        compiler_params=_cparams(1),
        name="mla_prep",
    )(p_act, p_act, p_act, p_act, p_act, kv_mem, g_q, g_kv, wqn, wqr, wk, wv, cos_t, sin_t)


ATTN_SPLIT = 2


def _mla_attn_kernel(nkv, bk, q_ref, k_ref, v_ref, gate_ref, o_ref, s_ref):
    tq = q_ref.shape[0] // ATTN_SPLIT
    q = [q_ref[h * tq:(h + 1) * tq, :] for h in range(ATTN_SPLIT)]

    def scores(h, j):
        s_ref[2 * h + j % 2] = _dot_nt(q[h], k_ref[j * bk:(j + 1) * bk, :])

    m = [jnp.full((tq, 1), -jnp.inf, F32) for _ in range(ATTN_SPLIT)]
    acc = [jnp.zeros((tq, MLA_V_PAD), F32) for _ in range(ATTN_SPLIT)]
    for h in range(ATTN_SPLIT):
        scores(h, 0)
    for j in range(nkv):
        for h in range(ATTN_SPLIT):
            if j + 1 < nkv:
                scores(h, j + 1)
            s = s_ref[2 * h + j % 2]
            m_new = jnp.maximum(m[h], jnp.max(s, axis=-1, keepdims=True))
            alpha = jnp.exp2(m[h] - m_new)
            p = jnp.exp2(s - m_new).astype(BF16)
            acc[h] = alpha * acc[h] + _dot(p, v_ref[j * bk:(j + 1) * bk, :])
            m[h] = m_new
    for h in range(ATTN_SPLIT):
        rows = slice(h * tq, (h + 1) * tq)
        o = acc[h][:, :MLA_V] / acc[h][:, MLA_V:MLA_V + 1]
        o_ref[rows, :] = (o * _silu(gate_ref[rows, :].astype(F32))).astype(o_ref.dtype)


def _mla_attn(q, k, v, p_act, bsz, t, tq, bk):
    nq = t // tq
    nkv = t // bk
    gate_col0 = COL_M_GATE // MLA_V
    return pl.pallas_call(
        functools.partial(_mla_attn_kernel, nkv, bk),
        grid=(bsz, MLA_HEADS, nq),
        in_specs=[
            pl.BlockSpec((tq, MLA_QK_PAD), lambda b, h, i: (b * nq + i, h)),
            pl.BlockSpec((t, MLA_QK_PAD), lambda b, h, i: (b, h)),
            pl.BlockSpec((t, MLA_V_PAD), lambda b, h, i: (b, h)),
            pl.BlockSpec((tq, MLA_V), lambda b, h, i: (b * nq + i, gate_col0 + h)),
        ],
        out_specs=pl.BlockSpec((tq, MLA_V), lambda b, h, i: (b * nq + i, h)),
        out_shape=jax.ShapeDtypeStruct((bsz * t, MLA_W), BF16),
        scratch_shapes=[pltpu.VMEM((2 * ATTN_SPLIT, tq // ATTN_SPLIT, bk), F32)],
        compiler_params=_cparams(3),
        name="mla_attn",
    )(q, k, v, p_act)


_MEM_HEAD_COLS = [slice(h * MEM_DH, (h + 1) * MEM_DH) for h in range(MEM_HEADS)]


def _mem_scores(q_ref, kv_ref):
    return [_dot_nt(q_ref[:, hs], kv_ref[:, hs]) for hs in _MEM_HEAD_COLS]


def _mem_outputs(scores, gate_ref, kv_ref, o_ref):
    for h, hs in enumerate(_MEM_HEAD_COLS):
        s = scores[h] * (MEM_DH ** -0.5)
        p = jnp.exp(s - jnp.max(s, axis=-1, keepdims=True))
        l = jnp.sum(p, axis=-1, keepdims=True)
        o = _dot(p.astype(BF16), kv_ref[:, MEM_W + h * MEM_DH:MEM_W + (h + 1) * MEM_DH]) / l
        o_ref[:, hs] = (o * _silu(gate_ref[:, hs].astype(F32))).astype(o_ref.dtype)


OUT_TN = 1024


def _out_kernel(oa_ref, ob_ref, oc_ref, w_ref, x_ref, g_ref, y_ref):
    ssq = jnp.zeros((y_ref.shape[0], 1), F32)
    for c0 in range(0, D_MODEL, OUT_TN):
        cs = slice(c0, c0 + OUT_TN)
        acc = (_dot(oa_ref[...], w_ref[0:GLA_W, cs])
               + _dot(ob_ref[...], w_ref[GLA_W:GLA_W + MLA_W, cs])
               + _dot(oc_ref[...], w_ref[GLA_W + MLA_W:, cs]) + x_ref[:, cs])
        ssq = ssq + jnp.sum(acc * acc, axis=-1, keepdims=True)
        y_ref[:, cs] = acc
    r = lax.rsqrt(ssq * (1.0 / D_MODEL) + EPS)
    for c0 in range(0, D_MODEL, OUT_TN):
        cs = slice(c0, c0 + OUT_TN)
        y_ref[:, cs] = y_ref[:, cs] * r * g_ref[:, cs]


def _out_proj(o_a, o_b, o_c, w_out, x, g_final, tm):
    m = x.shape[0]
    return pl.pallas_call(
        _out_kernel,
        grid=(m // tm,),
        in_specs=[
            pl.BlockSpec((tm, GLA_W), lambda i: (i, 0)),
            pl.BlockSpec((tm, MLA_W), lambda i: (i, 0)),
            pl.BlockSpec((tm, MEM_W), lambda i: (i, 0)),
            pl.BlockSpec((D_MODEL, D_MODEL), lambda i: (0, 0), pipeline_mode=pl.Buffered(1)),
            pl.BlockSpec((tm, D_MODEL), lambda i: (i, 0)),
            pl.BlockSpec((1, D_MODEL), lambda i: (0, 0)),
        ],
        out_specs=pl.BlockSpec((tm, D_MODEL), lambda i: (i, 0)),
        out_shape=jax.ShapeDtypeStruct((m, D_MODEL), F32),
        compiler_params=_cparams(1),
        name="out_proj",
    )(o_a, o_b, o_c, w_out, x, g_final)


IN_WIDTHS = (GLA_HK, GLA_HK, GLA_W, 2 * GLA_LR, GLA_W, MLA_Q_LORA, MLA_KV_LORA, MLA_ROPE,
             MLA_W, MEM_W, MEM_W)
IN_DST = (COL_G_Q, COL_G_K, COL_G_V, COL_G_LR, COL_G_GATE, COL_M_CQ, COL_M_CKV, COL_M_KR,
          COL_M_GATE, COL_C_Q, COL_C_GATE)
N_IN = sum(IN_WIDTHS)
W_IN_COLS = 256


def _w_in_relayout_kernel(w_ref, o_ref):
    o_ref[COL_G_LR:, :] = jnp.zeros((P_COLS - COL_G_LR, o_ref.shape[1]), o_ref.dtype)
    src = 0
    for width, dst in zip(IN_WIDTHS, IN_DST):
        o_ref[dst:dst + width, :] = w_ref[src:src + width, :].astype(o_ref.dtype)
        src += width


def _prep_w_in(w_in):
    k = w_in.shape[0]
    return pl.pallas_call(
        _w_in_relayout_kernel,
        grid=(k // W_IN_COLS,),
        in_specs=[pl.BlockSpec((N_IN, W_IN_COLS), lambda i: (0, i))],
        out_specs=pl.BlockSpec((P_COLS, W_IN_COLS), lambda i: (0, i)),
        out_shape=jax.ShapeDtypeStruct((P_COLS, k), BF16),
        compiler_params=_cparams(1),
        name="w_in_relayout",
    )(w_in.T)


def _prep_gla_gate(w_g2, b_g):
    wgs = []
    for d in range(2):
        w = jnp.zeros((LANES, GLA_HK), F32).at[d * GLA_LR:(d + 1) * GLA_LR].set(w_g2[d])
        wgs.append(w.astype(BF16))
    return wgs, [b_g[0][None, :], b_g[1][None, :]]


def _prep_mla(w_uq, w_ukv):
    wq = w_uq.reshape(MLA_Q_LORA, MLA_HEADS, MLA_NOPE + MLA_ROPE)
    wqn = wq[:, :, :MLA_NOPE].reshape(MLA_Q_LORA, MLA_HEADS * MLA_NOPE).astype(BF16)
    wqr = wq[:, :, MLA_NOPE:].reshape(MLA_Q_LORA, MLA_HEADS * MLA_ROPE).astype(BF16)
    wkv = w_ukv.reshape(MLA_KV_LORA, MLA_HEADS, MLA_NOPE + MLA_V)
    wk = wkv[:, :, :MLA_NOPE].reshape(MLA_KV_LORA, MLA_HEADS * MLA_NOPE).astype(BF16)
    wv = wkv[:, :, MLA_NOPE:].reshape(MLA_KV_LORA, MLA_W).astype(BF16)
    return wqn, wqr, wk, wv


def _rope_tables(t):
    pos = jnp.arange(t, dtype=F32)
    inv = 1.0 / (ROPE_THETA ** (jnp.arange(0, MLA_ROPE, 2, dtype=F32) / MLA_ROPE))
    ang = pos[:, None] * inv[None, :]
    cos, sin = jnp.cos(ang), jnp.sin(ang)
    return (jnp.concatenate([cos, cos, cos, cos], axis=-1),
            jnp.concatenate([-sin, sin, -sin, sin], axis=-1))


def _pick(t, pref):
    return pref if t % pref == 0 else t


ROW_TILE = 512
GLA_BLOCK = 1024
ATTN_Q_TILE = 512 * ATTN_SPLIT
ATTN_KV_TILE = 2048
ATTN_KV_SINGLE = 4096
MEM_KV_TN = 512
OUT_ROW_TILE = 256


def _trunk(x, kv_mem, w):
    bsz, t, _ = x.shape
    m = bsz * t
    xf = x.reshape(m, D_MODEL)
    p_act = _norm_matmul(xf, w["g_in"], w["w_in_t"], _pick(m, ROW_TILE), P_TN,
                         w_transposed=True)

    tb = _pick(t, GLA_BLOCK)
    o_f = _gla_call(p_act, w["wg"][0], w["bg"][0], False, bsz, t, tb)
    o_a = _gla_call(p_act, w["wg"][1], w["bg"][1], True, bsz, t, tb, (o_f, w["g_head"]))

    cos_t, sin_t = _rope_tables(t)
    q, k, v, o_c = _mla_prep(p_act, kv_mem, w["g_q"], w["g_kv"], w["wqn"], w["wqr"], w["wk"],
                             w["wv"], cos_t, sin_t, t, _pick(t, ROW_TILE))
    bk = t if t <= ATTN_KV_SINGLE else _pick(t, ATTN_KV_TILE)
    o_b = _mla_attn(q, k, v, p_act, bsz, t, _pick(t, ATTN_Q_TILE), bk)

    y = _out_proj(o_a, o_b, o_c, w["w_out"], xf, w["g_final"], _pick(m, OUT_ROW_TILE))
    return y.reshape(bsz, t, D_MODEL)


def kernel(x_prompt, x_sample, mem_prompt, mem_sample, g_in, w_in, gla_w_g2, gla_b_g, gla_g_head,
           mla_g_q, mla_w_uq, mla_g_kv, mla_w_ukv, mem_g, mem_w_kv, w_out, g_final):
    wg, bg = _prep_gla_gate(gla_w_g2[0], gla_b_g[0])
    wqn, wqr, wk, wv = _prep_mla(mla_w_uq[0], mla_w_ukv[0])
    w = {
        "g_in": g_in[0][None, :],
        "w_in_t": _prep_w_in(w_in[0]),
        "wg": wg, "bg": bg,
        "g_head": gla_g_head[0][None, :],
        "g_q": mla_g_q[0][None, :], "g_kv": mla_g_kv[0][None, :],
        "wqn": wqn, "wqr": wqr, "wk": wk, "wv": wv,
        "mem_g": mem_g[0][None, :],
        "mem_w_kv": mem_w_kv[0].astype(BF16),
        "w_out": w_out[0].astype(BF16),
        "g_final": g_final[None, :],
    }
    rows_p = mem_prompt.shape[0] * N_MEM
    rows_s = mem_sample.shape[0] * N_MEM
    assert rows_p == rows_s
    mem_all = jnp.concatenate([mem_prompt.reshape(rows_p, D_MODEL),
                               mem_sample.reshape(rows_s, D_MODEL)], axis=0)
    kv_all = _norm_matmul(mem_all, w["mem_g"], w["mem_w_kv"], rows_p, MEM_KV_TN)
    return (_trunk(x_prompt, kv_all[:rows_p], w), _trunk(x_sample, kv_all[rows_p:], w))
```
